```python
import math, functools
import jax, jax.numpy as jnp
from jax import lax
import numpy as np

D_MODEL = 1024
BATCH = 8
SEQ = 2048
DEPTH = 2
DEC_BATCH = 128
DEC_SEQ = 1
PAST_LEN = 16384
PAGE_SIZE = 128

MIX_W = D_MODEL
RWKV_W = MIX_W // 2
CONV_W = MIX_W - RWKV_W
HEAD_SIZE = 64
N_RWKV_HEADS = RWKV_W // HEAD_SIZE
DECAY_LORA = 64
AAA_LORA = 64
MV_LORA = 32
GATE_LORA = 128
CONV_K = 3
N_MEM = 256
N_XHEADS = 4
XHEAD_DIM = D_MODEL // N_XHEADS
FFN_W = ((8 * D_MODEL // 3 + 127) // 128) * 128
FFN_K = 3
EPS = 1e-6
GN_EPS = HEAD_SIZE * 1e-5
RWKV_COLS = 3 * RWKV_W + DECAY_LORA + AAA_LORA + GATE_LORA
IN_COLS = RWKV_COLS + 3 * CONV_W
RWKV_SPLITS = (RWKV_W, 2 * RWKV_W, 3 * RWKV_W, 3 * RWKV_W + DECAY_LORA, 3 * RWKV_W + DECAY_LORA + AAA_LORA)

kernel_name = 'hymba_rwkv7_shortconv_convffn_memxattn_step'


def rmsnorm(x, g):
    xf = x.astype(jnp.float32)
    y = xf * lax.rsqrt(jnp.mean(xf * xf, axis=-1, keepdims=True) + EPS)
    return (y * g.astype(jnp.float32)).astype(x.dtype)


def causal_dwconv(u, prev, w):
    K = w.shape[0]
    T = u.shape[1]
    ext = jnp.concatenate([prev.astype(u.dtype), u], axis=1)
    out = w[0] * ext[:, 0:T]
    for j in range(1, K):
        out = out + w[j] * ext[:, j:j + T]
    return out, ext[:, -(K - 1):]


def wkv_scan(S0, r, decay, k, v, a, b):
    def step(S, inp):
        r_t, d_t, k_t, v_t, a_t, b_t = inp
        Sa = jnp.einsum('bhvk,bhk->bhv', S, a_t)
        S = S * d_t[:, :, None, :] + Sa[..., :, None] * b_t[..., None, :] + v_t[..., :, None] * k_t[..., None, :]
        return S, jnp.einsum('bhvk,bhk->bhv', S, r_t)
    xs = tuple(jnp.swapaxes(t, 0, 1) for t in (r, decay, k, v, a, b))
    S, ys = lax.scan(step, S0, xs)
    return S, jnp.swapaxes(ys, 0, 1)


def rwkv7_group(p_cur, p_prev, S0, v_first, l, P):
    B, T, _ = p_cur.shape
    f32 = jnp.float32
    p_shift = jnp.concatenate([p_prev.astype(p_cur.dtype), p_cur[:, :-1]], axis=1)
    z = p_cur + (p_shift - p_cur) * P['mu_shift'][l]
    r, k, v, wl, al, gl = jnp.split(z, RWKV_SPLITS, axis=-1)
    w = -jax.nn.softplus(-(P['w0'][l] + jnp.tanh(wl) @ P['w2'][l]).astype(f32)) - 0.5
    decay = jnp.exp(-jnp.exp(w))
    a = jax.nn.sigmoid((P['a0'][l] + al @ P['a2'][l]).astype(f32))
    g = jax.nn.sigmoid(gl) @ P['g2'][l]
    if v_first is None:
        v_first = v
    else:
        vmix = jax.nn.sigmoid(P['v0'][l - 1] + (v @ P['v1'][l - 1]) @ P['v2'][l - 1])
        v = v + (v_first - v) * vmix
    hs = lambda t: t.reshape(B, T, N_RWKV_HEADS, HEAD_SIZE).astype(f32)
    ph = lambda t: t.reshape(N_RWKV_HEADS, HEAD_SIZE).astype(f32)
    r_h, k_h, v_h, a_h, d_h = hs(r), hs(k), hs(v), hs(a), hs(decay)
    kk = k_h * ph(P['k_k'][l])
    kk = kk / jnp.maximum(jnp.sqrt(jnp.sum(kk * kk, axis=-1, keepdims=True)), 1e-12)
    k_h = k_h * (1.0 + (a_h - 1.0) * ph(P['k_a'][l]))
    S, y = wkv_scan(S0.astype(f32), r_h, d_h, k_h, v_h, -kk, kk * a_h)
    mean = jnp.mean(y, axis=-1, keepdims=True)
    var = jnp.mean(jnp.square(y - mean), axis=-1, keepdims=True)
    y = (y - mean) * lax.rsqrt(var + GN_EPS) * ph(P['ln_x_w'][l]) + ph(P['ln_x_b'][l])
    y = y + jnp.sum(r_h * k_h * ph(P['r_k'][l]), axis=-1, keepdims=True) * v_h
    out = y.reshape(B, T, RWKV_W).astype(p_cur.dtype) * g
    return out, S, v_first


def layer(x, l, mem_k, mem_v, shift_prev, wkv_prev, conv_prev, ffn_prev, v_first, P):
    B, T, _ = x.shape
    xn = rmsnorm(x, P['norm_mix'][l])
    w_in = P['w_in'][l]
    proj = xn @ w_in
    p_rwkv, p_conv = proj[..., :RWKV_COLS], proj[..., RWKV_COLS:]
    p_prev = (shift_prev.astype(xn.dtype) @ w_in[:, :RWKV_COLS])[:, None, :]
    y_a, wkv_new, v_first = rwkv7_group(p_rwkv, p_prev, wkv_prev, v_first, l, P)
    gate_b, gate_c, h_in = jnp.split(p_conv, 3, axis=-1)
    conv_out, conv_new = causal_dwconv(gate_c * h_in, conv_prev, P['conv_w'][l])
    y_b = gate_b * conv_out
    x = x + jnp.concatenate([y_a, y_b], axis=-1) @ P['w_out'][l]
    xq = rmsnorm(x, P['norm_x'][l])
    q = (xq @ P['wq'][l]).reshape(B, T, N_XHEADS, XHEAD_DIM)
    s = jnp.einsum('bthd,bmhd->bhtm', q, mem_k.astype(q.dtype)).astype(jnp.float32) * (XHEAD_DIM ** -0.5)
    p = jax.nn.softmax(s, axis=-1).astype(x.dtype)
    o = jnp.einsum('bhtm,bmhd->bthd', p, mem_v.astype(x.dtype)).reshape(B, T, D_MODEL)
    x = x + o @ P['wo'][l]
    xf = rmsnorm(x, P['norm_ffn'][l])
    up = xf @ P['w_up'][l]
    upc, ffn_new = causal_dwconv(up, ffn_prev, P['ffn_conv_w'][l])
    u, gt = jnp.split(upc, 2, axis=-1)
    x = x + (jax.nn.silu(gt) * u) @ P['w_down'][l]
    return x, xn[:, -1], wkv_new, conv_new, ffn_new, v_first


def trunk(x, mem_k, mem_v, shift0, wkv0, conv0, ffn0, P):
    v_first = None
    shs, wks, cvs, ffs = [], [], [], []
    for l in range(DEPTH):
        x, sh, S, cs, fs, v_first = layer(x, l, mem_k[l], mem_v[l], shift0[l], wkv0[l], conv0[l], ffn0[l], v_first, P)
        shs.append(sh)
        wks.append(S)
        cvs.append(cs)
        ffs.append(fs)
    y = rmsnorm(x, P['norm_final'])
    return y, jnp.stack(shs), jnp.stack(wks), jnp.stack(cvs), jnp.stack(ffs)


def setup_inputs(seed: int = 0) -> dict:
    key = jax.random.key(seed)
    ks = iter(jax.random.split(key, 48))
    nrm = lambda shape, scale: jax.random.normal(next(ks), shape, jnp.float32) * scale
    L = DEPTH
    D = D_MODEL
    H, N = N_RWKV_HEADS, HEAD_SIZE
    d = {}
    d['x_prompt'] = nrm((BATCH, SEQ, D), 1.0)
    d['x_sample'] = nrm((DEC_BATCH, DEC_SEQ, D), 1.0)
    d['mem_prompt'] = nrm((BATCH, N_MEM, D), 1.0)
    d['state_shift'] = nrm((L, DEC_BATCH, D), 1.0)
    d['state_wkv'] = nrm((L, DEC_BATCH, H, N, N), 0.3)
    d['state_conv'] = nrm((L, DEC_BATCH, CONV_K - 1, CONV_W), 1.0)
    d['state_ffn'] = nrm((L, DEC_BATCH, FFN_K - 1, 2 * FFN_W), 1.0)
    d['cache_mem_k'] = nrm((L, DEC_BATCH, N_MEM, N_XHEADS, XHEAD_DIM), 1.0)
    d['cache_mem_v'] = nrm((L, DEC_BATCH, N_MEM, N_XHEADS, XHEAD_DIM), 1.0)
    d['norm_mix'] = 1.0 + nrm((L, D), 0.02)
    d['w_in'] = nrm((L, D, IN_COLS), D ** -0.5)
    d['mu_shift'] = jax.random.uniform(next(ks), (L, RWKV_COLS), jnp.float32)
    d['w0'] = jax.random.uniform(next(ks), (L, RWKV_W), jnp.float32, -2.0, 1.0)
    d['w2'] = nrm((L, DECAY_LORA, RWKV_W), 0.1)
    d['a0'] = nrm((L, RWKV_W), 0.1)
    d['a2'] = nrm((L, AAA_LORA, RWKV_W), 0.1)
    d['g2'] = nrm((L, GATE_LORA, RWKV_W), GATE_LORA ** -0.5)
    d['v0'] = nrm((L - 1, RWKV_W), 0.1)
    d['v1'] = nrm((L - 1, RWKV_W, MV_LORA), RWKV_W ** -0.5)
    d['v2'] = nrm((L - 1, MV_LORA, RWKV_W), 0.1)
    d['k_k'] = 0.85 + nrm((L, RWKV_W), 0.02)
    d['k_a'] = 1.0 + nrm((L, RWKV_W), 0.02)
    d['r_k'] = nrm((L, RWKV_W), 0.1)
    d['ln_x_w'] = 1.0 + nrm((L, RWKV_W), 0.02)
    d['ln_x_b'] = nrm((L, RWKV_W), 0.02)
    d['conv_w'] = nrm((L, CONV_K, CONV_W), CONV_K ** -0.5)
    d['w_out'] = nrm((L, MIX_W, D), MIX_W ** -0.5)
    d['norm_x'] = 1.0 + nrm((L, D), 0.02)
    d['norm_mem'] = 1.0 + nrm((L, D), 0.02)
    d['wq'] = nrm((L, D, D), D ** -0.5)
    d['wk'] = nrm((L, D, D), D ** -0.5)
    d['wv'] = nrm((L, D, D), D ** -0.5)
    d['wo'] = nrm((L, D, D), D ** -0.5)
    d['norm_ffn'] = 1.0 + nrm((L, D), 0.02)
    d['w_up'] = nrm((L, D, 2 * FFN_W), D ** -0.5)
    d['ffn_conv_w'] = nrm((L, FFN_K, 2 * FFN_W), FFN_K ** -0.5)
    d['w_down'] = nrm((L, FFN_W, D), FFN_W ** -0.5)
    d['norm_final'] = 1.0 + nrm((D,), 0.02)
    return d


def reference(x_prompt, x_sample, mem_prompt, state_shift, state_wkv, state_conv, state_ffn,
              cache_mem_k, cache_mem_v, norm_mix, w_in, mu_shift, w0, w2, a0, a2, g2, v0, v1, v2,
              k_k, k_a, r_k, ln_x_w, ln_x_b, conv_w, w_out, norm_x, norm_mem, wq, wk, wv, wo,
              norm_ffn, w_up, ffn_conv_w, w_down, norm_final):
    P = dict(norm_mix=norm_mix, w_in=w_in, mu_shift=mu_shift, w0=w0, w2=w2, a0=a0, a2=a2, g2=g2,
             v0=v0, v1=v1, v2=v2, k_k=k_k, k_a=k_a, r_k=r_k, ln_x_w=ln_x_w, ln_x_b=ln_x_b,
             conv_w=conv_w, w_out=w_out, norm_x=norm_x, wq=wq, wo=wo, norm_ffn=norm_ffn,
             w_up=w_up, ffn_conv_w=ffn_conv_w, w_down=w_down, norm_final=norm_final)
    Bp, Mp = mem_prompt.shape[0], mem_prompt.shape[1]
    mks, mvs = [], []
    for l in range(DEPTH):
        mn = rmsnorm(mem_prompt, norm_mem[l])
        mks.append((mn @ wk[l]).reshape(Bp, Mp, N_XHEADS, XHEAD_DIM))
        mvs.append((mn @ wv[l]).reshape(Bp, Mp, N_XHEADS, XHEAD_DIM))
    mem_k_p = jnp.stack(mks)
    mem_v_p = jnp.stack(mvs)
    dt = x_prompt.dtype
    shift0 = jnp.zeros((DEPTH, Bp, D_MODEL), dt)
    wkv0 = jnp.zeros((DEPTH, Bp, N_RWKV_HEADS, HEAD_SIZE, HEAD_SIZE), jnp.float32)
    conv0 = jnp.zeros((DEPTH, Bp, CONV_K - 1, CONV_W), dt)
    ffn0 = jnp.zeros((DEPTH, Bp, FFN_K - 1, 2 * FFN_W), dt)
    y_prompt, shift_p, wkv_p, conv_p, ffn_p = trunk(x_prompt, mem_k_p, mem_v_p, shift0, wkv0, conv0, ffn0, P)
    y_sample, shift_s, wkv_s, conv_s, ffn_s = trunk(x_sample, cache_mem_k, cache_mem_v, state_shift, state_wkv, state_conv, state_ffn, P)
    return (y_prompt, y_sample, shift_p, wkv_p, conv_p, ffn_p, mem_k_p, mem_v_p, shift_s, wkv_s, conv_s, ffn_s)
```

```python
import functools

import jax
import jax.numpy as jnp
from jax import lax
from jax.experimental import pallas as pl
from jax.experimental.pallas import tpu as pltpu

F32 = jnp.float32
BF16 = jnp.bfloat16
HIGHEST = lax.Precision.HIGHEST

RMS_EPS = 1e-6
GROUPNORM_EPS_PER_CHANNEL = 1e-5
KK_NORM_FLOOR = 1e-12

ROW_TILE = 256
CHUNK = 64
ATTN_BATCH_TILE = 8
WKV_BATCH_TILE = 8
FFN_COL_TILE = 256
V7X_VMEM_LIMIT_BYTES = 56 * 1024 * 1024


def _rms(x, g):
    ms = jnp.mean(x * x, axis=-1, keepdims=True)
    return x * lax.rsqrt(ms + RMS_EPS) * g


def _bdot(a, b):
    return jnp.dot(a.astype(BF16), b.astype(BF16), preferred_element_type=F32)


def _hdot(a, b, dims=(((1,), (0,)), ((), ()))):
    return lax.dot_general(a, b, dims, precision=HIGHEST, preferred_element_type=F32)


def _sigmoid(x):
    return 1.0 / (1.0 + jnp.exp(-x))


def _softplus(x):
    return jnp.maximum(x, 0.0) + jnp.log1p(jnp.exp(-jnp.abs(x)))


def _shift_rows(x, prev_rows):
    s = len(prev_rows)
    rolled = pltpu.roll(x, s, axis=0)
    row = lax.broadcasted_iota(jnp.int32, x.shape, 0)
    for i, pr in enumerate(prev_rows):
        rolled = jnp.where(row == i, pr, rolled)
    return rolled


def _mix_rows(p_cur, p_shift, mu, w0, a0, w_lora, lora_dims, rw, vmix):
    d_decay, d_aaa, _ = lora_dims
    z = p_cur + (p_shift - p_cur) * mu
    r = z[:, 0:rw]
    k = z[:, rw:2 * rw]
    v = z[:, 2 * rw:3 * rw]
    lo = z[:, 3 * rw:]
    lane = lax.broadcasted_iota(jnp.int32, lo.shape, 1)
    feat = jnp.where(lane < d_decay, jnp.tanh(lo),
                     jnp.where(lane < d_decay + d_aaa, lo, _sigmoid(lo)))
    lora = _bdot(feat, w_lora)
    w = -_softplus(-(w0 + lora[:, 0:rw])) - 0.5
    log_decay = -jnp.exp(w)
    a_sig = _sigmoid(a0 + lora[:, rw:2 * rw])
    gate = lora[:, 2 * rw:3 * rw]
    if vmix is not None:
        v_first, v0, v1, v2 = vmix
        vm = _sigmoid(v0 + _bdot(_bdot(v, v1), v2))
        v = v + (v_first - v) * vm
    return r, k, v, a_sig, log_decay, gate


def _group_norm_bonus(y, r, kmod, v, rk, lnw, lnb, n):
    mean = jnp.mean(y, axis=-1, keepdims=True)
    yc = y - mean
    var = jnp.mean(yc * yc, axis=-1, keepdims=True)
    y = yc * lax.rsqrt(var + n * GROUPNORM_EPS_PER_CHANNEL) * lnw + lnb
    return y + jnp.sum(r * kmod * rk, axis=-1, keepdims=True) * v


def _key_features(k, a_sig, kk_w, ka_w):
    kk = k * kk_w
    nrm = jnp.sqrt(jnp.sum(kk * kk, axis=-1, keepdims=True))
    kk = kk / jnp.maximum(nrm, KK_NORM_FLOOR)
    kmod = k * (1.0 + (a_sig - 1.0) * ka_w)
    return kk, kmod


def _softmax_rows(s):
    m = jnp.max(s, axis=-1, keepdims=True)
    e = jnp.exp(s - m)
    return e / jnp.sum(e, axis=-1, keepdims=True)


def _mix_in_seq_kernel(has_vmix, lora_dims, n_heads, *refs):
    it = iter(refs)
    x_ref, gmix_ref, win_ref, sprev_ref, cprev_ref, mu_ref, w0_ref, a0_ref, wl_ref, cw_ref = (
        next(it) for _ in range(10))
    if has_vmix:
        vf_ref, v0_ref, v1_ref, v2_ref = (next(it) for _ in range(4))
    r_o, k_o, v_o, a_o, ld_o, g_o, yb_o = (next(it) for _ in range(7))
    vflat_o = None if has_vmix else next(it)
    xn_o, cnew_o, pc_scr, uc_scr = (next(it) for _ in range(4))

    t = pl.program_id(1)
    tm = x_ref.shape[0]
    rc = mu_ref.shape[1]
    cw = cw_ref.shape[1]
    rw = w0_ref.shape[1]
    hs = rw // n_heads

    @pl.when(t == 0)
    def _():
        sprev = jnp.broadcast_to(sprev_ref[...], (8, sprev_ref.shape[1]))
        pc_scr[...] = _bdot(sprev, win_ref[:, 0:rc])
        uc_scr[0:2, :] = cprev_ref[...]

    xn = _rms(x_ref[...], gmix_ref[...])
    proj = _bdot(xn, win_ref[...])
    p_cur = proj[:, 0:rc]
    p_shift = _shift_rows(p_cur, [pc_scr[0:1, :]])
    vmix = None
    if has_vmix:
        vmix = (vf_ref[...], v0_ref[...], v1_ref[...], v2_ref[...])
    r, k, v, a_sig, log_decay, gate = _mix_rows(
        p_cur, p_shift, mu_ref[...], w0_ref[...], a0_ref[...], wl_ref[...], lora_dims, rw, vmix)
    for h in range(n_heads):
        sl = slice(h * hs, (h + 1) * hs)
        r_o[h] = r[:, sl]
        k_o[h] = k[:, sl]
        v_o[h] = v[:, sl]
        a_o[h] = a_sig[:, sl]
        ld_o[h] = log_decay[:, sl]
    g_o[...] = gate
    if vflat_o is not None:
        vflat_o[...] = v

    gate_b = proj[:, rc:rc + cw]
    gate_c = proj[:, rc + cw:rc + 2 * cw]
    h_in = proj[:, rc + 2 * cw:rc + 3 * cw]
    u = gate_c * h_in
    u1 = _shift_rows(u, [uc_scr[1:2, :]])
    u2 = _shift_rows(u, [uc_scr[0:1, :], uc_scr[1:2, :]])
    conv = cw_ref[0:1, :] * u2 + cw_ref[1:2, :] * u1 + cw_ref[2:3, :] * u
    yb_o[...] = gate_b * conv

    pc_scr[0:1, :] = p_cur[tm - 1:tm, :]
    uc_scr[0:2, :] = u[tm - 2:tm, :]

    @pl.when(t == pl.num_programs(1) - 1)
    def _():
        xn_o[...] = xn[tm - 1:tm, :]
        cnew_o[...] = u[tm - 2:tm, :]


def _wkv_seq_kernel(r_ref, k_ref, v_ref, a_ref, ld_ref, kkw_ref, kaw_ref, rkw_ref, lnw_ref, lnb_ref,
                    s0_ref, y_o, s_o, st_scr):
    t = pl.program_id(2)
    tt, n = r_ref.shape
    c = CHUNK

    @pl.when(t == 0)
    def _():
        st_scr[...] = s0_ref[...].T

    row = lax.broadcasted_iota(jnp.int32, (c, c), 0)
    col = lax.broadcasted_iota(jnp.int32, (c, c), 1)
    tril_incl = (row >= col)
    tril_strict = (row > col)
    ones_tril = tril_incl.astype(F32)
    eye = (row == col).astype(F32)
    kkw, kaw, rkw, lnw, lnb = kkw_ref[...], kaw_ref[...], rkw_ref[...], lnw_ref[...], lnb_ref[...]

    for ci in range(tt // c):
        sl = slice(ci * c, (ci + 1) * c)
        r, k, v, a_sig, ld = r_ref[sl, :], k_ref[sl, :], v_ref[sl, :], a_ref[sl, :], ld_ref[sl, :]
        kk, kmod = _key_features(k, a_sig, kkw, kaw)
        a = -kk
        b = kk * a_sig
        cum = _hdot(ones_tril, ld)
        cum_last = cum[c - 1:c, :]
        at = a * jnp.exp(cum - ld)
        rt = r * jnp.exp(cum)
        inv = jnp.exp(-cum)
        bt = b * inv
        kt = kmod * inv
        to_end = jnp.exp(cum_last - cum)
        bk_end = jnp.concatenate([b * to_end, kmod * to_end], axis=0)
        p_end_col = jnp.sum(eye * jnp.exp(cum_last), axis=-1, keepdims=True)

        ar = jnp.concatenate([at, rt], axis=0)
        bk = jnp.concatenate([bt, kt], axis=0)
        gm = _hdot(ar, bk, (((1,), (1,)), ((), ())))
        l_ab = jnp.where(tril_strict, gm[0:c, 0:c], 0.0)
        l_ak = jnp.where(tril_strict, gm[0:c, c:2 * c], 0.0)
        m_rb = jnp.where(tril_incl, gm[c:2 * c, 0:c], 0.0)
        m_rk = jnp.where(tril_incl, gm[c:2 * c, c:2 * c], 0.0)

        t_inv = eye + l_ab
        pw = l_ab
        span = 2
        while span < c:
            pw = _hdot(pw, pw)
            t_inv = t_inv + _hdot(pw, t_inv)
            span *= 2

        st = st_scr[...]
        x = _hdot(at, st) + _hdot(l_ak, v)
        u = _hdot(t_inv, x)
        y = _hdot(rt, st) + _hdot(m_rb, u) + _hdot(m_rk, v)
        uv = jnp.concatenate([u, v], axis=0)
        st_scr[...] = p_end_col * st + _hdot(bk_end, uv, (((0,), (0,)), ((), ())))

        y_o[sl, :] = _group_norm_bonus(y, r, kmod, v, rkw, lnw, lnb, n)

    @pl.when(t == pl.num_programs(2) - 1)
    def _():
        s_o[...] = st_scr[...].T


def _mix_out_attn_seq_kernel(n_heads, n_xheads, x_ref, y_ref, g_ref, yb_ref, wout_ref, gx_ref, wq_ref,
                             mk_ref, mv_ref, wo_ref, x2_o):
    ya = jnp.concatenate([y_ref[h] for h in range(n_heads)], axis=-1) * g_ref[...]
    mixed = jnp.concatenate([ya, yb_ref[...]], axis=-1)
    x1 = x_ref[...] + _bdot(mixed, wout_ref[...])
    q = _bdot(_rms(x1, gx_ref[...]), wq_ref[...])
    d = q.shape[1]
    xd = d // n_xheads
    scale = xd ** -0.5
    outs = []
    for h in range(n_xheads):
        sl = slice(h * xd, (h + 1) * xd)
        s = lax.dot_general(q[:, sl].astype(BF16), mk_ref[:, sl].astype(BF16),
                            (((1,), (1,)), ((), ())), preferred_element_type=F32) * scale
        outs.append(_bdot(_softmax_rows(s), mv_ref[:, sl]))
    o = jnp.concatenate(outs, axis=-1)
    x2_o[...] = x1 + _bdot(o, wo_ref[...])


def _ffn_seq_kernel(final, *refs):
    it = iter(refs)
    x_ref, gf_ref, wup_ref, fprev_ref, fcw_ref, wdown_ref = (next(it) for _ in range(6))
    gfin_ref = next(it) if final else None
    y_o, fnew_o, carry = (next(it) for _ in range(3))

    t = pl.program_id(1)
    tm = x_ref.shape[0]
    f = wdown_ref.shape[0]
    fc = FFN_COL_TILE

    @pl.when(t == 0)
    def _():
        carry[0:2, :] = fprev_ref[...]

    x = x_ref[...]
    xf = _rms(x, gf_ref[...]).astype(BF16)
    acc = x

    def conv_cols(c0):
        up = jnp.dot(xf, wup_ref[:, c0:c0 + fc], preferred_element_type=F32)
        p0 = carry[0:1, c0:c0 + fc]
        p1 = carry[1:2, c0:c0 + fc]
        up1 = _shift_rows(up, [p1])
        up2 = _shift_rows(up, [p0, p1])
        carry[0:2, c0:c0 + fc] = up[tm - 2:tm, :]
        return (fcw_ref[0:1, c0:c0 + fc] * up2 + fcw_ref[1:2, c0:c0 + fc] * up1
                + fcw_ref[2:3, c0:c0 + fc] * up)

    for j in range(f // fc):
        c0 = j * fc
        u = conv_cols(c0)
        gt = conv_cols(f + c0)
        hidden = gt * _sigmoid(gt) * u
        acc = acc + _bdot(hidden, wdown_ref[c0:c0 + fc, :])
    if final:
        acc = _rms(acc, gfin_ref[...])
    y_o[...] = acc

    @pl.when(t == pl.num_programs(1) - 1)
    def _():
        fnew_o[...] = carry[0:2, :]


def _mem_kv_kernel(m_ref, g_ref, wk_ref, wv_ref, k_o, v_o):
    mn = _rms(m_ref[...], g_ref[...])
    k_o[...] = _bdot(mn, wk_ref[...])
    v_o[...] = _bdot(mn, wv_ref[...])


def _mix_in_step_kernel(has_vmix, lora_dims, *refs):
    it = iter(refs)
    x_ref, gmix_ref, win_ref, sprev_ref, cprev_ref, mu_ref, w0_ref, a0_ref, wl_ref, cw_ref = (
        next(it) for _ in range(10))
    if has_vmix:
        vf_ref, v0_ref, v1_ref, v2_ref = (next(it) for _ in range(4))
    r_o, k_o, v_o, a_o, ld_o, g_o, yb_o, xn_o, cnew_o = (next(it) for _ in range(9))

    nb = x_ref.shape[0]
    rc = mu_ref.shape[1]
    cw = cw_ref.shape[1]
    rw = w0_ref.shape[1]

    xn = _rms(x_ref[...], gmix_ref[...])
    xn_o[...] = xn
    stacked = jnp.concatenate([xn, sprev_ref[...]], axis=0)
    proj = _bdot(stacked, win_ref[...])
    p_cur = proj[0:nb, 0:rc]
    p_shift = proj[nb:2 * nb, 0:rc]
    vmix = None
    if has_vmix:
        vmix = (vf_ref[...], v0_ref[...], v1_ref[...], v2_ref[...])
    r, k, v, a_sig, log_decay, gate = _mix_rows(
        p_cur, p_shift, mu_ref[...], w0_ref[...], a0_ref[...], wl_ref[...], lora_dims, rw, vmix)
    r_o[...] = r
    k_o[...] = k
    v_o[...] = v
    a_o[...] = a_sig
    ld_o[...] = log_decay
    g_o[...] = gate

    gate_b = proj[0:nb, rc:rc + cw]
    gate_c = proj[0:nb, rc + cw:rc + 2 * cw]
    h_in = proj[0:nb, rc + 2 * cw:rc + 3 * cw]
    u = gate_c * h_in
    u2 = cprev_ref[:, 0:cw]
    u1 = cprev_ref[:, cw:2 * cw]
    yb_o[...] = gate_b * (cw_ref[0:1, :] * u2 + cw_ref[1:2, :] * u1 + cw_ref[2:3, :] * u)
    cnew_o[:, 0:cw] = u1
    cnew_o[:, cw:2 * cw] = u


def _wkv_step_kernel(r_ref, k_ref, v_ref, a_ref, ld_ref, kkw_ref, kaw_ref, rkw_ref, lnw_ref, lnb_ref,
                     s_ref, y_o, s_o):
    n = s_ref.shape[-1]
    r, k, v, a_sig, ld = r_ref[...], k_ref[...], v_ref[...], a_ref[...], ld_ref[...]
    kk, kmod = _key_features(k, a_sig, kkw_ref[...], kaw_ref[...])
    a = -kk
    b = kk * a_sig
    s = s_ref[...]
    row = lax.broadcasted_iota(jnp.int32, (n, n), 0)
    col = lax.broadcasted_iota(jnp.int32, (n, n), 1)
    eye = (row == col).astype(F32)
    v_col = jnp.sum(eye * v, axis=-1, keepdims=True)
    sa = jnp.sum(s * a, axis=-1, keepdims=True)
    s_new = s * jnp.exp(ld) + sa * b + v_col * kmod
    s_o[...] = s_new
    y_col = jnp.sum(s_new * r, axis=-1, keepdims=True)
    y = jnp.sum(eye * y_col, axis=-2, keepdims=True)
    y_o[...] = _group_norm_bonus(y, r, kmod, v, rkw_ref[...], lnw_ref[...], lnb_ref[...], n)


def _mix_out_q_step_kernel(x_ref, ya_ref, g_ref, yb_ref, wout_ref, gx_ref, wq_ref, x1_o, q_o):
    mixed = jnp.concatenate([ya_ref[...] * g_ref[...], yb_ref[...]], axis=-1)
    x1 = x_ref[...] + _bdot(mixed, wout_ref[...])
    x1_o[...] = x1
    q_o[...] = _bdot(_rms(x1, gx_ref[...]), wq_ref[...])


def _attn_step_kernel(n_xheads, q_ref, mk_ref, mv_ref, o_o):
    bt = q_ref.shape[0]
    d = q_ref.shape[-1]
    xd = d // n_xheads
    scale = xd ** -0.5

    def body(i, carry):
        q = q_ref[i]
        prod = mk_ref[i] * q
        mv = mv_ref[i]
        outs = []
        for h in range(n_xheads):
            sl = slice(h * xd, (h + 1) * xd)
            s = jnp.sum(prod[:, sl], axis=-1, keepdims=True) * scale
            e = jnp.exp(s - jnp.max(s, axis=0, keepdims=True))
            p = e / jnp.sum(e, axis=0, keepdims=True)
            outs.append(jnp.sum(p * mv[:, sl], axis=0, keepdims=True))
        o_o[i] = jnp.concatenate(outs, axis=-1)
        return carry

    lax.fori_loop(0, bt, body, 0)


def _ffn_step_kernel(final, *refs):
    it = iter(refs)
    x1_ref, o_ref, wo_ref, gf_ref, wup_ref, fprev_ref, fcw_ref, wdown_ref = (next(it) for _ in range(8))
    gfin_ref = next(it) if final else None
    y_o, fnew_o = (next(it) for _ in range(2))
    f = wdown_ref.shape[0]
    f2 = 2 * f
    x2 = x1_ref[...] + _bdot(o_ref[...], wo_ref[...])
    up = _bdot(_rms(x2, gf_ref[...]), wup_ref[...])
    up2 = fprev_ref[:, 0:f2]
    up1 = fprev_ref[:, f2:2 * f2]
    upc = fcw_ref[0:1, :] * up2 + fcw_ref[1:2, :] * up1 + fcw_ref[2:3, :] * up
    fnew_o[:, 0:f2] = up1
    fnew_o[:, f2:2 * f2] = up
    u = upc[:, 0:f]
    gt = upc[:, f:f2]
    x3 = x2 + _bdot(gt * _sigmoid(gt) * u, wdown_ref[...])
    if final:
        x3 = _rms(x3, gfin_ref[...])
    y_o[...] = x3


def _params(*sem):
    return pltpu.CompilerParams(dimension_semantics=sem, vmem_limit_bytes=V7X_VMEM_LIMIT_BYTES)


def _whole(arr):
    nd = arr.ndim
    return pl.BlockSpec(arr.shape, lambda *_: (0,) * nd, pipeline_mode=pl.Buffered(1))


def _sds(shape):
    return jax.ShapeDtypeStruct(shape, F32)


def _layer_weights(l, p):
    row = lambda a: a.reshape(1, -1)
    rw = p["w0"].shape[1]
    d_decay, d_aaa, d_gate = p["w2"].shape[1], p["a2"].shape[1], p["g2"].shape[1]
    nl = d_decay + d_aaa + d_gate
    w_lora = jnp.zeros((nl, 3 * rw), F32)
    w_lora = w_lora.at[0:d_decay, 0:rw].set(p["w2"][l])
    w_lora = w_lora.at[d_decay:d_decay + d_aaa, rw:2 * rw].set(p["a2"][l])
    w_lora = w_lora.at[d_decay + d_aaa:nl, 2 * rw:3 * rw].set(p["g2"][l])
    w = dict(
        lora_dims=(d_decay, d_aaa, d_gate),
        norm_mix=row(p["norm_mix"][l]), w_in=p["w_in"][l].astype(BF16), mu=row(p["mu_shift"][l]),
        w0=row(p["w0"][l]), a0=row(p["a0"][l]), w_lora=w_lora.astype(BF16), conv_w=p["conv_w"][l],
        k_k=p["k_k"][l], k_a=p["k_a"][l], r_k=p["r_k"][l], ln_w=p["ln_x_w"][l], ln_b=p["ln_x_b"][l],
        w_out=p["w_out"][l].astype(BF16), norm_x=row(p["norm_x"][l]), wq=p["wq"][l].astype(BF16),
        wo=p["wo"][l].astype(BF16), norm_ffn=row(p["norm_ffn"][l]), w_up=p["w_up"][l].astype(BF16),
        ffn_conv_w=p["ffn_conv_w"][l], w_down=p["w_down"][l].astype(BF16),
        norm_mem=row(p["norm_mem"][l]), wk=p["wk"][l].astype(BF16), wv=p["wv"][l].astype(BF16),
        norm_final=row(p["norm_final"]),
    )
    if l > 0:
        lanes = 128
        mv = p["v1"].shape[2]
        mvp = -(-mv // lanes) * lanes
        w["v0"] = row(p["v0"][l - 1])
        w["v1"] = jnp.zeros((rw, mvp), F32).at[:, 0:mv].set(p["v1"][l - 1]).astype(BF16)
        w["v2"] = jnp.zeros((mvp, rw), F32).at[0:mv, :].set(p["v2"][l - 1]).astype(BF16)
    return w


def _mem_kv(mem, w):
    rows, d = mem.shape
    tm = min(512, rows)
    return pl.pallas_call(
        _mem_kv_kernel,
        grid=(rows // tm,),
        in_specs=[pl.BlockSpec((tm, d), lambda i: (i, 0)), _whole(w["norm_mem"]), _whole(w["wk"]), _whole(w["wv"])],
        out_specs=[pl.BlockSpec((tm, d), lambda i: (i, 0))] * 2,
        out_shape=[_sds((rows, d))] * 2,
        compiler_params=_params("parallel"),
        name="mem_kv",
    )(mem, w["norm_mem"], w["wk"], w["wv"])


def _prompt_layer(x, mem_k, mem_v, shift0, wkv0, conv0, ffn0, v_first, w, n_heads, n_xheads, final):
    bsz, t, d = x.shape
    rw = w["w0"].shape[1]
    hs = rw // n_heads
    cw = w["conv_w"].shape[1]
    rc = w["mu"].shape[1]
    tm = min(ROW_TILE, t)
    nt = t // tm
    has_vmix = v_first is not None

    tile = lambda n: pl.BlockSpec((None, tm, n), lambda b, i: (b, i, 0))
    per_b = lambda s: pl.BlockSpec((None,) + s, lambda b, i: (b,) + (0,) * len(s))
    heads = pl.BlockSpec((None, n_heads, tm, hs), lambda b, i: (b, 0, i, 0))

    ins = [x, w["norm_mix"], w["w_in"], shift0[:, None, :], conv0, w["mu"], w["w0"], w["a0"], w["w_lora"],
           w["conv_w"]]
    specs = [tile(d), _whole(w["norm_mix"]), _whole(w["w_in"]), per_b((1, d)), per_b((2, cw)),
             _whole(w["mu"]), _whole(w["w0"]), _whole(w["a0"]), _whole(w["w_lora"]), _whole(w["conv_w"])]
    if has_vmix:
        ins += [v_first, w["v0"], w["v1"], w["v2"]]
        specs += [tile(rw), _whole(w["v0"]), _whole(w["v1"]), _whole(w["v2"])]
    head_shape = _sds((bsz, n_heads, t, hs))
    out_shape = [head_shape] * 5 + [_sds((bsz, t, rw)), _sds((bsz, t, cw))]
    out_specs = [heads] * 5 + [tile(rw), tile(cw)]
    if not has_vmix:
        out_shape.append(_sds((bsz, t, rw)))
        out_specs.append(tile(rw))
    out_shape += [_sds((bsz, 1, d)), _sds((bsz, 2, cw))]
    out_specs += [per_b((1, d)), per_b((2, cw))]
    outs = pl.pallas_call(
        functools.partial(_mix_in_seq_kernel, has_vmix, w["lora_dims"], n_heads),
        grid=(bsz, nt), in_specs=specs, out_specs=out_specs, out_shape=out_shape,
        scratch_shapes=[pltpu.VMEM((8, rc), F32), pltpu.VMEM((8, cw), F32)],
        compiler_params=_params("parallel", "arbitrary"),
        name="mix_in_seq",
    )(*ins)
    r, k, v, a_sig, ld, gate, y_b = outs[:7]
    if has_vmix:
        xn_last, conv_new = outs[7:]
    else:
        v_first, xn_last, conv_new = outs[7:]

    tt = tm
    hblk = pl.BlockSpec((None, None, tt, hs), lambda b, h, i: (b, h, i, 0))
    hpar = pl.BlockSpec((None, 1, hs), lambda b, h, i: (h, 0, 0))
    sblk = pl.BlockSpec((None, None, hs, hs), lambda b, h, i: (b, h, 0, 0))
    hp = lambda a: a.reshape(n_heads, 1, hs)
    y_a, wkv_new = pl.pallas_call(
        _wkv_seq_kernel,
        grid=(bsz, n_heads, t // tt),
        in_specs=[hblk] * 5 + [hpar] * 5 + [sblk],
        out_specs=[hblk, sblk],
        out_shape=[head_shape, _sds((bsz, n_heads, hs, hs))],
        scratch_shapes=[pltpu.VMEM((hs, hs), F32)],
        compiler_params=_params("parallel", "parallel", "arbitrary"),
        name="wkv_seq",
    )(r, k, v, a_sig, ld, hp(w["k_k"]), hp(w["k_a"]), hp(w["r_k"]), hp(w["ln_w"]), hp(w["ln_b"]), wkv0)

    m = mem_k.shape[1]
    x2 = pl.pallas_call(
        functools.partial(_mix_out_attn_seq_kernel, n_heads, n_xheads),
        grid=(bsz, nt),
        in_specs=[tile(d), heads, tile(rw), tile(cw), _whole(w["w_out"]), _whole(w["norm_x"]), _whole(w["wq"]),
                  per_b((m, d)), per_b((m, d)), _whole(w["wo"])],
        out_specs=tile(d), out_shape=_sds((bsz, t, d)),
        compiler_params=_params("parallel", "arbitrary"),
        name="mix_out_attn_seq",
    )(x, y_a, gate, y_b, w["w_out"], w["norm_x"], w["wq"], mem_k, mem_v, w["wo"])

    f2 = w["w_up"].shape[1]
    ins = [x2, w["norm_ffn"], w["w_up"], ffn0, w["ffn_conv_w"], w["w_down"]]
    specs = [tile(d), _whole(w["norm_ffn"]), _whole(w["w_up"]), per_b((2, f2)), _whole(w["ffn_conv_w"]),
             _whole(w["w_down"])]
    if final:
        ins.append(w["norm_final"])
        specs.append(_whole(w["norm_final"]))
    x3, ffn_new = pl.pallas_call(
        functools.partial(_ffn_seq_kernel, final),
        grid=(bsz, nt), in_specs=specs,
        out_specs=[tile(d), per_b((2, f2))], out_shape=[_sds((bsz, t, d)), _sds((bsz, 2, f2))],
        scratch_shapes=[pltpu.VMEM((8, f2), F32)],
        compiler_params=_params("parallel", "arbitrary"),
        name="ffn_seq",
    )(*ins)
    return x3, xn_last[:, 0, :], wkv_new, conv_new, ffn_new, v_first


def _sample_layer(x, mem_k, mem_v, shift0, wkv0, conv0, ffn0, v_first, w, n_heads, n_xheads, final):
    nb, d = x.shape
    rw = w["w0"].shape[1]
    hs = rw // n_heads
    cw = w["conv_w"].shape[1]
    has_vmix = v_first is not None
    vm = pl.BlockSpec(memory_space=pltpu.VMEM)

    ins = [x, w["norm_mix"], w["w_in"], shift0, conv0.reshape(nb, 2 * cw), w["mu"], w["w0"], w["a0"],
           w["w_lora"], w["conv_w"]]
    if has_vmix:
        ins += [v_first, w["v0"], w["v1"], w["v2"]]
    r, k, v, a_sig, ld, gate, y_b, xn, conv_new = pl.pallas_call(
        functools.partial(_mix_in_step_kernel, has_vmix, w["lora_dims"]),
        in_specs=[vm] * len(ins), out_specs=[vm] * 9,
        out_shape=[_sds((nb, rw))] * 6 + [_sds((nb, cw)), _sds((nb, d)), _sds((nb, 2 * cw))],
        compiler_params=pltpu.CompilerParams(vmem_limit_bytes=V7X_VMEM_LIMIT_BYTES),
        name="mix_in_step",
    )(*ins)
    if not has_vmix:
        v_first = v

    bt = min(WKV_BATCH_TILE, nb)
    hv = lambda a: a.reshape(nb, n_heads, 1, hs)
    hp = lambda a: a.reshape(n_heads, 1, hs)
    vblk = pl.BlockSpec((bt, n_heads, 1, hs), lambda i: (i, 0, 0, 0))
    pblk = pl.BlockSpec((n_heads, 1, hs), lambda i: (0, 0, 0))
    sblk = pl.BlockSpec((bt, n_heads, hs, hs), lambda i: (i, 0, 0, 0))
    y_a, wkv_new = pl.pallas_call(
        _wkv_step_kernel,
        grid=(nb // bt,),
        in_specs=[vblk] * 5 + [pblk] * 5 + [sblk],
        out_specs=[vblk, sblk],
        out_shape=[_sds((nb, n_heads, 1, hs)), _sds((nb, n_heads, hs, hs))],
        compiler_params=_params("parallel"),
        name="wkv_step",
    )(hv(r), hv(k), hv(v), hv(a_sig), hv(ld), hp(w["k_k"]), hp(w["k_a"]), hp(w["r_k"]), hp(w["ln_w"]),
      hp(w["ln_b"]), wkv0)

    x1, q = pl.pallas_call(
        _mix_out_q_step_kernel,
        in_specs=[vm] * 7, out_specs=[vm] * 2, out_shape=[_sds((nb, d))] * 2,
        compiler_params=pltpu.CompilerParams(vmem_limit_bytes=V7X_VMEM_LIMIT_BYTES),
        name="mix_out_q_step",
    )(x, y_a.reshape(nb, rw), gate, y_b, w["w_out"], w["norm_x"], w["wq"])

    m = mem_k.shape[1]
    ab = min(ATTN_BATCH_TILE, nb)
    o = pl.pallas_call(
        functools.partial(_attn_step_kernel, n_xheads),
        grid=(nb // ab,),
        in_specs=[pl.BlockSpec((ab, 1, d), lambda i: (i, 0, 0)), pl.BlockSpec((ab, m, d), lambda i: (i, 0, 0)),
                  pl.BlockSpec((ab, m, d), lambda i: (i, 0, 0))],
        out_specs=pl.BlockSpec((ab, 1, d), lambda i: (i, 0, 0)),
        out_shape=_sds((nb, 1, d)),
        compiler_params=_params("parallel"),
        name="attn_step",
    )(q.reshape(nb, 1, d), mem_k, mem_v)

    f2 = w["w_up"].shape[1]
    ins = [x1, o.reshape(nb, d), w["wo"], w["norm_ffn"], w["w_up"], ffn0.reshape(nb, 2 * f2), w["ffn_conv_w"],
           w["w_down"]]
    if final:
        ins.append(w["norm_final"])
    x3, ffn_new = pl.pallas_call(
        functools.partial(_ffn_step_kernel, final),
        in_specs=[vm] * len(ins), out_specs=[vm] * 2,
        out_shape=[_sds((nb, d)), _sds((nb, 2 * f2))],
        compiler_params=pltpu.CompilerParams(vmem_limit_bytes=V7X_VMEM_LIMIT_BYTES),
        name="ffn_step",
    )(*ins)
    return x3, xn, wkv_new, conv_new.reshape(nb, 2, cw), ffn_new.reshape(nb, 2, f2), v_first


def _trunk(layer_fn, x, mem_k, mem_v, shift0, wkv0, conv0, ffn0, weights, n_heads, n_xheads):
    depth = len(weights)
    v_first = None
    shs, wks, cvs, ffs = [], [], [], []
    for l in range(depth):
        x, sh, s, cs, fs, v_first = layer_fn(x, mem_k[l], mem_v[l], shift0[l], wkv0[l], conv0[l], ffn0[l],
                                             v_first, weights[l], n_heads, n_xheads, l == depth - 1)
        shs.append(sh)
        wks.append(s)
        cvs.append(cs)
        ffs.append(fs)
    return x, jnp.stack(shs), jnp.stack(wks), jnp.stack(cvs), jnp.stack(ffs)


def kernel(x_prompt, x_sample, mem_prompt, state_shift, state_wkv, state_conv, state_ffn, cache_mem_k, cache_mem_v, norm_mix, w_in, mu_shift, w0, w2, a0, a2, g2, v0, v1, v2, k_k, k_a, r_k, ln_x_w, ln_x_b, conv_w, w_out, norm_x, norm_mem, wq, wk, wv, wo, norm_ffn, w_up, ffn_conv_w, w_down, norm_final):
    p = dict(norm_mix=norm_mix, w_in=w_in, mu_shift=mu_shift, w0=w0, w2=w2, a0=a0, a2=a2, g2=g2, v0=v0, v1=v1,
             v2=v2, k_k=k_k, k_a=k_a, r_k=r_k, ln_x_w=ln_x_w, ln_x_b=ln_x_b, conv_w=conv_w, w_out=w_out,
             norm_x=norm_x, norm_mem=norm_mem, wq=wq, wk=wk, wv=wv, wo=wo, norm_ffn=norm_ffn, w_up=w_up,
             ffn_conv_w=ffn_conv_w, w_down=w_down, norm_final=norm_final)
    depth = w_in.shape[0]
    n_heads = state_wkv.shape[2]
    hs = state_wkv.shape[3]
    n_xheads, xd = cache_mem_k.shape[3], cache_mem_k.shape[4]
    weights = [_layer_weights(l, p) for l in range(depth)]

    bp, tp, d = x_prompt.shape
    m = mem_prompt.shape[1]
    mks, mvs = [], []
    for l in range(depth):
        mk, mv = _mem_kv(mem_prompt.reshape(bp * m, d), weights[l])
        mks.append(mk.reshape(bp, m, d))
        mvs.append(mv.reshape(bp, m, d))
    cw = conv_w.shape[2]
    f2 = w_up.shape[2]
    y_p, shift_p, wkv_p, conv_p, ffn_p = _trunk(
        _prompt_layer, x_prompt, mks, mvs,
        jnp.zeros((depth, bp, d), F32), jnp.zeros((depth, bp, n_heads, hs, hs), F32),
        jnp.zeros((depth, bp, state_conv.shape[2], cw), F32), jnp.zeros((depth, bp, state_ffn.shape[2], f2), F32),
        weights, n_heads, n_xheads)
    mem_k_p = jnp.stack(mks).reshape(depth, bp, m, n_xheads, xd)
    mem_v_p = jnp.stack(mvs).reshape(depth, bp, m, n_xheads, xd)

    nb = x_sample.shape[0]
    ms = cache_mem_k.shape[2]
    y_s, shift_s, wkv_s, conv_s, ffn_s = _trunk(
        _sample_layer, x_sample.reshape(nb, d), cache_mem_k.reshape(depth, nb, ms, d),
        cache_mem_v.reshape(depth, nb, ms, d), state_shift, state_wkv, state_conv, state_ffn,
        weights, n_heads, n_xheads)
    return (y_p, y_s.reshape(x_sample.shape), shift_p, wkv_p, conv_p, ffn_p, mem_k_p, mem_v_p,
            shift_s, wkv_s, conv_s, ffn_s)
```

```python
import functools

import jax
import jax.numpy as jnp
from jax import lax
from jax.experimental import pallas as pl
from jax.experimental.pallas import tpu as pltpu

F32 = jnp.float32
BF16 = jnp.bfloat16
HIGHEST = lax.Precision.HIGHEST

RMS_EPS = 1e-6
GROUPNORM_EPS_PER_CHANNEL = 1e-5
KK_NORM_FLOOR = 1e-12

ROW_TILE = 256
CHUNK = 64
ATTN_BATCH_TILE = 2
WKV_BATCH_TILE = 8
FFN_COL_TILE = 256
V7X_VMEM_LIMIT_BYTES = 56 * 1024 * 1024


def _rms(x, g):
    ms = jnp.mean(x * x, axis=-1, keepdims=True)
    return x * lax.rsqrt(ms + RMS_EPS) * g


def _bdot(a, b):
    return jnp.dot(a.astype(BF16), b.astype(BF16), preferred_element_type=F32)


def _hdot(a, b, dims=(((1,), (0,)), ((), ()))):
    return lax.dot_general(a, b, dims, precision=HIGHEST, preferred_element_type=F32)


def _sigmoid(x):
    return 1.0 / (1.0 + jnp.exp(-x))


def _softplus(x):
    return jnp.maximum(x, 0.0) + jnp.log1p(jnp.exp(-jnp.abs(x)))


def _shift_rows(x, prev_rows):
    s = len(prev_rows)
    rolled = pltpu.roll(x, s, axis=0)
    row = lax.broadcasted_iota(jnp.int32, x.shape, 0)
    for i, pr in enumerate(prev_rows):
        rolled = jnp.where(row == i, pr, rolled)
    return rolled


def _mix_rows(p_cur, p_shift, mu, w0, a0, w_lora, lora_dims, rw, vmix):
    d_decay, d_aaa, _ = lora_dims
    z = p_cur + (p_shift - p_cur) * mu
    r = z[:, 0:rw]
    k = z[:, rw:2 * rw]
    v = z[:, 2 * rw:3 * rw]
    lo = z[:, 3 * rw:]
    lane = lax.broadcasted_iota(jnp.int32, lo.shape, 1)
    feat = jnp.where(lane < d_decay, jnp.tanh(lo),
                     jnp.where(lane < d_decay + d_aaa, lo, _sigmoid(lo)))
    lora = _bdot(feat, w_lora)
    w = -_softplus(-(w0 + lora[:, 0:rw])) - 0.5
    log_decay = -jnp.exp(w)
    a_sig = _sigmoid(a0 + lora[:, rw:2 * rw])
    gate = lora[:, 2 * rw:3 * rw]
    if vmix is not None:
        v_first, v0, v1, v2 = vmix
        vm = _sigmoid(v0 + _bdot(_bdot(v, v1), v2))
        v = v + (v_first - v) * vm
    return r, k, v, a_sig, log_decay, gate


def _group_norm_bonus(y, r, kmod, v, rk, lnw, lnb, n):
    mean = jnp.mean(y, axis=-1, keepdims=True)
    yc = y - mean
    var = jnp.mean(yc * yc, axis=-1, keepdims=True)
    y = yc * lax.rsqrt(var + n * GROUPNORM_EPS_PER_CHANNEL) * lnw + lnb
    return y + jnp.sum(r * kmod * rk, axis=-1, keepdims=True) * v


def _key_features(k, a_sig, kk_w, ka_w):
    kk = k * kk_w
    nrm = jnp.sqrt(jnp.sum(kk * kk, axis=-1, keepdims=True))
    kk = kk / jnp.maximum(nrm, KK_NORM_FLOOR)
    kmod = k * (1.0 + (a_sig - 1.0) * ka_w)
    return kk, kmod


def _softmax_rows(s):
    m = jnp.max(s, axis=-1, keepdims=True)
    e = jnp.exp(s - m)
    return e / jnp.sum(e, axis=-1, keepdims=True)


def _mix_in_seq_kernel(has_vmix, lora_dims, *refs):
    it = iter(refs)
    x_ref, gmix_ref, win_ref, sprev_ref, cprev_ref, mu_ref, w0_ref, a0_ref, wl_ref, cw_ref = (
        next(it) for _ in range(10))
    if has_vmix:
        vf_ref, v0_ref, v1_ref, v2_ref = (next(it) for _ in range(4))
    r_o, k_o, v_o, a_o, ld_o, g_o, yb_o, xn_o, cnew_o, pc_scr, uc_scr = (next(it) for _ in range(11))

    t = pl.program_id(1)
    tm = x_ref.shape[0]
    rc = mu_ref.shape[1]
    cw = cw_ref.shape[1]
    rw = w0_ref.shape[1]

    @pl.when(t == 0)
    def _():
        sprev = jnp.broadcast_to(sprev_ref[...], (8, sprev_ref.shape[1]))
        pc_scr[...] = _bdot(sprev, win_ref[:, 0:rc])
        uc_scr[0:2, :] = cprev_ref[...]

    xn = _rms(x_ref[...], gmix_ref[...])
    proj = _bdot(xn, win_ref[...])
    p_cur = proj[:, 0:rc]
    p_shift = _shift_rows(p_cur, [pc_scr[0:1, :]])
    vmix = None
    if has_vmix:
        vmix = (vf_ref[...], v0_ref[...], v1_ref[...], v2_ref[...])
    r, k, v, a_sig, log_decay, gate = _mix_rows(
        p_cur, p_shift, mu_ref[...], w0_ref[...], a0_ref[...], wl_ref[...], lora_dims, rw, vmix)
    r_o[...] = r
    k_o[...] = k
    v_o[...] = v
    a_o[...] = a_sig
    ld_o[...] = log_decay
    g_o[...] = gate

    gate_b = proj[:, rc:rc + cw]
    gate_c = proj[:, rc + cw:rc + 2 * cw]
    h_in = proj[:, rc + 2 * cw:rc + 3 * cw]
    u = gate_c * h_in
    u1 = _shift_rows(u, [uc_scr[1:2, :]])
    u2 = _shift_rows(u, [uc_scr[0:1, :], uc_scr[1:2, :]])
    conv = cw_ref[0:1, :] * u2 + cw_ref[1:2, :] * u1 + cw_ref[2:3, :] * u
    yb_o[...] = gate_b * conv

    pc_scr[0:1, :] = p_cur[tm - 1:tm, :]
    uc_scr[0:2, :] = u[tm - 2:tm, :]

    @pl.when(t == pl.num_programs(1) - 1)
    def _():
        xn_o[...] = xn[tm - 1:tm, :]
        cnew_o[...] = u[tm - 2:tm, :]


def _split3(x):
    hi = x.astype(BF16)
    r1 = x - hi.astype(F32)
    mid = r1.astype(BF16)
    lo = (r1 - mid.astype(F32)).astype(BF16)
    return hi, mid, lo


def _dot3(a, b, dims=(((1,), (0,)), ((), ()))):
    a_hi = a.astype(BF16)
    a_lo = (a - a_hi.astype(F32)).astype(BF16)
    b_hi = b.astype(BF16)
    b_lo = (b - b_hi.astype(F32)).astype(BF16)
    d = functools.partial(lax.dot_general, dimension_numbers=dims, preferred_element_type=F32)
    return d(a_hi, b_hi) + d(a_lo, b_hi) + d(a_hi, b_lo)


def _dot1(a, b, dims=(((1,), (0,)), ((), ()))):
    return lax.dot_general(a.astype(BF16), b.astype(BF16), dims, preferred_element_type=F32)


_NT = (((1,), (1,)), ((), ()))
_TN = (((0,), (0,)), ((), ()))
_WKV_DOT = _dot1


def _wkv_seq_kernel(hs, r_ref, k_ref, v_ref, a_ref, ld_ref, kkw_ref, kaw_ref, rkw_ref, lnw_ref, lnb_ref,
                    s0_ref, y_o, s_o, st_scr):
    t = pl.program_id(1)
    tt, rw = r_ref.shape
    c = CHUNK
    pw_lanes = 2 * hs
    n_pairs = rw // pw_lanes
    assert c == hs, "chunk length equals the head size in the packed layout"

    lane = lax.broadcasted_iota(jnp.int32, (1, pw_lanes), 1)
    lane_a = lane < hs
    row_p = lax.broadcasted_iota(jnp.int32, (pw_lanes, pw_lanes), 0)
    col_p = lax.broadcasted_iota(jnp.int32, (pw_lanes, pw_lanes), 1)
    same_head = (row_p < hs) == (col_p < hs)
    head_ones = same_head.astype(BF16)
    eye_p = (row_p == col_p).astype(F32)
    row_c = lax.broadcasted_iota(jnp.int32, (c, pw_lanes), 0)
    col_c = lax.broadcasted_iota(jnp.int32, (c, pw_lanes), 1) & (hs - 1)
    tril_strict = row_c > col_c
    tril_incl = row_c >= col_c
    eye_c = (row_c == col_c).astype(F32)
    row_t = lax.broadcasted_iota(jnp.int32, (c, 3 * c), 0)
    col_t = lax.broadcasted_iota(jnp.int32, (c, 3 * c), 1)
    col_t = jnp.where(col_t < c, col_t, jnp.where(col_t < 2 * c, col_t - c, col_t - 2 * c))
    tril3 = jnp.where(row_t >= col_t, 1.0, 0.0).astype(BF16)
    ones3 = jnp.concatenate([head_ones] * 3, axis=0)

    def blockdiag(x):
        return jnp.concatenate([jnp.where(lane_a, x, 0.0), jnp.where(lane_a, 0.0, x)], axis=0)

    def head_sums(x):
        hi, mid, lo = _split3(x)
        return jnp.dot(jnp.concatenate([hi, mid, lo], axis=1), ones3, preferred_element_type=F32)

    @pl.when(t == 0)
    def _():
        for p in range(n_pairs):
            sa, sb = s0_ref[2 * p], s0_ref[2 * p + 1]
            z = jnp.zeros_like(sa)
            blk = jnp.concatenate([jnp.concatenate([sa, z], axis=1), jnp.concatenate([z, sb], axis=1)], axis=0)
            st_scr[p] = blk.T

    n_chunks = tt // c
    items = [(ci, p) for ci in range(n_chunks) for p in range(n_pairs)]
    cat = jnp.concatenate

    def each(f, *lists):
        return [f(*xs) for xs in zip(*lists)]

    def tiles(ref):
        return [ref[ci * c:(ci + 1) * c, p * pw_lanes:(p + 1) * pw_lanes] for ci, p in items]

    def params(ref):
        return [ref[:, p * pw_lanes:(p + 1) * pw_lanes] for _, p in items]

    def cumsum_rows(x):
        hi, mid, lo = _split3(x)
        return jnp.dot(tril3, cat([hi, mid, lo], axis=0), preferred_element_type=F32)

    r, k, v, a_sig, ld = (tiles(ref) for ref in (r_ref, k_ref, v_ref, a_ref, ld_ref))
    kkw, kaw, rkw, lnw, lnb = (params(ref) for ref in (kkw_ref, kaw_ref, rkw_ref, lnw_ref, lnb_ref))

    kk = each(lambda k_, w_: k_ * w_, k, kkw)
    kmod = each(lambda k_, a_, w_: k_ * (1.0 + (a_ - 1.0) * w_), k, a_sig, kaw)
    sums = each(lambda kk_, r_, km_, w_: head_sums(cat([kk_ * kk_, r_ * km_ * w_], axis=0)), kk, r, kmod, rkw)
    kk = each(lambda kk_, s_: kk_ / jnp.maximum(jnp.sqrt(s_[0:c]), KK_NORM_FLOOR), kk, sums)
    bonus = [s_[c:2 * c] for s_ in sums]
    b = each(lambda kk_, a_: kk_ * a_, kk, a_sig)

    cum = each(cumsum_rows, ld)
    at = each(lambda kk_, cu, l_: -kk_ * jnp.exp(cu - l_), kk, cum, ld)
    rt = each(lambda r_, cu: r_ * jnp.exp(cu), r, cum)
    inv = each(lambda cu: jnp.exp(-cu), cum)
    bt = each(lambda b_, i_: b_ * i_, b, inv)
    kt = each(lambda km_, i_: km_ * i_, kmod, inv)
    to_end = each(lambda cu: jnp.exp(cu[c - 1:c, :] - cu), cum)
    bk_end = each(lambda b_, km_, te: cat([b_ * te, km_ * te], axis=0), b, kmod, to_end)
    p_end_col = each(lambda cu: jnp.sum(eye_p * jnp.exp(cu[c - 1:c, :]), axis=-1, keepdims=True), cum)

    gm = each(lambda at_, rt_, bt_, kt_: _WKV_DOT(cat([at_, rt_], axis=0),
                                                  cat([blockdiag(bt_), blockdiag(kt_)], axis=0), _NT),
              at, rt, bt, kt)
    l_ab = [jnp.where(tril_strict, g[0:c, 0:2 * c], 0.0) for g in gm]
    l_ak = [jnp.where(tril_strict, g[0:c, 2 * c:4 * c], 0.0) for g in gm]
    m_rb = [jnp.where(tril_incl, g[c:2 * c, 0:2 * c], 0.0) for g in gm]
    m_rk = [jnp.where(tril_incl, g[c:2 * c, 2 * c:4 * c], 0.0) for g in gm]

    t_inv = [eye_c + l_ for l_ in l_ab]
    pw = l_ab
    span = 2
    while span < c:
        pw = each(lambda q: _WKV_DOT(q, blockdiag(q)), pw)
        t_inv = each(lambda t_, q: t_ + _WKV_DOT(q, blockdiag(t_)), t_inv, pw)
        span *= 2

    y = []
    for ci in range(n_chunks):
        idx = [ci * n_pairs + p for p in range(n_pairs)]
        st = [st_scr[p] for p in range(n_pairs)]
        vb = [blockdiag(v[i]) for i in idx]
        x = [_WKV_DOT(cat([at[i], l_ak[i]], axis=1), cat([st[p], vb[p]], axis=0)) for p, i in enumerate(idx)]
        u = [_WKV_DOT(t_inv[i], blockdiag(x[p])) for p, i in enumerate(idx)]
        y += [_WKV_DOT(cat([rt[i], m_rk[i], m_rb[i]], axis=1), cat([st[p], vb[p], blockdiag(u[p])], axis=0))
              for p, i in enumerate(idx)]
        upd = [_WKV_DOT(bk_end[i], cat([u[p], v[i]], axis=0), _TN) for p, i in enumerate(idx)]
        for p, i in enumerate(idx):
            st_scr[p] = p_end_col[i] * st[p] + jnp.where(same_head, upd[p], 0.0)

    s2 = each(lambda y_: head_sums(cat([y_, y_ * y_], axis=0)), y)
    for (ci, p), y_, s_, lnw_, lnb_, bo_, v_ in zip(items, y, s2, lnw, lnb, bonus, v):
        mean = s_[0:c] * (1.0 / hs)
        var = s_[c:2 * c] * (1.0 / hs) - mean * mean
        yn = (y_ - mean) * lax.rsqrt(var + hs * GROUPNORM_EPS_PER_CHANNEL) * lnw_ + lnb_
        y_o[ci * c:(ci + 1) * c, p * pw_lanes:(p + 1) * pw_lanes] = yn + bo_ * v_

    @pl.when(t == pl.num_programs(1) - 1)
    def _():
        for p in range(n_pairs):
            s_pair = st_scr[p].T
            s_o[2 * p] = s_pair[0:hs, 0:hs]
            s_o[2 * p + 1] = s_pair[hs:2 * hs, hs:2 * hs]


def _mix_out_attn_seq_kernel(n_xheads, x_ref, y_ref, g_ref, yb_ref, wout_ref, gx_ref, wq_ref,
                             mk_ref, mv_ref, wo_ref, x2_o):
    mixed = jnp.concatenate([y_ref[...] * g_ref[...], yb_ref[...]], axis=-1)
    x1 = x_ref[...] + _bdot(mixed, wout_ref[...])
    q = _bdot(_rms(x1, gx_ref[...]), wq_ref[...])
    d = q.shape[1]
    xd = d // n_xheads
    scale = xd ** -0.5
    outs = []
    for h in range(n_xheads):
        sl = slice(h * xd, (h + 1) * xd)
        s = lax.dot_general(q[:, sl].astype(BF16), mk_ref[:, sl].astype(BF16),
                            (((1,), (1,)), ((), ())), preferred_element_type=F32) * scale
        outs.append(_bdot(_softmax_rows(s), mv_ref[:, sl]))
    o = jnp.concatenate(outs, axis=-1)
    x2_o[...] = x1 + _bdot(o, wo_ref[...])


def _ffn_seq_kernel(final, *refs):
    it = iter(refs)
    x_ref, gf_ref, wup_ref, fprev_ref, fcw_ref, wdown_ref = (next(it) for _ in range(6))
    gfin_ref = next(it) if final else None
    y_o, fnew_o, carry = (next(it) for _ in range(3))

    t = pl.program_id(1)
    tm = x_ref.shape[0]
    f = wdown_ref.shape[0]
    fc = FFN_COL_TILE

    @pl.when(t == 0)
    def _():
        carry[0:2, :] = fprev_ref[...]

    x = x_ref[...]
    xf = _rms(x, gf_ref[...]).astype(BF16)
    acc = x

    def conv_cols(c0):
        up = jnp.dot(xf, wup_ref[:, c0:c0 + fc], preferred_element_type=F32)
        p0 = carry[0:1, c0:c0 + fc]
        p1 = carry[1:2, c0:c0 + fc]
        up1 = _shift_rows(up, [p1])
        up2 = _shift_rows(up, [p0, p1])
        carry[0:2, c0:c0 + fc] = up[tm - 2:tm, :]
        return (fcw_ref[0:1, c0:c0 + fc] * up2 + fcw_ref[1:2, c0:c0 + fc] * up1
                + fcw_ref[2:3, c0:c0 + fc] * up)

    for j in range(f // fc):
        c0 = j * fc
        u = conv_cols(c0)
        gt = conv_cols(f + c0)
        hidden = gt * _sigmoid(gt) * u
        acc = acc + _bdot(hidden, wdown_ref[c0:c0 + fc, :])
    if final:
        acc = _rms(acc, gfin_ref[...])
    y_o[...] = acc

    @pl.when(t == pl.num_programs(1) - 1)
    def _():
        fnew_o[...] = carry[0:2, :]


def _mem_kv_kernel(m_ref, g_ref, wk_ref, wv_ref, k_o, v_o):
    mn = _rms(m_ref[...], g_ref[...])
    k_o[...] = _bdot(mn, wk_ref[...])
    v_o[...] = _bdot(mn, wv_ref[...])


def _mix_in_step_kernel(has_vmix, lora_dims, *refs):
    it = iter(refs)
    x_ref, gmix_ref, win_ref, sprev_ref, cprev_ref, mu_ref, w0_ref, a0_ref, wl_ref, cw_ref = (
        next(it) for _ in range(10))
    if has_vmix:
        vf_ref, v0_ref, v1_ref, v2_ref = (next(it) for _ in range(4))
    r_o, k_o, v_o, a_o, ld_o, g_o, yb_o, xn_o, cnew_o = (next(it) for _ in range(9))

    nb = x_ref.shape[0]
    rc = mu_ref.shape[1]
    cw = cw_ref.shape[1]
    rw = w0_ref.shape[1]

    xn = _rms(x_ref[...], gmix_ref[...])
    xn_o[...] = xn
    stacked = jnp.concatenate([xn, sprev_ref[...]], axis=0)
    proj = _bdot(stacked, win_ref[...])
    p_cur = proj[0:nb, 0:rc]
    p_shift = proj[nb:2 * nb, 0:rc]
    vmix = None
    if has_vmix:
        vmix = (vf_ref[...], v0_ref[...], v1_ref[...], v2_ref[...])
    r, k, v, a_sig, log_decay, gate = _mix_rows(
        p_cur, p_shift, mu_ref[...], w0_ref[...], a0_ref[...], wl_ref[...], lora_dims, rw, vmix)
    r_o[...] = r
    k_o[...] = k
    v_o[...] = v
    a_o[...] = a_sig
    ld_o[...] = log_decay
    g_o[...] = gate

    gate_b = proj[0:nb, rc:rc + cw]
    gate_c = proj[0:nb, rc + cw:rc + 2 * cw]
    h_in = proj[0:nb, rc + 2 * cw:rc + 3 * cw]
    u = gate_c * h_in
    u2 = cprev_ref[:, 0:cw]
    u1 = cprev_ref[:, cw:2 * cw]
    yb_o[...] = gate_b * (cw_ref[0:1, :] * u2 + cw_ref[1:2, :] * u1 + cw_ref[2:3, :] * u)
    cnew_o[:, 0:cw] = u1
    cnew_o[:, cw:2 * cw] = u


def _wkv_step_kernel(r_ref, k_ref, v_ref, a_ref, ld_ref, kkw_ref, kaw_ref, rkw_ref, lnw_ref, lnb_ref,
                     s_ref, y_o, s_o):
    n = s_ref.shape[-1]
    r, k, v, a_sig, ld = r_ref[...], k_ref[...], v_ref[...], a_ref[...], ld_ref[...]
    kk, kmod = _key_features(k, a_sig, kkw_ref[...], kaw_ref[...])
    a = -kk
    b = kk * a_sig
    s = s_ref[...]
    row = lax.broadcasted_iota(jnp.int32, (n, n), 0)
    col = lax.broadcasted_iota(jnp.int32, (n, n), 1)
    eye = (row == col).astype(F32)
    v_col = jnp.sum(eye * v, axis=-1, keepdims=True)
    sa = jnp.sum(s * a, axis=-1, keepdims=True)
    s_new = s * jnp.exp(ld) + sa * b + v_col * kmod
    s_o[...] = s_new
    y_col = jnp.sum(s_new * r, axis=-1, keepdims=True)
    y = jnp.sum(eye * y_col, axis=-2, keepdims=True)
    y_o[...] = _group_norm_bonus(y, r, kmod, v, rkw_ref[...], lnw_ref[...], lnb_ref[...], n)


def _mix_out_q_step_kernel(x_ref, ya_ref, g_ref, yb_ref, wout_ref, gx_ref, wq_ref, x1_o, q_o):
    mixed = jnp.concatenate([ya_ref[...] * g_ref[...], yb_ref[...]], axis=-1)
    x1 = x_ref[...] + _bdot(mixed, wout_ref[...])
    x1_o[...] = x1
    q_o[...] = _bdot(_rms(x1, gx_ref[...]), wq_ref[...])


def _attn_step_kernel(q_ref, mk_ref, mv_ref, o_o):
    bt = q_ref.shape[0]
    scale = q_ref.shape[-1] ** -0.5

    def body(i, carry):
        q = q_ref[i]
        s = jnp.sum(mk_ref[i] * q, axis=-1, keepdims=True) * scale
        e = jnp.exp(s - jnp.max(s, axis=0, keepdims=True))
        p = e / jnp.sum(e, axis=0, keepdims=True)
        o_o[i] = jnp.sum(p * mv_ref[i], axis=0)
        return carry

    lax.fori_loop(0, bt, body, 0)


def _ffn_step_kernel(final, *refs):
    it = iter(refs)
    x1_ref, o_ref, wo_ref, gf_ref, wup_ref, fprev_ref, fcw_ref, wdown_ref = (next(it) for _ in range(8))
    gfin_ref = next(it) if final else None
    y_o, fnew_o = (next(it) for _ in range(2))
    f = wdown_ref.shape[0]
    f2 = 2 * f
    x2 = x1_ref[...] + _bdot(o_ref[...], wo_ref[...])
    up = _bdot(_rms(x2, gf_ref[...]), wup_ref[...])
    up2 = fprev_ref[:, 0:f2]
    up1 = fprev_ref[:, f2:2 * f2]
    upc = fcw_ref[0:1, :] * up2 + fcw_ref[1:2, :] * up1 + fcw_ref[2:3, :] * up
    fnew_o[:, 0:f2] = up1
    fnew_o[:, f2:2 * f2] = up
    u = upc[:, 0:f]
    gt = upc[:, f:f2]
    x3 = x2 + _bdot(gt * _sigmoid(gt) * u, wdown_ref[...])
    if final:
        x3 = _rms(x3, gfin_ref[...])
    y_o[...] = x3


def _params(*sem):
    return pltpu.CompilerParams(dimension_semantics=sem, vmem_limit_bytes=V7X_VMEM_LIMIT_BYTES)


def _whole(arr):
    nd = arr.ndim
    return pl.BlockSpec(arr.shape, lambda *_: (0,) * nd, pipeline_mode=pl.Buffered(1))


def _sds(shape):
    return jax.ShapeDtypeStruct(shape, F32)


def _layer_weights(l, p):
    row = lambda a: a.reshape(1, -1)
    rw = p["w0"].shape[1]
    d_decay, d_aaa, d_gate = p["w2"].shape[1], p["a2"].shape[1], p["g2"].shape[1]
    nl = d_decay + d_aaa + d_gate
    w_lora = jnp.zeros((nl, 3 * rw), F32)
    w_lora = w_lora.at[0:d_decay, 0:rw].set(p["w2"][l])
    w_lora = w_lora.at[d_decay:d_decay + d_aaa, rw:2 * rw].set(p["a2"][l])
    w_lora = w_lora.at[d_decay + d_aaa:nl, 2 * rw:3 * rw].set(p["g2"][l])
    w = dict(
        lora_dims=(d_decay, d_aaa, d_gate),
        norm_mix=row(p["norm_mix"][l]), w_in=p["w_in"][l].astype(BF16), mu=row(p["mu_shift"][l]),
        w0=row(p["w0"][l]), a0=row(p["a0"][l]), w_lora=w_lora.astype(BF16), conv_w=p["conv_w"][l],
        k_k=p["k_k"][l], k_a=p["k_a"][l], r_k=p["r_k"][l], ln_w=p["ln_x_w"][l], ln_b=p["ln_x_b"][l],
        w_out=p["w_out"][l].astype(BF16), norm_x=row(p["norm_x"][l]), wq=p["wq"][l].astype(BF16),
        wo=p["wo"][l].astype(BF16), norm_ffn=row(p["norm_ffn"][l]), w_up=p["w_up"][l].astype(BF16),
        ffn_conv_w=p["ffn_conv_w"][l], w_down=p["w_down"][l].astype(BF16),
        norm_mem=row(p["norm_mem"][l]), wk=p["wk"][l].astype(BF16), wv=p["wv"][l].astype(BF16),
        norm_final=row(p["norm_final"]),
    )
    if l > 0:
        lanes = 128
        mv = p["v1"].shape[2]
        mvp = -(-mv // lanes) * lanes
        w["v0"] = row(p["v0"][l - 1])
        w["v1"] = jnp.zeros((rw, mvp), F32).at[:, 0:mv].set(p["v1"][l - 1]).astype(BF16)
        w["v2"] = jnp.zeros((mvp, rw), F32).at[0:mv, :].set(p["v2"][l - 1]).astype(BF16)
    return w


def _mem_kv(mem, w):
    rows, d = mem.shape
    tm = min(512, rows)
    return pl.pallas_call(
        _mem_kv_kernel,
        grid=(rows // tm,),
        in_specs=[pl.BlockSpec((tm, d), lambda i: (i, 0)), _whole(w["norm_mem"]), _whole(w["wk"]), _whole(w["wv"])],
        out_specs=[pl.BlockSpec((tm, d), lambda i: (i, 0))] * 2,
        out_shape=[_sds((rows, d))] * 2,
        compiler_params=_params("parallel"),
        name="mem_kv",
    )(mem, w["norm_mem"], w["wk"], w["wv"])


def _prompt_layer(l, x, mem_k, mem_v, shift0, wkv0, conv0, ffn0, v_first, w, n_heads, n_xheads, final):
    mem_k, mem_v, shift0, wkv0, conv0, ffn0 = (a[l] for a in (mem_k, mem_v, shift0, wkv0, conv0, ffn0))
    bsz, t, d = x.shape
    rw = w["w0"].shape[1]
    hs = rw // n_heads
    cw = w["conv_w"].shape[1]
    rc = w["mu"].shape[1]
    tm = min(ROW_TILE, t)
    nt = t // tm
    has_vmix = v_first is not None

    tile = lambda n: pl.BlockSpec((None, tm, n), lambda b, i: (b, i, 0))
    per_b = lambda s: pl.BlockSpec((None,) + s, lambda b, i: (b,) + (0,) * len(s))

    ins = [x, w["norm_mix"], w["w_in"], shift0[:, None, :], conv0, w["mu"], w["w0"], w["a0"], w["w_lora"],
           w["conv_w"]]
    specs = [tile(d), _whole(w["norm_mix"]), _whole(w["w_in"]), per_b((1, d)), per_b((2, cw)),
             _whole(w["mu"]), _whole(w["w0"]), _whole(w["a0"]), _whole(w["w_lora"]), _whole(w["conv_w"])]
    if has_vmix:
        ins += [v_first, w["v0"], w["v1"], w["v2"]]
        specs += [tile(rw), _whole(w["v0"]), _whole(w["v1"]), _whole(w["v2"])]
    out_shape = [_sds((bsz, t, rw))] * 6 + [_sds((bsz, t, cw)), _sds((bsz, 1, d)), _sds((bsz, 2, cw))]
    out_specs = [tile(rw)] * 6 + [tile(cw), per_b((1, d)), per_b((2, cw))]
    r, k, v, a_sig, ld, gate, y_b, xn_last, conv_new = pl.pallas_call(
        functools.partial(_mix_in_seq_kernel, has_vmix, w["lora_dims"]),
        grid=(bsz, nt), in_specs=specs, out_specs=out_specs, out_shape=out_shape,
        scratch_shapes=[pltpu.VMEM((8, rc), F32), pltpu.VMEM((8, cw), F32)],
        compiler_params=_params("parallel", "arbitrary"),
        name="mix_in_seq",
    )(*ins)
    if not has_vmix:
        v_first = v

    row = lambda a: a.reshape(1, rw)
    sblk = per_b((n_heads, hs, hs))
    y_a, wkv_new = pl.pallas_call(
        functools.partial(_wkv_seq_kernel, hs),
        grid=(bsz, nt),
        in_specs=[tile(rw)] * 5 + [_whole(row(w["k_k"]))] * 5 + [sblk],
        out_specs=[tile(rw), sblk],
        out_shape=[_sds((bsz, t, rw)), _sds((bsz, n_heads, hs, hs))],
        scratch_shapes=[pltpu.VMEM((rw // (2 * hs), 2 * hs, 2 * hs), F32)],
        compiler_params=_params("parallel", "arbitrary"),
        name="wkv_seq",
    )(r, k, v, a_sig, ld, row(w["k_k"]), row(w["k_a"]), row(w["r_k"]), row(w["ln_w"]), row(w["ln_b"]), wkv0)

    m = mem_k.shape[1]
    x2 = pl.pallas_call(
        functools.partial(_mix_out_attn_seq_kernel, n_xheads),
        grid=(bsz, nt),
        in_specs=[tile(d), tile(rw), tile(rw), tile(cw), _whole(w["w_out"]), _whole(w["norm_x"]), _whole(w["wq"]),
                  per_b((m, d)), per_b((m, d)), _whole(w["wo"])],
        out_specs=tile(d), out_shape=_sds((bsz, t, d)),
        compiler_params=_params("parallel", "arbitrary"),
        name="mix_out_attn_seq",
    )(x, y_a, gate, y_b, w["w_out"], w["norm_x"], w["wq"], mem_k, mem_v, w["wo"])

    f2 = w["w_up"].shape[1]
    ins = [x2, w["norm_ffn"], w["w_up"], ffn0, w["ffn_conv_w"], w["w_down"]]
    specs = [tile(d), _whole(w["norm_ffn"]), _whole(w["w_up"]), per_b((2, f2)), _whole(w["ffn_conv_w"]),
             _whole(w["w_down"])]
    if final:
        ins.append(w["norm_final"])
        specs.append(_whole(w["norm_final"]))
    x3, ffn_new = pl.pallas_call(
        functools.partial(_ffn_seq_kernel, final),
        grid=(bsz, nt), in_specs=specs,
        out_specs=[tile(d), per_b((2, f2))], out_shape=[_sds((bsz, t, d)), _sds((bsz, 2, f2))],
        scratch_shapes=[pltpu.VMEM((8, f2), F32)],
        compiler_params=_params("parallel", "arbitrary"),
        name="ffn_seq",
    )(*ins)
    return x3, xn_last[:, 0, :], wkv_new, conv_new, ffn_new, v_first


def _sample_layer(l, x, mem_k, mem_v, shift0, wkv0, conv0, ffn0, v_first, w, n_heads, n_xheads, final):
    shift0, conv0, ffn0 = shift0[l], conv0[l], ffn0[l]
    nb, d = x.shape
    rw = w["w0"].shape[1]
    hs = rw // n_heads
    cw = w["conv_w"].shape[1]
    has_vmix = v_first is not None
    vm = pl.BlockSpec(memory_space=pltpu.VMEM)

    ins = [x, w["norm_mix"], w["w_in"], shift0, conv0.reshape(nb, 2 * cw), w["mu"], w["w0"], w["a0"],
           w["w_lora"], w["conv_w"]]
    if has_vmix:
        ins += [v_first, w["v0"], w["v1"], w["v2"]]
    r, k, v, a_sig, ld, gate, y_b, xn, conv_new = pl.pallas_call(
        functools.partial(_mix_in_step_kernel, has_vmix, w["lora_dims"]),
        in_specs=[vm] * len(ins), out_specs=[vm] * 9,
        out_shape=[_sds((nb, rw))] * 6 + [_sds((nb, cw)), _sds((nb, d)), _sds((nb, 2 * cw))],
        compiler_params=pltpu.CompilerParams(vmem_limit_bytes=V7X_VMEM_LIMIT_BYTES),
        name="mix_in_step",
    )(*ins)
    if not has_vmix:
        v_first = v

    bt = min(WKV_BATCH_TILE, nb)
    hv = lambda a: a.reshape(nb, n_heads, 1, hs)
    hp = lambda a: a.reshape(n_heads, 1, hs)
    vblk = pl.BlockSpec((bt, n_heads, 1, hs), lambda i: (i, 0, 0, 0))
    pblk = pl.BlockSpec((n_heads, 1, hs), lambda i: (0, 0, 0))
    sblk = pl.BlockSpec((bt, n_heads, hs, hs), lambda i: (i, 0, 0, 0))
    s0blk = pl.BlockSpec((None, bt, n_heads, hs, hs), lambda i: (l, i, 0, 0, 0))
    y_a, wkv_new = pl.pallas_call(
        _wkv_step_kernel,
        grid=(nb // bt,),
        in_specs=[vblk] * 5 + [pblk] * 5 + [s0blk],
        out_specs=[vblk, sblk],
        out_shape=[_sds((nb, n_heads, 1, hs)), _sds((nb, n_heads, hs, hs))],
        compiler_params=_params("parallel"),
        name="wkv_step",
    )(hv(r), hv(k), hv(v), hv(a_sig), hv(ld), hp(w["k_k"]), hp(w["k_a"]), hp(w["r_k"]), hp(w["ln_w"]),
      hp(w["ln_b"]), wkv0)

    x1, q = pl.pallas_call(
        _mix_out_q_step_kernel,
        in_specs=[vm] * 7, out_specs=[vm] * 2, out_shape=[_sds((nb, d))] * 2,
        compiler_params=pltpu.CompilerParams(vmem_limit_bytes=V7X_VMEM_LIMIT_BYTES),
        name="mix_out_q_step",
    )(x, y_a.reshape(nb, rw), gate, y_b, w["w_out"], w["norm_x"], w["wq"])

    m, xd = mem_k.shape[2], mem_k.shape[4]
    ab = min(ATTN_BATCH_TILE, nb)
    qblk = pl.BlockSpec((ab, n_xheads, xd), lambda i: (i, 0, 0))
    mblk = pl.BlockSpec((None, ab, m, n_xheads, xd), lambda i: (l, i, 0, 0, 0))
    o = pl.pallas_call(
        _attn_step_kernel,
        grid=(nb // ab,),
        in_specs=[qblk, mblk, mblk], out_specs=qblk, out_shape=_sds((nb, n_xheads, xd)),
        compiler_params=_params("parallel"),
        name="attn_step",
    )(q.reshape(nb, n_xheads, xd), mem_k, mem_v)

    f2 = w["w_up"].shape[1]
    ins = [x1, o.reshape(nb, d), w["wo"], w["norm_ffn"], w["w_up"], ffn0.reshape(nb, 2 * f2), w["ffn_conv_w"],
           w["w_down"]]
    if final:
        ins.append(w["norm_final"])
    x3, ffn_new = pl.pallas_call(
        functools.partial(_ffn_step_kernel, final),
        in_specs=[vm] * len(ins), out_specs=[vm] * 2,
        out_shape=[_sds((nb, d)), _sds((nb, 2 * f2))],
        compiler_params=pltpu.CompilerParams(vmem_limit_bytes=V7X_VMEM_LIMIT_BYTES),
        name="ffn_step",
    )(*ins)
    return x3, xn, wkv_new, conv_new.reshape(nb, 2, cw), ffn_new.reshape(nb, 2, f2), v_first


def _trunk(layer_fn, x, mem_k, mem_v, shift0, wkv0, conv0, ffn0, weights, n_heads, n_xheads):
    depth = len(weights)
    v_first = None
    shs, wks, cvs, ffs = [], [], [], []
    for l in range(depth):
        x, sh, s, cs, fs, v_first = layer_fn(l, x, mem_k, mem_v, shift0, wkv0, conv0, ffn0,
                                             v_first, weights[l], n_heads, n_xheads, l == depth - 1)
        shs.append(sh)
        wks.append(s)
        cvs.append(cs)
        ffs.append(fs)
    return x, jnp.stack(shs), jnp.stack(wks), jnp.stack(cvs), jnp.stack(ffs)


def kernel(x_prompt, x_sample, mem_prompt, state_shift, state_wkv, state_conv, state_ffn, cache_mem_k, cache_mem_v, norm_mix, w_in, mu_shift, w0, w2, a0, a2, g2, v0, v1, v2, k_k, k_a, r_k, ln_x_w, ln_x_b, conv_w, w_out, norm_x, norm_mem, wq, wk, wv, wo, norm_ffn, w_up, ffn_conv_w, w_down, norm_final):
    p = dict(norm_mix=norm_mix, w_in=w_in, mu_shift=mu_shift, w0=w0, w2=w2, a0=a0, a2=a2, g2=g2, v0=v0, v1=v1,
             v2=v2, k_k=k_k, k_a=k_a, r_k=r_k, ln_x_w=ln_x_w, ln_x_b=ln_x_b, conv_w=conv_w, w_out=w_out,
             norm_x=norm_x, norm_mem=norm_mem, wq=wq, wk=wk, wv=wv, wo=wo, norm_ffn=norm_ffn, w_up=w_up,
             ffn_conv_w=ffn_conv_w, w_down=w_down, norm_final=norm_final)
    depth = w_in.shape[0]
    n_heads = state_wkv.shape[2]
    hs = state_wkv.shape[3]
    n_xheads, xd = cache_mem_k.shape[3], cache_mem_k.shape[4]
    weights = [_layer_weights(l, p) for l in range(depth)]

    bp, tp, d = x_prompt.shape
    m = mem_prompt.shape[1]
    mks, mvs = [], []
    for l in range(depth):
        mk, mv = _mem_kv(mem_prompt.reshape(bp * m, d), weights[l])
        mks.append(mk.reshape(bp, m, d))
        mvs.append(mv.reshape(bp, m, d))
    cw = conv_w.shape[2]
    f2 = w_up.shape[2]
    y_p, shift_p, wkv_p, conv_p, ffn_p = _trunk(
        _prompt_layer, x_prompt, mks, mvs,
        jnp.zeros((depth, bp, d), F32), jnp.zeros((depth, bp, n_heads, hs, hs), F32),
        jnp.zeros((depth, bp, state_conv.shape[2], cw), F32), jnp.zeros((depth, bp, state_ffn.shape[2], f2), F32),
        weights, n_heads, n_xheads)
    mem_k_p = jnp.stack(mks).reshape(depth, bp, m, n_xheads, xd)
    mem_v_p = jnp.stack(mvs).reshape(depth, bp, m, n_xheads, xd)

    nb = x_sample.shape[0]
    ms = cache_mem_k.shape[2]
    y_s, shift_s, wkv_s, conv_s, ffn_s = _trunk(
        _sample_layer, x_sample.reshape(nb, d), cache_mem_k, cache_mem_v, state_shift, state_wkv, state_conv,
        state_ffn,
        weights, n_heads, n_xheads)
    return (y_p, y_s.reshape(x_sample.shape), shift_p, wkv_p, conv_p, ffn_p, mem_k_p, mem_v_p,
            shift_s, wkv_s, conv_s, ffn_s)
```

```python
import functools

import jax
import jax.numpy as jnp
from jax import lax
from jax.experimental import pallas as pl
from jax.experimental.pallas import tpu as pltpu

F32 = jnp.float32
BF16 = jnp.bfloat16
HIGHEST = lax.Precision.HIGHEST

RMS_EPS = 1e-6
GROUPNORM_EPS_PER_CHANNEL = 1e-5
KK_NORM_FLOOR = 1e-12

ROW_TILE = 512
WKV_ROW_TILE = 1024
FFN_ROW_TILE = 256
CHUNK = 64
WKV_HEAD_PACK = 2
WKV_BATCH_CHUNKS = 4
ATTN_BATCH_TILE = 2
WKV_BATCH_TILE = 8
FFN_COL_TILE = 256
FFN_LOOKAHEAD = 2
V7X_VMEM_LIMIT_BYTES = 56 * 1024 * 1024
SUBLANES = 8


def _rms(x, g):
    ms = jnp.mean(x * x, axis=-1, keepdims=True)
    return x * lax.rsqrt(ms + RMS_EPS) * g


def _bdot(a, b):
    return jnp.dot(a.astype(BF16), b.astype(BF16), preferred_element_type=F32)


def _hdot(a, b, dims=(((1,), (0,)), ((), ()))):
    return lax.dot_general(a, b, dims, precision=HIGHEST, preferred_element_type=F32)


def _sigmoid(x):
    return 1.0 / (1.0 + jnp.exp(-x))


def _softplus(x):
    return jnp.maximum(x, 0.0) + jnp.log1p(jnp.exp(-jnp.abs(x)))


def _shift_rows(x, prev_rows):
    s = len(prev_rows)
    rolled = pltpu.roll(x, s, axis=0)
    top = rolled[0:SUBLANES, :]
    row = lax.broadcasted_iota(jnp.int32, top.shape, 0)
    for i, pr in enumerate(prev_rows):
        top = jnp.where(row == i, pr, top)
    return jnp.concatenate([top, rolled[SUBLANES:, :]], axis=0)


def _mix_rows(p_cur, p_shift, mu, w0, a0, w_lora, lora_dims, rw, vmix):
    d_decay, d_aaa, _ = lora_dims
    z = p_cur + (p_shift - p_cur) * mu
    r = z[:, 0:rw]
    k = z[:, rw:2 * rw]
    v = z[:, 2 * rw:3 * rw]
    lo = z[:, 3 * rw:]
    lane = lax.broadcasted_iota(jnp.int32, lo.shape, 1)
    feat = jnp.where(lane < d_decay, jnp.tanh(lo),
                     jnp.where(lane < d_decay + d_aaa, lo, _sigmoid(lo)))
    lora = _bdot(feat, w_lora)
    w = -_softplus(-(w0 + lora[:, 0:rw])) - 0.5
    log_decay = -jnp.exp(w)
    a_sig = _sigmoid(a0 + lora[:, rw:2 * rw])
    gate = lora[:, 2 * rw:3 * rw]
    if vmix is not None:
        v_first, v0, v1, v2 = vmix
        vm = _sigmoid(v0 + _bdot(_bdot(v, v1), v2))
        v = v + (v_first - v) * vm
    return r, k, v, a_sig, log_decay, gate


def _group_norm_bonus(y, r, kmod, v, rk, lnw, lnb, n):
    mean = jnp.mean(y, axis=-1, keepdims=True)
    yc = y - mean
    var = jnp.mean(yc * yc, axis=-1, keepdims=True)
    y = yc * lax.rsqrt(var + n * GROUPNORM_EPS_PER_CHANNEL) * lnw + lnb
    return y + jnp.sum(r * kmod * rk, axis=-1, keepdims=True) * v


def _key_features(k, a_sig, kk_w, ka_w):
    kk = k * kk_w
    nrm = jnp.sqrt(jnp.sum(kk * kk, axis=-1, keepdims=True))
    kk = kk / jnp.maximum(nrm, KK_NORM_FLOOR)
    kmod = k * (1.0 + (a_sig - 1.0) * ka_w)
    return kk, kmod


def _softmax_rows(s):
    m = jnp.max(s, axis=-1, keepdims=True)
    e = jnp.exp(s - m)
    return e / jnp.sum(e, axis=-1, keepdims=True)


def _mix_in_seq_kernel(has_vmix, lora_dims, *refs):
    it = iter(refs)
    x_ref, gmix_ref, win_ref, sprev_ref, cprev_ref, mu_ref, w0_ref, a0_ref, wl_ref, cw_ref = (
        next(it) for _ in range(10))
    if has_vmix:
        vf_ref, v0_ref, v1_ref, v2_ref = (next(it) for _ in range(4))
    r_o, k_o, v_o, a_o, ld_o, g_o, yb_o, xn_o, cnew_o, pc_scr, uc_scr = (next(it) for _ in range(11))

    t = pl.program_id(1)
    tm = x_ref.shape[0]
    rc = mu_ref.shape[1]
    cw = cw_ref.shape[1]
    rw = w0_ref.shape[1]

    @pl.when(t == 0)
    def _():
        sprev = jnp.broadcast_to(sprev_ref[...], (8, sprev_ref.shape[1]))
        pc_scr[...] = _bdot(sprev, win_ref[:, 0:rc])
        uc_scr[0:2, :] = cprev_ref[...]

    xn = _rms(x_ref[...], gmix_ref[...])
    proj = _bdot(xn, win_ref[...])
    p_cur = proj[:, 0:rc]
    p_shift = _shift_rows(p_cur, [pc_scr[0:1, :]])
    vmix = None
    if has_vmix:
        vmix = (vf_ref[...], v0_ref[...], v1_ref[...], v2_ref[...])
    r, k, v, a_sig, log_decay, gate = _mix_rows(
        p_cur, p_shift, mu_ref[...], w0_ref[...], a0_ref[...], wl_ref[...], lora_dims, rw, vmix)
    r_o[...] = r
    k_o[...] = k
    v_o[...] = v
    a_o[...] = a_sig
    ld_o[...] = log_decay
    g_o[...] = gate

    gate_b = proj[:, rc:rc + cw]
    gate_c = proj[:, rc + cw:rc + 2 * cw]
    h_in = proj[:, rc + 2 * cw:rc + 3 * cw]
    u = gate_c * h_in
    u1 = _shift_rows(u, [uc_scr[1:2, :]])
    u2 = _shift_rows(u, [uc_scr[0:1, :], uc_scr[1:2, :]])
    conv = cw_ref[0:1, :] * u2 + cw_ref[1:2, :] * u1 + cw_ref[2:3, :] * u
    yb_o[...] = gate_b * conv

    pc_scr[0:1, :] = p_cur[tm - 1:tm, :]
    uc_scr[0:2, :] = u[tm - 2:tm, :]

    @pl.when(t == pl.num_programs(1) - 1)
    def _():
        xn_o[...] = xn[tm - 1:tm, :]
        cnew_o[...] = u[tm - 2:tm, :]


def _split3(x):
    hi = x.astype(BF16)
    r1 = x - hi.astype(F32)
    mid = r1.astype(BF16)
    lo = (r1 - mid.astype(F32)).astype(BF16)
    return hi, mid, lo


def _dot3(a, b, dims=(((1,), (0,)), ((), ()))):
    a_hi = a.astype(BF16)
    a_lo = (a - a_hi.astype(F32)).astype(BF16)
    b_hi = b.astype(BF16)
    b_lo = (b - b_hi.astype(F32)).astype(BF16)
    d = functools.partial(lax.dot_general, dimension_numbers=dims, preferred_element_type=F32)
    return d(a_hi, b_hi) + d(a_lo, b_hi) + d(a_hi, b_lo)


def _dot1(a, b, dims=(((1,), (0,)), ((), ()))):
    return lax.dot_general(a.astype(BF16), b.astype(BF16), dims, preferred_element_type=F32)


_NT = (((1,), (1,)), ((), ()))
_TN = (((0,), (0,)), ((), ()))
_WKV_DOT = _dot1


def _wkv_seq_kernel(hs, r_ref, k_ref, v_ref, a_ref, ld_ref, kkw_ref, kaw_ref, rkw_ref, lnw_ref, lnb_ref,
                    s0_ref, y_o, s_o, st_scr):
    t = pl.program_id(1)
    tt, rw = r_ref.shape
    c = CHUNK
    pack = WKV_HEAD_PACK
    pw_lanes = pack * hs
    n_pairs = rw // pw_lanes
    assert c == hs, "chunk length equals the head size in the packed layout"
    shift = hs.bit_length() - 1
    assert 1 << shift == hs

    lane_head = lax.broadcasted_iota(jnp.int32, (1, pw_lanes), 1) >> shift
    row_p = lax.broadcasted_iota(jnp.int32, (pw_lanes, pw_lanes), 0)
    col_p = lax.broadcasted_iota(jnp.int32, (pw_lanes, pw_lanes), 1)
    same_head = (row_p >> shift) == (col_p >> shift)
    head_ones = jnp.where(same_head, 1.0, 0.0).astype(BF16)
    eye_p = (row_p == col_p).astype(F32)
    row_c = lax.broadcasted_iota(jnp.int32, (c, pw_lanes), 0)
    col_c = lax.broadcasted_iota(jnp.int32, (c, pw_lanes), 1) & (hs - 1)
    tril_strict = row_c > col_c
    tril_incl = row_c >= col_c
    eye_c = (row_c == col_c).astype(F32)
    row_t = lax.broadcasted_iota(jnp.int32, (c, 3 * c), 0)
    col_t = lax.broadcasted_iota(jnp.int32, (c, 3 * c), 1)
    col_t = jnp.where(col_t < c, col_t, jnp.where(col_t < 2 * c, col_t - c, col_t - 2 * c))
    tril3 = jnp.where(row_t >= col_t, 1.0, 0.0).astype(BF16)
    ones2 = jnp.concatenate([head_ones] * 2, axis=0)

    def blockdiag(x):
        return jnp.concatenate([jnp.where(lane_head == i, x, 0.0) for i in range(pack)], axis=0)

    def head_sums(x):
        hi = x.astype(BF16)
        lo = (x - hi.astype(F32)).astype(BF16)
        return jnp.dot(jnp.concatenate([hi, lo], axis=1), ones2, preferred_element_type=F32)

    @pl.when(t == 0)
    def _():
        z = jnp.zeros((hs, hs), F32)
        for p in range(n_pairs):
            blk = jnp.concatenate(
                [jnp.concatenate([s0_ref[pack * p + i] if j == i else z for j in range(pack)], axis=1)
                 for i in range(pack)], axis=0)
            st_scr[p] = blk.T

    n_chunks = tt // c
    cat = jnp.concatenate
    dot = _WKV_DOT

    def each(f, *lists):
        return [f(*xs) for xs in zip(*lists)]

    def cumsum_rows(x):
        hi, mid, lo = _split3(x)
        return jnp.dot(tril3, cat([hi, mid, lo], axis=0), preferred_element_type=F32)

    def prepare(chunks, out):
        items = [(ci, p) for ci in chunks for p in range(n_pairs)]
        tiles = lambda ref: [ref[ci * c:(ci + 1) * c, p * pw_lanes:(p + 1) * pw_lanes] for ci, p in items]
        params = lambda ref: [ref[:, p * pw_lanes:(p + 1) * pw_lanes] for _, p in items]
        r, k, v, a_sig, ld = (tiles(ref) for ref in (r_ref, k_ref, v_ref, a_ref, ld_ref))
        kkw, kaw, rkw = (params(ref) for ref in (kkw_ref, kaw_ref, rkw_ref))
        kk = each(lambda k_, w_: k_ * w_, k, kkw)
        kmod = each(lambda k_, a_, w_: k_ * (1.0 + (a_ - 1.0) * w_), k, a_sig, kaw)
        sums = each(lambda kk_, r_, km_, w_: head_sums(cat([kk_ * kk_, r_ * km_ * w_], axis=0)), kk, r, kmod, rkw)
        yield
        kk = each(lambda kk_, s_: kk_ / jnp.maximum(jnp.sqrt(s_[0:c]), KK_NORM_FLOOR), kk, sums)
        b = each(lambda kk_, a_: kk_ * a_, kk, a_sig)
        cum = each(cumsum_rows, ld)
        yield
        at = each(lambda kk_, cu, l_: -kk_ * jnp.exp(cu - l_), kk, cum, ld)
        rt = each(lambda r_, cu: r_ * jnp.exp(cu), r, cum)
        inv = each(lambda cu: jnp.exp(-cu), cum)
        bt = each(lambda b_, i_: b_ * i_, b, inv)
        kt = each(lambda km_, i_: km_ * i_, kmod, inv)
        yield
        to_end = each(lambda cu: jnp.exp(cu[c - 1:c, :] - cu), cum)
        bk_end = each(lambda b_, km_, te: cat([b_ * te, km_ * te], axis=0), b, kmod, to_end)
        p_end_col = each(lambda cu: jnp.sum(eye_p * jnp.exp(cu[c - 1:c, :]), axis=-1, keepdims=True), cum)
        gm = each(lambda at_, rt_, bt_, kt_: dot(cat([at_, rt_], axis=0),
                                                 cat([blockdiag(bt_), blockdiag(kt_)], axis=0), _NT),
                  at, rt, bt, kt)
        yield
        pc = pack * c
        l_ab = [jnp.where(tril_strict, g[0:c, 0:pc], 0.0) for g in gm]
        l_ak = [jnp.where(tril_strict, g[0:c, pc:2 * pc], 0.0) for g in gm]
        m_rb = [jnp.where(tril_incl, g[c:2 * c, 0:pc], 0.0) for g in gm]
        m_rk = [jnp.where(tril_incl, g[c:2 * c, pc:2 * pc], 0.0) for g in gm]
        t_inv = [eye_c + l_ for l_ in l_ab]
        pw = l_ab
        span = 2
        while span < c:
            pw = each(lambda q: dot(q, blockdiag(q)), pw)
            yield
            t_inv = each(lambda t_, q: t_ + dot(q, blockdiag(t_)), t_inv, pw)
            yield
            span *= 2
        out.update(items=items, r=r, v=v, kmod=kmod, bonus=[s_[c:2 * c] for s_ in sums], at=at, rt=rt,
                   l_ak=l_ak, m_rb=m_rb, m_rk=m_rk, t_inv=t_inv, bk_end=bk_end, p_end_col=p_end_col)

    def state_pass(d):
        items = d["items"]
        y = []
        for i0 in range(0, len(items), n_pairs):
            idx = range(i0, i0 + n_pairs)
            st = [st_scr[p] for p in range(n_pairs)]
            vb = [blockdiag(d["v"][i]) for i in idx]
            x = [dot(cat([d["at"][i], d["l_ak"][i]], axis=1), cat([st[p], vb[p]], axis=0))
                 for p, i in enumerate(idx)]
            yield
            u = [dot(d["t_inv"][i], blockdiag(x[p])) for p, i in enumerate(idx)]
            yield
            y += [dot(cat([d["rt"][i], d["m_rk"][i], d["m_rb"][i]], axis=1),
                      cat([st[p], vb[p], blockdiag(u[p])], axis=0)) for p, i in enumerate(idx)]
            upd = [dot(d["bk_end"][i], cat([u[p], d["v"][i]], axis=0), _TN) for p, i in enumerate(idx)]
            for p, i in enumerate(idx):
                st_scr[p] = d["p_end_col"][i] * st[p] + jnp.where(same_head, upd[p], 0.0)
            yield
        s2 = each(lambda y_: head_sums(cat([y_, y_ * y_], axis=0)), y)
        yield
        for (ci, p), y_, s_, bo_, v_ in zip(items, y, s2, d["bonus"], d["v"]):
            ln = slice(p * pw_lanes, (p + 1) * pw_lanes)
            mean = s_[0:c] * (1.0 / hs)
            var = s_[c:2 * c] * (1.0 / hs) - mean * mean
            yn = (y_ - mean) * lax.rsqrt(var + hs * GROUPNORM_EPS_PER_CHANNEL) * lnw_ref[:, ln] + lnb_ref[:, ln]
            y_o[ci * c:(ci + 1) * c, ln] = yn + bo_ * v_

    def run_together(*progs):
        progs = list(progs)
        while progs:
            for prog in list(progs):
                if next(prog, StopIteration) is StopIteration:
                    progs.remove(prog)

    halves = [range(h, min(h + WKV_BATCH_CHUNKS, n_chunks)) for h in range(0, n_chunks, WKV_BATCH_CHUNKS)]
    prev = None
    for chunks in halves:
        cur = {}
        if prev is None:
            run_together(prepare(chunks, cur))
        else:
            run_together(prepare(chunks, cur), state_pass(prev))
        prev = cur
    run_together(state_pass(prev))

    @pl.when(t == pl.num_programs(1) - 1)
    def _():
        for p in range(n_pairs):
            s_grp = st_scr[p].T
            for i in range(pack):
                s_o[pack * p + i] = s_grp[i * hs:(i + 1) * hs, i * hs:(i + 1) * hs]


def _mix_out_attn_seq_kernel(n_xheads, x_ref, y_ref, g_ref, yb_ref, wout_ref, gx_ref, wq_ref,
                             mk_ref, mv_ref, wo_ref, x2_o):
    mixed = jnp.concatenate([y_ref[...] * g_ref[...], yb_ref[...]], axis=-1)
    x1 = x_ref[...] + _bdot(mixed, wout_ref[...])
    q = _bdot(_rms(x1, gx_ref[...]), wq_ref[...])
    d = q.shape[1]
    xd = d // n_xheads
    scale = xd ** -0.5
    outs = []
    for h in range(n_xheads):
        sl = slice(h * xd, (h + 1) * xd)
        s = lax.dot_general(q[:, sl].astype(BF16), mk_ref[:, sl].astype(BF16),
                            (((1,), (1,)), ((), ())), preferred_element_type=F32) * scale
        outs.append(_bdot(_softmax_rows(s), mv_ref[:, sl]))
    o = jnp.concatenate(outs, axis=-1)
    x2_o[...] = x1 + _bdot(o, wo_ref[...])


def _ffn_seq_kernel(final, *refs):
    it = iter(refs)
    x_ref, gf_ref, wup_ref, fprev_ref, fcw_ref, wdown_ref = (next(it) for _ in range(6))
    gfin_ref = next(it) if final else None
    y_o, fnew_o, carry = (next(it) for _ in range(3))

    t = pl.program_id(1)
    tm = x_ref.shape[0]
    f = wdown_ref.shape[0]
    fc = FFN_COL_TILE

    @pl.when(t == 0)
    def _():
        carry[0:2, :] = fprev_ref[...]

    x = x_ref[...]
    xf = _rms(x, gf_ref[...]).astype(BF16)
    acc = x

    def up_cols(c0):
        return jnp.dot(xf, wup_ref[:, c0:c0 + fc], preferred_element_type=F32)

    def conv_cols(up, c0):
        p0 = carry[0:1, c0:c0 + fc]
        p1 = carry[1:2, c0:c0 + fc]
        up1 = _shift_rows(up, [p1])
        up2 = _shift_rows(up, [p0, p1])
        carry[0:2, c0:c0 + fc] = up[tm - 2:tm, :]
        return (fcw_ref[0:1, c0:c0 + fc] * up2 + fcw_ref[1:2, c0:c0 + fc] * up1
                + fcw_ref[2:3, c0:c0 + fc] * up)

    n_tiles = f // fc
    ahead = [(up_cols(i * fc), up_cols(f + i * fc)) for i in range(min(FFN_LOOKAHEAD, n_tiles))]
    for j in range(n_tiles):
        c0 = j * fc
        cur = ahead.pop(0)
        if j + FFN_LOOKAHEAD < n_tiles:
            ahead.append((up_cols(c0 + FFN_LOOKAHEAD * fc), up_cols(f + c0 + FFN_LOOKAHEAD * fc)))
        u = conv_cols(cur[0], c0)
        gt = conv_cols(cur[1], f + c0)
        hidden = gt * _sigmoid(gt) * u
        acc = acc + _bdot(hidden, wdown_ref[c0:c0 + fc, :])
    if final:
        acc = _rms(acc, gfin_ref[...])
    y_o[...] = acc

    @pl.when(t == pl.num_programs(1) - 1)
    def _():
        fnew_o[...] = carry[0:2, :]


def _mem_kv_kernel(m_ref, g_ref, wk_ref, wv_ref, k_o, v_o):
    mn = _rms(m_ref[...], g_ref[...])
    k_o[...] = _bdot(mn, wk_ref[...])
    v_o[...] = _bdot(mn, wv_ref[...])


def _mix_in_step_kernel(has_vmix, lora_dims, *refs):
    it = iter(refs)
    x_ref, gmix_ref, win_ref, sprev_ref, cprev_ref, mu_ref, w0_ref, a0_ref, wl_ref, cw_ref = (
        next(it) for _ in range(10))
    if has_vmix:
        vf_ref, v0_ref, v1_ref, v2_ref = (next(it) for _ in range(4))
    r_o, k_o, v_o, a_o, ld_o, g_o, yb_o, xn_o, cnew_o = (next(it) for _ in range(9))

    nb = x_ref.shape[0]
    rc = mu_ref.shape[1]
    cw = cw_ref.shape[1]
    rw = w0_ref.shape[1]

    xn = _rms(x_ref[...], gmix_ref[...])
    xn_o[...] = xn
    stacked = jnp.concatenate([xn, sprev_ref[...]], axis=0)
    proj = _bdot(stacked, win_ref[...])
    p_cur = proj[0:nb, 0:rc]
    p_shift = proj[nb:2 * nb, 0:rc]
    vmix = None
    if has_vmix:
        vmix = (vf_ref[...], v0_ref[...], v1_ref[...], v2_ref[...])
    r, k, v, a_sig, log_decay, gate = _mix_rows(
        p_cur, p_shift, mu_ref[...], w0_ref[...], a0_ref[...], wl_ref[...], lora_dims, rw, vmix)
    r_o[...] = r
    k_o[...] = k
    v_o[...] = v
    a_o[...] = a_sig
    ld_o[...] = log_decay
    g_o[...] = gate

    gate_b = proj[0:nb, rc:rc + cw]
    gate_c = proj[0:nb, rc + cw:rc + 2 * cw]
    h_in = proj[0:nb, rc + 2 * cw:rc + 3 * cw]
    u = gate_c * h_in
    u2 = cprev_ref[:, 0:cw]
    u1 = cprev_ref[:, cw:2 * cw]
    yb_o[...] = gate_b * (cw_ref[0:1, :] * u2 + cw_ref[1:2, :] * u1 + cw_ref[2:3, :] * u)
    cnew_o[:, 0:cw] = u1
    cnew_o[:, cw:2 * cw] = u


def _wkv_step_kernel(r_ref, k_ref, v_ref, a_ref, ld_ref, kkw_ref, kaw_ref, rkw_ref, lnw_ref, lnb_ref,
                     s_ref, y_o, s_o):
    n = s_ref.shape[-1]
    r, k, v, a_sig, ld = r_ref[...], k_ref[...], v_ref[...], a_ref[...], ld_ref[...]
    kk, kmod = _key_features(k, a_sig, kkw_ref[...], kaw_ref[...])
    a = -kk
    b = kk * a_sig
    s = s_ref[...]
    row = lax.broadcasted_iota(jnp.int32, (n, n), 0)
    col = lax.broadcasted_iota(jnp.int32, (n, n), 1)
    eye = (row == col).astype(F32)
    v_col = jnp.sum(eye * v, axis=-1, keepdims=True)
    sa = jnp.sum(s * a, axis=-1, keepdims=True)
    s_new = s * jnp.exp(ld) + sa * b + v_col * kmod
    s_o[...] = s_new
    y_col = jnp.sum(s_new * r, axis=-1, keepdims=True)
    y = jnp.sum(eye * y_col, axis=-2, keepdims=True)
    y_o[...] = _group_norm_bonus(y, r, kmod, v, rkw_ref[...], lnw_ref[...], lnb_ref[...], n)


def _mix_out_q_step_kernel(x_ref, ya_ref, g_ref, yb_ref, wout_ref, gx_ref, wq_ref, x1_o, q_o):
    mixed = jnp.concatenate([ya_ref[...] * g_ref[...], yb_ref[...]], axis=-1)
    x1 = x_ref[...] + _bdot(mixed, wout_ref[...])
    x1_o[...] = x1
    q_o[...] = _bdot(_rms(x1, gx_ref[...]), wq_ref[...])


def _attn_step_kernel(q_ref, mk_ref, mv_ref, o_o):
    bt = q_ref.shape[0]
    scale = q_ref.shape[-1] ** -0.5

    def body(i, carry):
        q = q_ref[i]
        s = jnp.sum(mk_ref[i] * q, axis=-1, keepdims=True) * scale
        e = jnp.exp(s - jnp.max(s, axis=0, keepdims=True))
        p = e / jnp.sum(e, axis=0, keepdims=True)
        o_o[i] = jnp.sum(p * mv_ref[i], axis=0)
        return carry

    lax.fori_loop(0, bt, body, 0)


def _ffn_step_kernel(final, *refs):
    it = iter(refs)
    x1_ref, o_ref, wo_ref, gf_ref, wup_ref, fprev_ref, fcw_ref, wdown_ref = (next(it) for _ in range(8))
    gfin_ref = next(it) if final else None
    y_o, fnew_o = (next(it) for _ in range(2))
    f = wdown_ref.shape[0]
    f2 = 2 * f
    x2 = x1_ref[...] + _bdot(o_ref[...], wo_ref[...])
    up = _bdot(_rms(x2, gf_ref[...]), wup_ref[...])
    up2 = fprev_ref[:, 0:f2]
    up1 = fprev_ref[:, f2:2 * f2]
    upc = fcw_ref[0:1, :] * up2 + fcw_ref[1:2, :] * up1 + fcw_ref[2:3, :] * up
    fnew_o[:, 0:f2] = up1
    fnew_o[:, f2:2 * f2] = up
    u = upc[:, 0:f]
    gt = upc[:, f:f2]
    x3 = x2 + _bdot(gt * _sigmoid(gt) * u, wdown_ref[...])
    if final:
        x3 = _rms(x3, gfin_ref[...])
    y_o[...] = x3


def _params(*sem):
    return pltpu.CompilerParams(dimension_semantics=sem, vmem_limit_bytes=V7X_VMEM_LIMIT_BYTES)


def _whole(arr):
    nd = arr.ndim
    return pl.BlockSpec(arr.shape, lambda *_: (0,) * nd, pipeline_mode=pl.Buffered(1))


def _sds(shape):
    return jax.ShapeDtypeStruct(shape, F32)


def _layer_weights(l, p):
    row = lambda a: a.reshape(1, -1)
    rw = p["w0"].shape[1]
    d_decay, d_aaa, d_gate = p["w2"].shape[1], p["a2"].shape[1], p["g2"].shape[1]
    nl = d_decay + d_aaa + d_gate
    w_lora = jnp.zeros((nl, 3 * rw), F32)
    w_lora = w_lora.at[0:d_decay, 0:rw].set(p["w2"][l])
    w_lora = w_lora.at[d_decay:d_decay + d_aaa, rw:2 * rw].set(p["a2"][l])
    w_lora = w_lora.at[d_decay + d_aaa:nl, 2 * rw:3 * rw].set(p["g2"][l])
    w = dict(
        lora_dims=(d_decay, d_aaa, d_gate),
        norm_mix=row(p["norm_mix"][l]), w_in=p["w_in"][l].astype(BF16), mu=row(p["mu_shift"][l]),
        w0=row(p["w0"][l]), a0=row(p["a0"][l]), w_lora=w_lora.astype(BF16), conv_w=p["conv_w"][l],
        k_k=p["k_k"][l], k_a=p["k_a"][l], r_k=p["r_k"][l], ln_w=p["ln_x_w"][l], ln_b=p["ln_x_b"][l],
        w_out=p["w_out"][l].astype(BF16), norm_x=row(p["norm_x"][l]), wq=p["wq"][l].astype(BF16),
        wo=p["wo"][l].astype(BF16), norm_ffn=row(p["norm_ffn"][l]), w_up=p["w_up"][l].astype(BF16),
        ffn_conv_w=p["ffn_conv_w"][l], w_down=p["w_down"][l].astype(BF16),
        norm_mem=row(p["norm_mem"][l]), wk=p["wk"][l].astype(BF16), wv=p["wv"][l].astype(BF16),
        norm_final=row(p["norm_final"]),
    )
    if l > 0:
        lanes = 128
        mv = p["v1"].shape[2]
        mvp = -(-mv // lanes) * lanes
        w["v0"] = row(p["v0"][l - 1])
        w["v1"] = jnp.zeros((rw, mvp), F32).at[:, 0:mv].set(p["v1"][l - 1]).astype(BF16)
        w["v2"] = jnp.zeros((mvp, rw), F32).at[0:mv, :].set(p["v2"][l - 1]).astype(BF16)
    return w


def _mem_kv(mem, w):
    rows, d = mem.shape
    tm = min(512, rows)
    return pl.pallas_call(
        _mem_kv_kernel,
        grid=(rows // tm,),
        in_specs=[pl.BlockSpec((tm, d), lambda i: (i, 0)), _whole(w["norm_mem"]), _whole(w["wk"]), _whole(w["wv"])],
        out_specs=[pl.BlockSpec((tm, d), lambda i: (i, 0))] * 2,
        out_shape=[_sds((rows, d))] * 2,
        compiler_params=_params("parallel"),
        name="mem_kv",
    )(mem, w["norm_mem"], w["wk"], w["wv"])


def _prompt_layer(l, x, mem_k, mem_v, shift0, wkv0, conv0, ffn0, v_first, w, n_heads, n_xheads, final):
    mem_k, mem_v, shift0, wkv0, conv0, ffn0 = (a[l] for a in (mem_k, mem_v, shift0, wkv0, conv0, ffn0))
    bsz, t, d = x.shape
    rw = w["w0"].shape[1]
    hs = rw // n_heads
    cw = w["conv_w"].shape[1]
    rc = w["mu"].shape[1]
    tm = min(ROW_TILE, t)
    nt = t // tm
    has_vmix = v_first is not None

    tile = lambda n: pl.BlockSpec((None, tm, n), lambda b, i: (b, i, 0))
    per_b = lambda s: pl.BlockSpec((None,) + s, lambda b, i: (b,) + (0,) * len(s))

    ins = [x, w["norm_mix"], w["w_in"], shift0[:, None, :], conv0, w["mu"], w["w0"], w["a0"], w["w_lora"],
           w["conv_w"]]
    specs = [tile(d), _whole(w["norm_mix"]), _whole(w["w_in"]), per_b((1, d)), per_b((2, cw)),
             _whole(w["mu"]), _whole(w["w0"]), _whole(w["a0"]), _whole(w["w_lora"]), _whole(w["conv_w"])]
    if has_vmix:
        ins += [v_first, w["v0"], w["v1"], w["v2"]]
        specs += [tile(rw), _whole(w["v0"]), _whole(w["v1"]), _whole(w["v2"])]
    out_shape = [_sds((bsz, t, rw))] * 6 + [_sds((bsz, t, cw)), _sds((bsz, 1, d)), _sds((bsz, 2, cw))]
    out_specs = [tile(rw)] * 6 + [tile(cw), per_b((1, d)), per_b((2, cw))]
    r, k, v, a_sig, ld, gate, y_b, xn_last, conv_new = pl.pallas_call(
        functools.partial(_mix_in_seq_kernel, has_vmix, w["lora_dims"]),
        grid=(bsz, nt), in_specs=specs, out_specs=out_specs, out_shape=out_shape,
        scratch_shapes=[pltpu.VMEM((8, rc), F32), pltpu.VMEM((8, cw), F32)],
        compiler_params=_params("parallel", "arbitrary"),
        name="mix_in_seq",
    )(*ins)
    if not has_vmix:
        v_first = v

    row = lambda a: a.reshape(1, rw)
    sblk = per_b((n_heads, hs, hs))
    wt = min(WKV_ROW_TILE, t)
    wtile = pl.BlockSpec((None, wt, rw), lambda b, i: (b, i, 0))
    y_a, wkv_new = pl.pallas_call(
        functools.partial(_wkv_seq_kernel, hs),
        grid=(bsz, t // wt),
        in_specs=[wtile] * 5 + [_whole(row(w["k_k"]))] * 5 + [sblk],
        out_specs=[wtile, sblk],
        out_shape=[_sds((bsz, t, rw)), _sds((bsz, n_heads, hs, hs))],
        scratch_shapes=[pltpu.VMEM((rw // (WKV_HEAD_PACK * hs), WKV_HEAD_PACK * hs, WKV_HEAD_PACK * hs), F32)],
        compiler_params=_params("parallel", "arbitrary"),
        name="wkv_seq",
    )(r, k, v, a_sig, ld, row(w["k_k"]), row(w["k_a"]), row(w["r_k"]), row(w["ln_w"]), row(w["ln_b"]), wkv0)

    m = mem_k.shape[1]
    x2 = pl.pallas_call(
        functools.partial(_mix_out_attn_seq_kernel, n_xheads),
        grid=(bsz, nt),
        in_specs=[tile(d), tile(rw), tile(rw), tile(cw), _whole(w["w_out"]), _whole(w["norm_x"]), _whole(w["wq"]),
                  per_b((m, d)), per_b((m, d)), _whole(w["wo"])],
        out_specs=tile(d), out_shape=_sds((bsz, t, d)),
        compiler_params=_params("parallel", "arbitrary"),
        name="mix_out_attn_seq",
    )(x, y_a, gate, y_b, w["w_out"], w["norm_x"], w["wq"], mem_k, mem_v, w["wo"])

    f2 = w["w_up"].shape[1]
    ins = [x2, w["norm_ffn"], w["w_up"], ffn0, w["ffn_conv_w"], w["w_down"]]
    ft = min(FFN_ROW_TILE, t)
    ftile = pl.BlockSpec((None, ft, d), lambda b, i: (b, i, 0))
    specs = [ftile, _whole(w["norm_ffn"]), _whole(w["w_up"]), per_b((2, f2)), _whole(w["ffn_conv_w"]),
             _whole(w["w_down"])]
    if final:
        ins.append(w["norm_final"])
        specs.append(_whole(w["norm_final"]))
    x3, ffn_new = pl.pallas_call(
        functools.partial(_ffn_seq_kernel, final),
        grid=(bsz, t // ft), in_specs=specs,
        out_specs=[ftile, per_b((2, f2))], out_shape=[_sds((bsz, t, d)), _sds((bsz, 2, f2))],
        scratch_shapes=[pltpu.VMEM((8, f2), F32)],
        compiler_params=_params("parallel", "arbitrary"),
        name="ffn_seq",
    )(*ins)
    return x3, xn_last[:, 0, :], wkv_new, conv_new, ffn_new, v_first


def _sample_layer(l, x, mem_k, mem_v, shift0, wkv0, conv0, ffn0, v_first, w, n_heads, n_xheads, final):
    shift0, conv0, ffn0 = shift0[l], conv0[l], ffn0[l]
    nb, d = x.shape
    rw = w["w0"].shape[1]
    hs = rw // n_heads
    cw = w["conv_w"].shape[1]
    has_vmix = v_first is not None
    vm = pl.BlockSpec(memory_space=pltpu.VMEM)

    ins = [x, w["norm_mix"], w["w_in"], shift0, conv0.reshape(nb, 2 * cw), w["mu"], w["w0"], w["a0"],
           w["w_lora"], w["conv_w"]]
    if has_vmix:
        ins += [v_first, w["v0"], w["v1"], w["v2"]]
    r, k, v, a_sig, ld, gate, y_b, xn, conv_new = pl.pallas_call(
        functools.partial(_mix_in_step_kernel, has_vmix, w["lora_dims"]),
        in_specs=[vm] * len(ins), out_specs=[vm] * 9,
        out_shape=[_sds((nb, rw))] * 6 + [_sds((nb, cw)), _sds((nb, d)), _sds((nb, 2 * cw))],
        compiler_params=pltpu.CompilerParams(vmem_limit_bytes=V7X_VMEM_LIMIT_BYTES),
        name="mix_in_step",
    )(*ins)
    if not has_vmix:
        v_first = v

    bt = min(WKV_BATCH_TILE, nb)
    hv = lambda a: a.reshape(nb, n_heads, 1, hs)
    hp = lambda a: a.reshape(n_heads, 1, hs)
    vblk = pl.BlockSpec((bt, n_heads, 1, hs), lambda i: (i, 0, 0, 0))
    pblk = pl.BlockSpec((n_heads, 1, hs), lambda i: (0, 0, 0))
    sblk = pl.BlockSpec((bt, n_heads, hs, hs), lambda i: (i, 0, 0, 0))
    s0blk = pl.BlockSpec((None, bt, n_heads, hs, hs), lambda i: (l, i, 0, 0, 0))
    y_a, wkv_new = pl.pallas_call(
        _wkv_step_kernel,
        grid=(nb // bt,),
        in_specs=[vblk] * 5 + [pblk] * 5 + [s0blk],
        out_specs=[vblk, sblk],
        out_shape=[_sds((nb, n_heads, 1, hs)), _sds((nb, n_heads, hs, hs))],
        compiler_params=_params("parallel"),
        name="wkv_step",
    )(hv(r), hv(k), hv(v), hv(a_sig), hv(ld), hp(w["k_k"]), hp(w["k_a"]), hp(w["r_k"]), hp(w["ln_w"]),
      hp(w["ln_b"]), wkv0)

    x1, q = pl.pallas_call(
        _mix_out_q_step_kernel,
        in_specs=[vm] * 7, out_specs=[vm] * 2, out_shape=[_sds((nb, d))] * 2,
        compiler_params=pltpu.CompilerParams(vmem_limit_bytes=V7X_VMEM_LIMIT_BYTES),
        name="mix_out_q_step",
    )(x, y_a.reshape(nb, rw), gate, y_b, w["w_out"], w["norm_x"], w["wq"])

    m, xd = mem_k.shape[2], mem_k.shape[4]
    ab = min(ATTN_BATCH_TILE, nb)
    qblk = pl.BlockSpec((ab, n_xheads, xd), lambda i: (i, 0, 0))
    mblk = pl.BlockSpec((None, ab, m, n_xheads, xd), lambda i: (l, i, 0, 0, 0))
    o = pl.pallas_call(
        _attn_step_kernel,
        grid=(nb // ab,),
        in_specs=[qblk, mblk, mblk], out_specs=qblk, out_shape=_sds((nb, n_xheads, xd)),
        compiler_params=_params("parallel"),
        name="attn_step",
    )(q.reshape(nb, n_xheads, xd), mem_k, mem_v)

    f2 = w["w_up"].shape[1]
    ins = [x1, o.reshape(nb, d), w["wo"], w["norm_ffn"], w["w_up"], ffn0.reshape(nb, 2 * f2), w["ffn_conv_w"],
           w["w_down"]]
    if final:
        ins.append(w["norm_final"])
    x3, ffn_new = pl.pallas_call(
        functools.partial(_ffn_step_kernel, final),
        in_specs=[vm] * len(ins), out_specs=[vm] * 2,
        out_shape=[_sds((nb, d)), _sds((nb, 2 * f2))],
        compiler_params=pltpu.CompilerParams(vmem_limit_bytes=V7X_VMEM_LIMIT_BYTES),
        name="ffn_step",
    )(*ins)
    return x3, xn, wkv_new, conv_new.reshape(nb, 2, cw), ffn_new.reshape(nb, 2, f2), v_first


def _trunk(layer_fn, x, mem_k, mem_v, shift0, wkv0, conv0, ffn0, weights, n_heads, n_xheads):
    depth = len(weights)
    v_first = None
    shs, wks, cvs, ffs = [], [], [], []
    for l in range(depth):
        x, sh, s, cs, fs, v_first = layer_fn(l, x, mem_k, mem_v, shift0, wkv0, conv0, ffn0,
                                             v_first, weights[l], n_heads, n_xheads, l == depth - 1)
        shs.append(sh)
        wks.append(s)
        cvs.append(cs)
        ffs.append(fs)
    return x, jnp.stack(shs), jnp.stack(wks), jnp.stack(cvs), jnp.stack(ffs)


def kernel(x_prompt, x_sample, mem_prompt, state_shift, state_wkv, state_conv, state_ffn, cache_mem_k, cache_mem_v, norm_mix, w_in, mu_shift, w0, w2, a0, a2, g2, v0, v1, v2, k_k, k_a, r_k, ln_x_w, ln_x_b, conv_w, w_out, norm_x, norm_mem, wq, wk, wv, wo, norm_ffn, w_up, ffn_conv_w, w_down, norm_final):
    p = dict(norm_mix=norm_mix, w_in=w_in, mu_shift=mu_shift, w0=w0, w2=w2, a0=a0, a2=a2, g2=g2, v0=v0, v1=v1,
             v2=v2, k_k=k_k, k_a=k_a, r_k=r_k, ln_x_w=ln_x_w, ln_x_b=ln_x_b, conv_w=conv_w, w_out=w_out,
             norm_x=norm_x, norm_mem=norm_mem, wq=wq, wk=wk, wv=wv, wo=wo, norm_ffn=norm_ffn, w_up=w_up,
             ffn_conv_w=ffn_conv_w, w_down=w_down, norm_final=norm_final)
    depth = w_in.shape[0]
    n_heads = state_wkv.shape[2]
    hs = state_wkv.shape[3]
    n_xheads, xd = cache_mem_k.shape[3], cache_mem_k.shape[4]
    weights = [_layer_weights(l, p) for l in range(depth)]

    bp, tp, d = x_prompt.shape
    m = mem_prompt.shape[1]
    mks, mvs = [], []
    for l in range(depth):
        mk, mv = _mem_kv(mem_prompt.reshape(bp * m, d), weights[l])
        mks.append(mk.reshape(bp, m, d))
        mvs.append(mv.reshape(bp, m, d))
    cw = conv_w.shape[2]
    f2 = w_up.shape[2]
    y_p, shift_p, wkv_p, conv_p, ffn_p = _trunk(
        _prompt_layer, x_prompt, mks, mvs,
        jnp.zeros((depth, bp, d), F32), jnp.zeros((depth, bp, n_heads, hs, hs), F32),
        jnp.zeros((depth, bp, state_conv.shape[2], cw), F32), jnp.zeros((depth, bp, state_ffn.shape[2], f2), F32),
        weights, n_heads, n_xheads)
    mem_k_p = jnp.stack(mks).reshape(depth, bp, m, n_xheads, xd)
    mem_v_p = jnp.stack(mvs).reshape(depth, bp, m, n_xheads, xd)

    nb = x_sample.shape[0]
    ms = cache_mem_k.shape[2]
    y_s, shift_s, wkv_s, conv_s, ffn_s = _trunk(
        _sample_layer, x_sample.reshape(nb, d), cache_mem_k, cache_mem_v, state_shift, state_wkv, state_conv,
        state_ffn,
        weights, n_heads, n_xheads)
    return (y_p, y_s.reshape(x_sample.shape), shift_p, wkv_p, conv_p, ffn_p, mem_k_p, mem_v_p,
            shift_s, wkv_s, conv_s, ffn_s)
```

```python
import functools
from typing import NamedTuple

import jax
import jax.numpy as jnp
from jax import lax
from jax.experimental import pallas as pl
from jax.experimental.pallas import tpu as pltpu

F32 = jnp.float32
BF16 = jnp.bfloat16
HIGHEST = lax.Precision.HIGHEST

RMS_EPS = 1e-6
GROUPNORM_EPS_PER_CHANNEL = 1e-5
KK_NORM_FLOOR = 1e-12

ROW_TILE = 512
WKV_ROW_TILE = 1024
FFN_ROW_TILE = 256
CHUNK = 64
WKV_HEAD_PACK = 2
WKV_BATCH_CHUNKS = 4
ATTN_BATCH_TILE = 2
WKV_BATCH_TILE = 8
FFN_COL_TILE = 256
FFN_LOOKAHEAD = 2
V7X_VMEM_LIMIT_BYTES = 56 * 1024 * 1024
SUBLANES = 8


def _rms(x, g):
    ms = jnp.mean(x * x, axis=-1, keepdims=True)
    return x * lax.rsqrt(ms + RMS_EPS) * g


def _bdot(a, b):
    return jnp.dot(a.astype(BF16), b.astype(BF16), preferred_element_type=F32)


def _hdot(a, b, dims=(((1,), (0,)), ((), ()))):
    return lax.dot_general(a, b, dims, precision=HIGHEST, preferred_element_type=F32)


def _sigmoid(x):
    return 1.0 / (1.0 + jnp.exp(-x))


def _softplus(x):
    return jnp.maximum(x, 0.0) + jnp.log1p(jnp.exp(-jnp.abs(x)))


def _shift_rows(x, prev_rows):
    s = len(prev_rows)
    rolled = pltpu.roll(x, s, axis=0)
    top = rolled[0:SUBLANES, :]
    row = lax.broadcasted_iota(jnp.int32, top.shape, 0)
    for i, pr in enumerate(prev_rows):
        top = jnp.where(row == i, pr, top)
    return jnp.concatenate([top, rolled[SUBLANES:, :]], axis=0)


def _mix_rows(p_cur, p_shift, mu, w0, a0, w_lora, lora_dims, rw, vmix):
    d_decay, d_aaa, _ = lora_dims
    z = p_cur + (p_shift - p_cur) * mu
    r = z[:, 0:rw]
    k = z[:, rw:2 * rw]
    v = z[:, 2 * rw:3 * rw]
    lo = z[:, 3 * rw:]
    lane = lax.broadcasted_iota(jnp.int32, lo.shape, 1)
    feat = jnp.where(lane < d_decay, jnp.tanh(lo),
                     jnp.where(lane < d_decay + d_aaa, lo, _sigmoid(lo)))
    lora = _bdot(feat, w_lora)
    w = -_softplus(-(w0 + lora[:, 0:rw])) - 0.5
    log_decay = -jnp.exp(w)
    a_sig = _sigmoid(a0 + lora[:, rw:2 * rw])
    gate = lora[:, 2 * rw:3 * rw]
    if vmix is not None:
        v_first, v0, v1, v2 = vmix
        vm = _sigmoid(v0 + _bdot(_bdot(v, v1), v2))
        v = v + (v_first - v) * vm
    return r, k, v, a_sig, log_decay, gate


def _group_norm_bonus(y, r, kmod, v, rk, lnw, lnb, n):
    mean = jnp.mean(y, axis=-1, keepdims=True)
    yc = y - mean
    var = jnp.mean(yc * yc, axis=-1, keepdims=True)
    y = yc * lax.rsqrt(var + n * GROUPNORM_EPS_PER_CHANNEL) * lnw + lnb
    return y + jnp.sum(r * kmod * rk, axis=-1, keepdims=True) * v


def _key_features(k, a_sig, kk_w, ka_w):
    kk = k * kk_w
    nrm = jnp.sqrt(jnp.sum(kk * kk, axis=-1, keepdims=True))
    kk = kk / jnp.maximum(nrm, KK_NORM_FLOOR)
    kmod = k * (1.0 + (a_sig - 1.0) * ka_w)
    return kk, kmod


def _softmax_rows(s):
    m = jnp.max(s, axis=-1, keepdims=True)
    e = jnp.exp(s - m)
    return e / jnp.sum(e, axis=-1, keepdims=True)


def _mix_in_seq_kernel(has_vmix, lora_dims, *refs):
    it = iter(refs)
    x_ref, gmix_ref, win_ref, sprev_ref, cprev_ref, mu_ref, w0_ref, a0_ref, wl_ref, cw_ref = (
        next(it) for _ in range(10))
    if has_vmix:
        vf_ref, v0_ref, v1_ref, v2_ref = (next(it) for _ in range(4))
    r_o, k_o, v_o, a_o, ld_o, g_o, yb_o, xn_o, cnew_o, pc_scr, uc_scr = (next(it) for _ in range(11))

    t = pl.program_id(1)
    tm = x_ref.shape[0]
    rc = mu_ref.shape[1]
    cw = cw_ref.shape[1]
    rw = w0_ref.shape[1]

    @pl.when(t == 0)
    def _():
        sprev = jnp.broadcast_to(sprev_ref[...], (8, sprev_ref.shape[1]))
        pc_scr[...] = _bdot(sprev, win_ref[:, 0:rc])
        uc_scr[0:2, :] = cprev_ref[...]

    xn = _rms(x_ref[...], gmix_ref[...])
    proj = _bdot(xn, win_ref[...])
    p_cur = proj[:, 0:rc]
    p_shift = _shift_rows(p_cur, [pc_scr[0:1, :]])
    vmix = None
    if has_vmix:
        vmix = (vf_ref[...], v0_ref[...], v1_ref[...], v2_ref[...])
    r, k, v, a_sig, log_decay, gate = _mix_rows(
        p_cur, p_shift, mu_ref[...], w0_ref[...], a0_ref[...], wl_ref[...], lora_dims, rw, vmix)
    r_o[...] = r
    k_o[...] = k
    v_o[...] = v
    a_o[...] = a_sig
    ld_o[...] = log_decay
    g_o[...] = gate

    gate_b = proj[:, rc:rc + cw]
    gate_c = proj[:, rc + cw:rc + 2 * cw]
    h_in = proj[:, rc + 2 * cw:rc + 3 * cw]
    u = gate_c * h_in
    u1 = _shift_rows(u, [uc_scr[1:2, :]])
    u2 = _shift_rows(u, [uc_scr[0:1, :], uc_scr[1:2, :]])
    conv = cw_ref[0:1, :] * u2 + cw_ref[1:2, :] * u1 + cw_ref[2:3, :] * u
    yb_o[...] = gate_b * conv

    pc_scr[0:1, :] = p_cur[tm - 1:tm, :]
    uc_scr[0:2, :] = u[tm - 2:tm, :]

    @pl.when(t == pl.num_programs(1) - 1)
    def _():
        xn_o[...] = xn[tm - 1:tm, :]
        cnew_o[...] = u[tm - 2:tm, :]


def _split3(x):
    hi = x.astype(BF16)
    r1 = x - hi.astype(F32)
    mid = r1.astype(BF16)
    lo = (r1 - mid.astype(F32)).astype(BF16)
    return hi, mid, lo


def _dot3(a, b, dims=(((1,), (0,)), ((), ()))):
    a_hi = a.astype(BF16)
    a_lo = (a - a_hi.astype(F32)).astype(BF16)
    b_hi = b.astype(BF16)
    b_lo = (b - b_hi.astype(F32)).astype(BF16)
    d = functools.partial(lax.dot_general, dimension_numbers=dims, preferred_element_type=F32)
    return d(a_hi, b_hi) + d(a_lo, b_hi) + d(a_hi, b_lo)


def _dot1(a, b, dims=(((1,), (0,)), ((), ()))):
    return lax.dot_general(a.astype(BF16), b.astype(BF16), dims, preferred_element_type=F32)


_NT = (((1,), (1,)), ((), ()))
_TN = (((0,), (0,)), ((), ()))
_WKV_DOT = _dot1


def _wkv_seq_kernel(hs, r_ref, k_ref, v_ref, a_ref, ld_ref, kkw_ref, kaw_ref, rkw_ref, lnw_ref, lnb_ref,
                    s0_ref, y_o, s_o, st_scr):
    t = pl.program_id(1)
    tt, rw = r_ref.shape
    c = CHUNK
    pack = WKV_HEAD_PACK
    pw_lanes = pack * hs
    n_pairs = rw // pw_lanes
    assert c == hs, "chunk length equals the head size in the packed layout"
    shift = hs.bit_length() - 1
    assert 1 << shift == hs

    lane_head = lax.broadcasted_iota(jnp.int32, (1, pw_lanes), 1) >> shift
    row_p = lax.broadcasted_iota(jnp.int32, (pw_lanes, pw_lanes), 0)
    col_p = lax.broadcasted_iota(jnp.int32, (pw_lanes, pw_lanes), 1)
    same_head = (row_p >> shift) == (col_p >> shift)
    head_ones = jnp.where(same_head, 1.0, 0.0).astype(BF16)
    eye_p = (row_p == col_p).astype(F32)
    row_c = lax.broadcasted_iota(jnp.int32, (c, pw_lanes), 0)
    col_c = lax.broadcasted_iota(jnp.int32, (c, pw_lanes), 1) & (hs - 1)
    tril_strict = row_c > col_c
    tril_incl = row_c >= col_c
    eye_c = (row_c == col_c).astype(F32)
    row_t = lax.broadcasted_iota(jnp.int32, (c, 3 * c), 0)
    col_t = lax.broadcasted_iota(jnp.int32, (c, 3 * c), 1)
    col_t = jnp.where(col_t < c, col_t, jnp.where(col_t < 2 * c, col_t - c, col_t - 2 * c))
    tril3 = jnp.where(row_t >= col_t, 1.0, 0.0).astype(BF16)
    ones2 = jnp.concatenate([head_ones] * 2, axis=0)

    def blockdiag(x):
        return jnp.concatenate([jnp.where(lane_head == i, x, 0.0) for i in range(pack)], axis=0)

    def head_sums(x):
        hi = x.astype(BF16)
        lo = (x - hi.astype(F32)).astype(BF16)
        return jnp.dot(jnp.concatenate([hi, lo], axis=1), ones2, preferred_element_type=F32)

    @pl.when(t == 0)
    def _():
        z = jnp.zeros((hs, hs), F32)
        for p in range(n_pairs):
            blk = jnp.concatenate(
                [jnp.concatenate([s0_ref[pack * p + i] if j == i else z for j in range(pack)], axis=1)
                 for i in range(pack)], axis=0)
            st_scr[p] = blk.T

    n_chunks = tt // c
    cat = jnp.concatenate
    dot = _WKV_DOT

    def each(f, *lists):
        return [f(*xs) for xs in zip(*lists)]

    def cumsum_rows(x):
        hi, mid, lo = _split3(x)
        return jnp.dot(tril3, cat([hi, mid, lo], axis=0), preferred_element_type=F32)

    def prepare(chunks, out):
        items = [(ci, p) for ci in chunks for p in range(n_pairs)]
        tiles = lambda ref: [ref[ci * c:(ci + 1) * c, p * pw_lanes:(p + 1) * pw_lanes] for ci, p in items]
        params = lambda ref: [ref[:, p * pw_lanes:(p + 1) * pw_lanes] for _, p in items]
        r, k, v, a_sig, ld = (tiles(ref) for ref in (r_ref, k_ref, v_ref, a_ref, ld_ref))
        kkw, kaw, rkw = (params(ref) for ref in (kkw_ref, kaw_ref, rkw_ref))
        kk = each(lambda k_, w_: k_ * w_, k, kkw)
        kmod = each(lambda k_, a_, w_: k_ * (1.0 + (a_ - 1.0) * w_), k, a_sig, kaw)
        sums = each(lambda kk_, r_, km_, w_: head_sums(cat([kk_ * kk_, r_ * km_ * w_], axis=0)), kk, r, kmod, rkw)
        yield
        kk = each(lambda kk_, s_: kk_ / jnp.maximum(jnp.sqrt(s_[0:c]), KK_NORM_FLOOR), kk, sums)
        b = each(lambda kk_, a_: kk_ * a_, kk, a_sig)
        cum = each(cumsum_rows, ld)
        yield
        at = each(lambda kk_, cu, l_: -kk_ * jnp.exp(cu - l_), kk, cum, ld)
        rt = each(lambda r_, cu: r_ * jnp.exp(cu), r, cum)
        inv = each(lambda cu: jnp.exp(-cu), cum)
        bt = each(lambda b_, i_: b_ * i_, b, inv)
        kt = each(lambda km_, i_: km_ * i_, kmod, inv)
        yield
        to_end = each(lambda cu: jnp.exp(cu[c - 1:c, :] - cu), cum)
        bk_end = each(lambda b_, km_, te: cat([b_ * te, km_ * te], axis=0), b, kmod, to_end)
        p_end_col = each(lambda cu: jnp.sum(eye_p * jnp.exp(cu[c - 1:c, :]), axis=-1, keepdims=True), cum)
        gm = each(lambda at_, rt_, bt_, kt_: dot(cat([at_, rt_], axis=0),
                                                 cat([blockdiag(bt_), blockdiag(kt_)], axis=0), _NT),
                  at, rt, bt, kt)
        yield
        pc = pack * c
        l_ab = [jnp.where(tril_strict, g[0:c, 0:pc], 0.0) for g in gm]
        l_ak = [jnp.where(tril_strict, g[0:c, pc:2 * pc], 0.0) for g in gm]
        m_rb = [jnp.where(tril_incl, g[c:2 * c, 0:pc], 0.0) for g in gm]
        m_rk = [jnp.where(tril_incl, g[c:2 * c, pc:2 * pc], 0.0) for g in gm]
        t_inv = [eye_c + l_ for l_ in l_ab]
        pw = l_ab
        span = 2
        while span < c:
            pw = each(lambda q: dot(q, blockdiag(q)), pw)
            yield
            t_inv = each(lambda t_, q: t_ + dot(q, blockdiag(t_)), t_inv, pw)
            yield
            span *= 2
        out.update(items=items, r=r, v=v, kmod=kmod, bonus=[s_[c:2 * c] for s_ in sums], at=at, rt=rt,
                   l_ak=l_ak, m_rb=m_rb, m_rk=m_rk, t_inv=t_inv, bk_end=bk_end, p_end_col=p_end_col)

    def state_pass(d):
        items = d["items"]
        y = []
        for i0 in range(0, len(items), n_pairs):
            idx = range(i0, i0 + n_pairs)
            st = [st_scr[p] for p in range(n_pairs)]
            vb = [blockdiag(d["v"][i]) for i in idx]
            x = [dot(cat([d["at"][i], d["l_ak"][i]], axis=1), cat([st[p], vb[p]], axis=0))
                 for p, i in enumerate(idx)]
            yield
            u = [dot(d["t_inv"][i], blockdiag(x[p])) for p, i in enumerate(idx)]
            yield
            y += [dot(cat([d["rt"][i], d["m_rk"][i], d["m_rb"][i]], axis=1),
                      cat([st[p], vb[p], blockdiag(u[p])], axis=0)) for p, i in enumerate(idx)]
            upd = [dot(d["bk_end"][i], cat([u[p], d["v"][i]], axis=0), _TN) for p, i in enumerate(idx)]
            for p, i in enumerate(idx):
                st_scr[p] = d["p_end_col"][i] * st[p] + jnp.where(same_head, upd[p], 0.0)
            yield
        s2 = each(lambda y_: head_sums(cat([y_, y_ * y_], axis=0)), y)
        yield
        for (ci, p), y_, s_, bo_, v_ in zip(items, y, s2, d["bonus"], d["v"]):
            ln = slice(p * pw_lanes, (p + 1) * pw_lanes)
            mean = s_[0:c] * (1.0 / hs)
            var = s_[c:2 * c] * (1.0 / hs) - mean * mean
            yn = (y_ - mean) * lax.rsqrt(var + hs * GROUPNORM_EPS_PER_CHANNEL) * lnw_ref[:, ln] + lnb_ref[:, ln]
            y_o[ci * c:(ci + 1) * c, ln] = yn + bo_ * v_

    def run_together(*progs):
        progs = list(progs)
        while progs:
            for prog in list(progs):
                if next(prog, StopIteration) is StopIteration:
                    progs.remove(prog)

    halves = [range(h, min(h + WKV_BATCH_CHUNKS, n_chunks)) for h in range(0, n_chunks, WKV_BATCH_CHUNKS)]
    prev = None
    for chunks in halves:
        cur = {}
        if prev is None:
            run_together(prepare(chunks, cur))
        else:
            run_together(prepare(chunks, cur), state_pass(prev))
        prev = cur
    run_together(state_pass(prev))

    @pl.when(t == pl.num_programs(1) - 1)
    def _():
        for p in range(n_pairs):
            s_grp = st_scr[p].T
            for i in range(pack):
                s_o[pack * p + i] = s_grp[i * hs:(i + 1) * hs, i * hs:(i + 1) * hs]


def _mix_out_attn_seq_kernel(n_xheads, x_ref, y_ref, g_ref, yb_ref, wout_ref, gx_ref, wq_ref,
                             mk_ref, mv_ref, wo_ref, x2_o):
    mixed = jnp.concatenate([y_ref[...] * g_ref[...], yb_ref[...]], axis=-1)
    x1 = x_ref[...] + _bdot(mixed, wout_ref[...])
    q = _bdot(_rms(x1, gx_ref[...]), wq_ref[...])
    d = q.shape[1]
    xd = d // n_xheads
    scale = xd ** -0.5
    outs = []
    for h in range(n_xheads):
        sl = slice(h * xd, (h + 1) * xd)
        s = lax.dot_general(q[:, sl].astype(BF16), mk_ref[:, sl].astype(BF16),
                            (((1,), (1,)), ((), ())), preferred_element_type=F32) * scale
        outs.append(_bdot(_softmax_rows(s), mv_ref[:, sl]))
    o = jnp.concatenate(outs, axis=-1)
    x2_o[...] = x1 + _bdot(o, wo_ref[...])


def _ffn_seq_kernel(final, *refs):
    it = iter(refs)
    x_ref, gf_ref, wup_ref, fprev_ref, fcw_ref, wdown_ref = (next(it) for _ in range(6))
    gfin_ref = next(it) if final else None
    y_o, fnew_o, carry = (next(it) for _ in range(3))

    t = pl.program_id(1)
    tm = x_ref.shape[0]
    f = wdown_ref.shape[0]
    fc = FFN_COL_TILE

    @pl.when(t == 0)
    def _():
        carry[0:2, :] = fprev_ref[...]

    x = x_ref[...]
    xf = _rms(x, gf_ref[...]).astype(BF16)
    acc = x

    def up_cols(c0):
        return jnp.dot(xf, wup_ref[:, c0:c0 + fc], preferred_element_type=F32)

    def conv_cols(up, c0):
        p0 = carry[0:1, c0:c0 + fc]
        p1 = carry[1:2, c0:c0 + fc]
        up1 = _shift_rows(up, [p1])
        up2 = _shift_rows(up, [p0, p1])
        carry[0:2, c0:c0 + fc] = up[tm - 2:tm, :]
        return (fcw_ref[0:1, c0:c0 + fc] * up2 + fcw_ref[1:2, c0:c0 + fc] * up1
                + fcw_ref[2:3, c0:c0 + fc] * up)

    n_tiles = f // fc
    ahead = [(up_cols(i * fc), up_cols(f + i * fc)) for i in range(min(FFN_LOOKAHEAD, n_tiles))]
    for j in range(n_tiles):
        c0 = j * fc
        cur = ahead.pop(0)
        if j + FFN_LOOKAHEAD < n_tiles:
            ahead.append((up_cols(c0 + FFN_LOOKAHEAD * fc), up_cols(f + c0 + FFN_LOOKAHEAD * fc)))
        u = conv_cols(cur[0], c0)
        gt = conv_cols(cur[1], f + c0)
        hidden = gt * _sigmoid(gt) * u
        acc = acc + _bdot(hidden, wdown_ref[c0:c0 + fc, :])
    if final:
        acc = _rms(acc, gfin_ref[...])
    y_o[...] = acc

    @pl.when(t == pl.num_programs(1) - 1)
    def _():
        fnew_o[...] = carry[0:2, :]


def _mem_kv_kernel(m_ref, g_ref, wk_ref, wv_ref, k_o, v_o):
    mn = _rms(m_ref[...], g_ref[...])
    k_o[...] = _bdot(mn, wk_ref[...])
    v_o[...] = _bdot(mn, wv_ref[...])


def _mix_in_step_kernel(has_vmix, lora_dims, *refs):
    it = iter(refs)
    x_ref, gmix_ref, win_ref, sprev_ref, cprev_ref, mu_ref, w0_ref, a0_ref, wl_ref, cw_ref = (
        next(it) for _ in range(10))
    if has_vmix:
        vf_ref, v0_ref, v1_ref, v2_ref = (next(it) for _ in range(4))
    r_o, k_o, v_o, a_o, ld_o, vrow_o, g_o, yb_o, xn_o, cnew_o = (next(it) for _ in range(10))

    nb = x_ref.shape[0]
    rc = mu_ref.shape[1]
    cw = cw_ref.shape[1]
    rw = w0_ref.shape[1]

    xn = _rms(x_ref[...], gmix_ref[...])
    xn_o[...] = xn
    stacked = jnp.concatenate([xn, sprev_ref[...]], axis=0)
    proj = _bdot(stacked, win_ref[...])
    p_cur = proj[0:nb, 0:rc]
    p_shift = proj[nb:2 * nb, 0:rc]
    vmix = None
    if has_vmix:
        vmix = (vf_ref[...], v0_ref[...], v1_ref[...], v2_ref[...])
    r, k, v, a_sig, log_decay, gate = _mix_rows(
        p_cur, p_shift, mu_ref[...], w0_ref[...], a0_ref[...], wl_ref[...], lora_dims, rw, vmix)
    r_o[...] = r.T
    k_o[...] = k.T
    v_o[...] = v.T
    a_o[...] = a_sig.T
    ld_o[...] = log_decay.T
    vrow_o[...] = v
    g_o[...] = gate

    gate_b = proj[0:nb, rc:rc + cw]
    gate_c = proj[0:nb, rc + cw:rc + 2 * cw]
    h_in = proj[0:nb, rc + 2 * cw:rc + 3 * cw]
    u = gate_c * h_in
    u2 = cprev_ref[:, 0, :]
    u1 = cprev_ref[:, 1, :]
    yb_o[...] = gate_b * (cw_ref[0:1, :] * u2 + cw_ref[1:2, :] * u1 + cw_ref[2:3, :] * u)
    cnew_o[:, 0, :] = u1
    cnew_o[:, 1, :] = u


def _wkv_step_kernel(aliased, *refs):
    r_ref, k_ref, v_ref, a_ref, ld_ref, kkw_ref, kaw_ref, rkw_ref, lnw_ref, lnb_ref, s_ref = refs[:11]
    y_o, s_o, y_scr = refs[12:] if aliased else refs[11:]
    n = s_ref.shape[0]
    r, k, v, a_sig, ld = r_ref[...], k_ref[...], v_ref[...], a_ref[...], ld_ref[...]
    kk = k * kkw_ref[...]
    kk = kk / jnp.maximum(jnp.sqrt(jnp.sum(kk * kk, axis=0, keepdims=True)), KK_NORM_FLOOR)
    kmod = k * (1.0 + (a_sig - 1.0) * kaw_ref[...])
    a = -kk
    b = kk * a_sig
    decay = jnp.exp(ld)
    for vi in range(n):
        s = s_ref[vi]
        sa = jnp.sum(s * a, axis=0, keepdims=True)
        s_new = s * decay + sa * b + v[vi:vi + 1, :] * kmod
        s_o[vi] = s_new
        y_scr[vi:vi + 1, :] = jnp.sum(s_new * r, axis=0, keepdims=True)
    y = y_scr[...]
    mean = jnp.mean(y, axis=0, keepdims=True)
    yc = y - mean
    var = jnp.mean(yc * yc, axis=0, keepdims=True)
    y = yc * lax.rsqrt(var + n * GROUPNORM_EPS_PER_CHANNEL) * lnw_ref[...] + lnb_ref[...]
    y_o[...] = y + jnp.sum(r * kmod * rkw_ref[...], axis=0, keepdims=True) * v


def _mix_out_q_step_kernel(x_ref, ya_ref, g_ref, yb_ref, wout_ref, gx_ref, wq_ref, x1_o, q_o):
    mixed = jnp.concatenate([ya_ref[...].T * g_ref[...], yb_ref[...]], axis=-1)
    x1 = x_ref[...] + _bdot(mixed, wout_ref[...])
    x1_o[...] = x1
    q_o[...] = _bdot(_rms(x1, gx_ref[...]), wq_ref[...])


def _attn_step_kernel(q_ref, mk_ref, mv_ref, o_o):
    bt = q_ref.shape[0]
    scale = q_ref.shape[-1] ** -0.5

    def body(i, carry):
        q = q_ref[i]
        s = jnp.sum(mk_ref[i] * q, axis=-1, keepdims=True) * scale
        e = jnp.exp(s - jnp.max(s, axis=0, keepdims=True))
        p = e / jnp.sum(e, axis=0, keepdims=True)
        o_o[i] = jnp.sum(p * mv_ref[i], axis=0)
        return carry

    lax.fori_loop(0, bt, body, 0)


def _ffn_step_kernel(final, aliased, *refs):
    it = iter(refs)
    x1_ref, o_ref, wo_ref, gf_ref, wup_ref, fprev_ref, fcw_ref, wdown_ref = (next(it) for _ in range(8))
    gfin_ref = next(it) if final else None
    if aliased:
        next(it)
    y_o, fnew_o = (next(it) for _ in range(2))
    f = wdown_ref.shape[0]
    f2 = 2 * f
    x2 = x1_ref[...] + _bdot(o_ref[...], wo_ref[...])
    up = _bdot(_rms(x2, gf_ref[...]), wup_ref[...])
    up2 = fprev_ref[:, 0, :]
    up1 = fprev_ref[:, 1, :]
    upc = fcw_ref[0:1, :] * up2 + fcw_ref[1:2, :] * up1 + fcw_ref[2:3, :] * up
    fnew_o[:, 0, :] = up1
    fnew_o[:, 1, :] = up
    u = upc[:, 0:f]
    gt = upc[:, f:f2]
    x3 = x2 + _bdot(gt * _sigmoid(gt) * u, wdown_ref[...])
    if final:
        x3 = _rms(x3, gfin_ref[...])
    y_o[...] = x3


def _params(*sem):
    return pltpu.CompilerParams(dimension_semantics=sem, vmem_limit_bytes=V7X_VMEM_LIMIT_BYTES)


class _Layered(NamedTuple):
    stacked: jax.Array
    layer: int

    @property
    def shape(self):
        return self.stacked.shape[1:]


def _operand(x):
    return x.stacked if isinstance(x, _Layered) else x


def _whole(x):
    if isinstance(x, _Layered):
        nd, l = x.stacked.ndim, x.layer
        return pl.BlockSpec((None,) + x.shape, lambda *_: (l,) + (0,) * (nd - 1), pipeline_mode=pl.Buffered(1))
    nd = x.ndim
    return pl.BlockSpec(x.shape, lambda *_: (0,) * nd, pipeline_mode=pl.Buffered(1))


def _sds(shape):
    return jax.ShapeDtypeStruct(shape, F32)


def _stacked_weights(p):
    depth = p["w_in"].shape[0]
    rows = lambda a: a.reshape(a.shape[0], 1, -1)
    rw = p["w0"].shape[1]
    d_decay, d_aaa, d_gate = p["w2"].shape[1], p["a2"].shape[1], p["g2"].shape[1]
    pad_cols = lambda a, before, after: jnp.pad(a, ((0, 0), (0, 0), (before, after)))
    w_lora = jnp.concatenate([pad_cols(p["w2"], 0, 2 * rw), pad_cols(p["a2"], rw, rw), pad_cols(p["g2"], 2 * rw, 0)],
                             axis=1)
    lanes = 128
    mv = p["v1"].shape[2]
    mvp = -(-mv // lanes) * lanes
    w = dict(
        norm_mix=rows(p["norm_mix"]), w_in=p["w_in"].astype(BF16), mu=rows(p["mu_shift"]), w0=rows(p["w0"]),
        a0=rows(p["a0"]), w_lora=w_lora.astype(BF16), conv_w=p["conv_w"], k_k=p["k_k"], k_a=p["k_a"],
        r_k=p["r_k"], ln_w=p["ln_x_w"], ln_b=p["ln_x_b"], w_out=p["w_out"].astype(BF16), norm_x=rows(p["norm_x"]),
        wq=p["wq"].astype(BF16), wo=p["wo"].astype(BF16), norm_ffn=rows(p["norm_ffn"]),
        w_up=p["w_up"].astype(BF16), ffn_conv_w=p["ffn_conv_w"], w_down=p["w_down"].astype(BF16),
        norm_mem=rows(p["norm_mem"]), wk=p["wk"].astype(BF16), wv=p["wv"].astype(BF16),
    )
    layers = []
    for l in range(depth):
        wl = {k_: _Layered(v_, l) for k_, v_ in w.items()}
        wl["norm_final"] = p["norm_final"].reshape(1, -1)
        wl["lora_dims"] = (d_decay, d_aaa, d_gate)
        if l > 0:
            wl["v0"] = _Layered(rows(p["v0"]), l - 1)
            wl["v1"] = _Layered(jnp.pad(p["v1"], ((0, 0), (0, 0), (0, mvp - mv))).astype(BF16), l - 1)
            wl["v2"] = _Layered(jnp.pad(p["v2"], ((0, 0), (0, mvp - mv), (0, 0))).astype(BF16), l - 1)
        layers.append(wl)
    return layers


def _head_rows(x, n_heads):
    st = x.stacked
    return _Layered(st.reshape(st.shape[0], n_heads, st.shape[1] // n_heads, 1), x.layer)


def _mem_kv(mem, w):
    rows, d = mem.shape
    tm = min(512, rows)
    args = [mem, w["norm_mem"], w["wk"], w["wv"]]
    return pl.pallas_call(
        _mem_kv_kernel,
        grid=(rows // tm,),
        in_specs=[pl.BlockSpec((tm, d), lambda i: (i, 0))] + [_whole(a) for a in args[1:]],
        out_specs=[pl.BlockSpec((tm, d), lambda i: (i, 0))] * 2,
        out_shape=[_sds((rows, d))] * 2,
        compiler_params=_params("parallel"),
        name="mem_kv",
    )(*map(_operand, args))


def _prompt_layer(l, x, mem_k, mem_v, shift0, wkv0, conv0, ffn0, v_first, w, n_heads, n_xheads, final):
    mem_k, mem_v, shift0, wkv0, conv0, ffn0 = (a[l] for a in (mem_k, mem_v, shift0, wkv0, conv0, ffn0))
    bsz, t, d = x.shape
    rw = w["w0"].shape[1]
    hs = rw // n_heads
    cw = w["conv_w"].shape[1]
    rc = w["mu"].shape[1]
    tm = min(ROW_TILE, t)
    nt = t // tm
    has_vmix = v_first is not None

    tile = lambda n: pl.BlockSpec((None, tm, n), lambda b, i: (b, i, 0))
    per_b = lambda s: pl.BlockSpec((None,) + s, lambda b, i: (b,) + (0,) * len(s))

    ins = [x, w["norm_mix"], w["w_in"], shift0[:, None, :], conv0, w["mu"], w["w0"], w["a0"], w["w_lora"],
           w["conv_w"]]
    specs = [tile(d), _whole(w["norm_mix"]), _whole(w["w_in"]), per_b((1, d)), per_b((2, cw)),
             _whole(w["mu"]), _whole(w["w0"]), _whole(w["a0"]), _whole(w["w_lora"]), _whole(w["conv_w"])]
    if has_vmix:
        ins += [v_first, w["v0"], w["v1"], w["v2"]]
        specs += [tile(rw), _whole(w["v0"]), _whole(w["v1"]), _whole(w["v2"])]
    out_shape = [_sds((bsz, t, rw))] * 6 + [_sds((bsz, t, cw)), _sds((bsz, 1, d)), _sds((bsz, 2, cw))]
    out_specs = [tile(rw)] * 6 + [tile(cw), per_b((1, d)), per_b((2, cw))]
    r, k, v, a_sig, ld, gate, y_b, xn_last, conv_new = pl.pallas_call(
        functools.partial(_mix_in_seq_kernel, has_vmix, w["lora_dims"]),
        grid=(bsz, nt), in_specs=specs, out_specs=out_specs, out_shape=out_shape,
        scratch_shapes=[pltpu.VMEM((8, rc), F32), pltpu.VMEM((8, cw), F32)],
        compiler_params=_params("parallel", "arbitrary"),
        name="mix_in_seq",
    )(*map(_operand, ins))
    if not has_vmix:
        v_first = v

    head_params = [_Layered(a.stacked[:, None, :], a.layer) for a in
                   (w["k_k"], w["k_a"], w["r_k"], w["ln_w"], w["ln_b"])]
    sblk = per_b((n_heads, hs, hs))
    wt = min(WKV_ROW_TILE, t)
    wtile = pl.BlockSpec((None, wt, rw), lambda b, i: (b, i, 0))
    y_a, wkv_new = pl.pallas_call(
        functools.partial(_wkv_seq_kernel, hs),
        grid=(bsz, t // wt),
        in_specs=[wtile] * 5 + [_whole(a) for a in head_params] + [sblk],
        out_specs=[wtile, sblk],
        out_shape=[_sds((bsz, t, rw)), _sds((bsz, n_heads, hs, hs))],
        scratch_shapes=[pltpu.VMEM((rw // (WKV_HEAD_PACK * hs), WKV_HEAD_PACK * hs, WKV_HEAD_PACK * hs), F32)],
        compiler_params=_params("parallel", "arbitrary"),
        name="wkv_seq",
    )(r, k, v, a_sig, ld, *map(_operand, head_params), wkv0)

    m = mem_k.shape[1]
    x2 = pl.pallas_call(
        functools.partial(_mix_out_attn_seq_kernel, n_xheads),
        grid=(bsz, nt),
        in_specs=[tile(d), tile(rw), tile(rw), tile(cw), _whole(w["w_out"]), _whole(w["norm_x"]), _whole(w["wq"]),
                  per_b((m, d)), per_b((m, d)), _whole(w["wo"])],
        out_specs=tile(d), out_shape=_sds((bsz, t, d)),
        compiler_params=_params("parallel", "arbitrary"),
        name="mix_out_attn_seq",
    )(*map(_operand, (x, y_a, gate, y_b, w["w_out"], w["norm_x"], w["wq"], mem_k, mem_v, w["wo"])))

    f2 = w["w_up"].shape[1]
    ins = [x2, w["norm_ffn"], w["w_up"], ffn0, w["ffn_conv_w"], w["w_down"]]
    ft = min(FFN_ROW_TILE, t)
    ftile = pl.BlockSpec((None, ft, d), lambda b, i: (b, i, 0))
    specs = [ftile, _whole(w["norm_ffn"]), _whole(w["w_up"]), per_b((2, f2)), _whole(w["ffn_conv_w"]),
             _whole(w["w_down"])]
    if final:
        ins.append(w["norm_final"])
        specs.append(_whole(w["norm_final"]))
    x3, ffn_new = pl.pallas_call(
        functools.partial(_ffn_seq_kernel, final),
        grid=(bsz, t // ft), in_specs=specs,
        out_specs=[ftile, per_b((2, f2))], out_shape=[_sds((bsz, t, d)), _sds((bsz, 2, f2))],
        scratch_shapes=[pltpu.VMEM((8, f2), F32)],
        compiler_params=_params("parallel", "arbitrary"),
        name="ffn_seq",
    )(*map(_operand, ins))
    return x3, xn_last[:, 0, :], wkv_new, conv_new, ffn_new, v_first


def _sample_layer(l, x, mem_k, mem_v, shift0, wkv_t, conv0, ffn0, carried, w, n_heads, n_xheads, final):
    v_first, wkv_all, ffn_all = carried
    nb, d = x.shape
    rw = w["w0"].shape[1]
    hs = rw // n_heads
    cw = w["conv_w"].shape[1]
    has_vmix = v_first is not None
    one = lambda a: pl.BlockSpec(a.shape, lambda i: (0,) * a.ndim, pipeline_mode=pl.Buffered(1))
    layer_of = lambda a: pl.BlockSpec((None,) + a.shape[1:], lambda i: (l,) + (0,) * (a.ndim - 1),
                                      pipeline_mode=pl.Buffered(1))

    ins = [x, w["norm_mix"], w["w_in"], shift0, conv0, w["mu"], w["w0"], w["a0"], w["w_lora"], w["conv_w"]]
    specs = [one(x), _whole(w["norm_mix"]), _whole(w["w_in"]), layer_of(shift0), layer_of(conv0)] + [
        _whole(a) for a in ins[5:]]
    if has_vmix:
        ins += [v_first, w["v0"], w["v1"], w["v2"]]
        specs += [one(v_first), _whole(w["v0"]), _whole(w["v1"]), _whole(w["v2"])]
    out_shape = [_sds((rw, nb))] * 5 + [_sds((nb, rw))] * 2 + [_sds((nb, cw)), _sds((nb, d)), _sds((nb, 2, cw))]
    r, k, v, a_sig, ld, v_rows, gate, y_b, xn, conv_new = pl.pallas_call(
        functools.partial(_mix_in_step_kernel, has_vmix, w["lora_dims"]),
        grid=(1,), in_specs=specs, out_specs=[one(s_) for s_ in out_shape], out_shape=out_shape,
        compiler_params=_params("arbitrary"),
        name="mix_in_step",
    )(*map(_operand, ins))
    if not has_vmix:
        v_first = v_rows

    heads = lambda a: a.reshape(n_heads, hs, nb)
    hblk = pl.BlockSpec((None, hs, nb), lambda h: (h, 0, 0))
    head_params = [_head_rows(w[n_], n_heads) for n_ in ("k_k", "k_a", "r_k", "ln_w", "ln_b")]
    pblk = pl.BlockSpec((None, None, hs, 1), lambda h: (l, h, 0, 0))
    sblk = pl.BlockSpec((None, None, hs, hs, nb), lambda h: (l, h, 0, 0, 0))
    aliased = wkv_all is not None
    ins = [heads(r), heads(k), heads(v), heads(a_sig), heads(ld)] + [a.stacked for a in head_params] + [wkv_t]
    specs = [hblk] * 5 + [pblk] * 5 + [sblk]
    if aliased:
        ins.append(wkv_all)
        specs.append(pl.BlockSpec(memory_space=pl.ANY))
    y_a, wkv_all = pl.pallas_call(
        functools.partial(_wkv_step_kernel, aliased),
        grid=(n_heads,),
        in_specs=specs, out_specs=[hblk, sblk],
        out_shape=[_sds((n_heads, hs, nb)), _sds(wkv_t.shape)],
        scratch_shapes=[pltpu.VMEM((hs, nb), F32)],
        input_output_aliases={len(ins) - 1: 1} if aliased else {},
        compiler_params=_params("arbitrary"),
        name="wkv_step",
    )(*ins)

    ins = [x, y_a.reshape(rw, nb), gate, y_b, w["w_out"], w["norm_x"], w["wq"]]
    x1, q = pl.pallas_call(
        _mix_out_q_step_kernel,
        grid=(1,), in_specs=[one(a) for a in ins[:4]] + [_whole(a) for a in ins[4:]],
        out_specs=[one(x)] * 2, out_shape=[_sds((nb, d))] * 2,
        compiler_params=_params("arbitrary"),
        name="mix_out_q_step",
    )(*map(_operand, ins))

    m, xd = mem_k.shape[2], mem_k.shape[4]
    ab = min(ATTN_BATCH_TILE, nb)
    qblk = pl.BlockSpec((ab, n_xheads, xd), lambda i: (i, 0, 0))
    mblk = pl.BlockSpec((None, ab, m, n_xheads, xd), lambda i: (l, i, 0, 0, 0))
    o = pl.pallas_call(
        _attn_step_kernel,
        grid=(nb // ab,),
        in_specs=[qblk, mblk, mblk], out_specs=qblk, out_shape=_sds((nb, n_xheads, xd)),
        compiler_params=_params("parallel"),
        name="attn_step",
    )(q.reshape(nb, n_xheads, xd), mem_k, mem_v)
    o = o.reshape(nb, d)

    ins = [x1, o, w["wo"], w["norm_ffn"], w["w_up"], ffn0, w["ffn_conv_w"], w["w_down"]]
    fblk = pl.BlockSpec((None,) + ffn0.shape[1:], lambda i: (l, 0, 0, 0))
    specs = [one(x1), one(o), _whole(w["wo"]), _whole(w["norm_ffn"]), _whole(w["w_up"]), fblk,
             _whole(w["ffn_conv_w"]), _whole(w["w_down"])]
    if final:
        ins.append(w["norm_final"])
        specs.append(_whole(w["norm_final"]))
    ffn_aliased = ffn_all is not None
    if ffn_aliased:
        ins.append(ffn_all)
        specs.append(pl.BlockSpec(memory_space=pl.ANY))
    x3, ffn_all = pl.pallas_call(
        functools.partial(_ffn_step_kernel, final, ffn_aliased),
        grid=(1,), in_specs=specs, out_specs=[one(x), fblk],
        out_shape=[_sds((nb, d)), _sds(ffn0.shape)],
        input_output_aliases={len(ins) - 1: 1} if ffn_aliased else {},
        compiler_params=_params("arbitrary"),
        name="ffn_step",
    )(*map(_operand, ins))
    return x3, xn, conv_new, (v_first, wkv_all, ffn_all)


def kernel(x_prompt, x_sample, mem_prompt, state_shift, state_wkv, state_conv, state_ffn, cache_mem_k, cache_mem_v, norm_mix, w_in, mu_shift, w0, w2, a0, a2, g2, v0, v1, v2, k_k, k_a, r_k, ln_x_w, ln_x_b, conv_w, w_out, norm_x, norm_mem, wq, wk, wv, wo, norm_ffn, w_up, ffn_conv_w, w_down, norm_final):
    p = dict(norm_mix=norm_mix, w_in=w_in, mu_shift=mu_shift, w0=w0, w2=w2, a0=a0, a2=a2, g2=g2, v0=v0, v1=v1,
             v2=v2, k_k=k_k, k_a=k_a, r_k=r_k, ln_x_w=ln_x_w, ln_x_b=ln_x_b, conv_w=conv_w, w_out=w_out,
             norm_x=norm_x, norm_mem=norm_mem, wq=wq, wk=wk, wv=wv, wo=wo, norm_ffn=norm_ffn, w_up=w_up,
             ffn_conv_w=ffn_conv_w, w_down=w_down, norm_final=norm_final)
    depth = w_in.shape[0]
    n_heads = state_wkv.shape[2]
    hs = state_wkv.shape[3]
    n_xheads, xd = cache_mem_k.shape[3], cache_mem_k.shape[4]
    assert x_sample.shape[1] == 1, "the sample group advances one token per sequence"
    weights = _stacked_weights(p)

    bp, tp, d = x_prompt.shape
    m = mem_prompt.shape[1]
    mks, mvs = [], []
    for l in range(depth):
        mk, mv = _mem_kv(mem_prompt.reshape(bp * m, d), weights[l])
        mks.append(mk.reshape(bp, m, d))
        mvs.append(mv.reshape(bp, m, d))
    cw = conv_w.shape[2]
    f2 = w_up.shape[2]
    zeros = lambda *s_: jnp.zeros((depth, bp) + s_, F32)
    x, v_first = x_prompt, None
    shs, wks, cvs, ffs = [], [], [], []
    for l in range(depth):
        x, sh, s_, cs, fs, v_first = _prompt_layer(
            l, x, mks, mvs, zeros(d), zeros(n_heads, hs, hs), zeros(state_conv.shape[2], cw),
            zeros(state_ffn.shape[2], f2), v_first, weights[l], n_heads, n_xheads, l == depth - 1)
        shs.append(sh)
        wks.append(s_)
        cvs.append(cs)
        ffs.append(fs)
    y_p = x
    mem_k_p = jnp.stack(mks).reshape(depth, bp, m, n_xheads, xd)
    mem_v_p = jnp.stack(mvs).reshape(depth, bp, m, n_xheads, xd)

    nb = x_sample.shape[0]
    wkv_t = jnp.transpose(state_wkv, (0, 2, 3, 4, 1))
    x, carried = x_sample.reshape(nb, d), (None, None, None)
    xns, cns = [], []
    for l in range(depth):
        x, xn, cn, carried = _sample_layer(l, x, cache_mem_k, cache_mem_v, state_shift, wkv_t, state_conv,
                                           state_ffn, carried, weights[l], n_heads, n_xheads, l == depth - 1)
        xns.append(xn)
        cns.append(cn)
    _, wkv_all, ffn_s = carried
    wkv_s = jnp.transpose(wkv_all, (0, 4, 1, 2, 3))
    return (y_p, x.reshape(x_sample.shape), jnp.stack(shs), jnp.stack(wks), jnp.stack(cvs), jnp.stack(ffs),
            mem_k_p, mem_v_p, jnp.stack(xns), wkv_s, jnp.stack(cns), ffn_s)
```

```python
import functools
from typing import NamedTuple

import jax
import jax.numpy as jnp
from jax import lax
from jax.experimental import pallas as pl
from jax.experimental.pallas import tpu as pltpu

F32 = jnp.float32
BF16 = jnp.bfloat16
HIGHEST = lax.Precision.HIGHEST

RMS_EPS = 1e-6
GROUPNORM_EPS_PER_CHANNEL = 1e-5
KK_NORM_FLOOR = 1e-12

ROW_TILE = 512
WKV_ROW_TILE = 1024
FFN_ROW_TILE = 256
MIX_SUB_ROWS = 256
CHUNK = 64
WKV_HEAD_PACK = 2
WKV_BATCH_CHUNKS = 4
ATTN_BATCH_TILE = 2
WKV_BATCH_TILE = 8
FFN_COL_TILE = 256
FFN_LOOKAHEAD = 3
V7X_VMEM_LIMIT_BYTES = 56 * 1024 * 1024
SUBLANES = 8


def _rms(x, g):
    ms = jnp.mean(x * x, axis=-1, keepdims=True)
    return x * lax.rsqrt(ms + RMS_EPS) * g


def _bdot(a, b):
    return jnp.dot(a.astype(BF16), b.astype(BF16), preferred_element_type=F32)


def _hdot(a, b, dims=(((1,), (0,)), ((), ()))):
    return lax.dot_general(a, b, dims, precision=HIGHEST, preferred_element_type=F32)


def _sigmoid(x):
    return 1.0 / (1.0 + jnp.exp(-x))


def _softplus(x):
    return jnp.maximum(x, 0.0) + jnp.log1p(jnp.exp(-jnp.abs(x)))


def _shift_rows(x, prev_rows):
    s = len(prev_rows)
    rolled = pltpu.roll(x, s, axis=0)
    top = rolled[0:SUBLANES, :]
    row = lax.broadcasted_iota(jnp.int32, top.shape, 0)
    for i, pr in enumerate(prev_rows):
        top = jnp.where(row == i, pr, top)
    return jnp.concatenate([top, rolled[SUBLANES:, :]], axis=0)


def _mix_rows_stages(p_cur, p_shift, mu, w0, a0, w_lora, lora_dims, rw, vmix, out):
    d_decay, d_aaa, _ = lora_dims
    z = p_cur + (p_shift - p_cur) * mu
    r = z[:, 0:rw]
    k = z[:, rw:2 * rw]
    v = z[:, 2 * rw:3 * rw]
    lo = z[:, 3 * rw:]
    lane = lax.broadcasted_iota(jnp.int32, lo.shape, 1)
    feat = jnp.where(lane < d_decay, jnp.tanh(lo),
                     jnp.where(lane < d_decay + d_aaa, lo, _sigmoid(lo)))
    yield
    lora = _bdot(feat, w_lora)
    yield
    w = -_softplus(-(w0 + lora[:, 0:rw])) - 0.5
    log_decay = -jnp.exp(w)
    a_sig = _sigmoid(a0 + lora[:, rw:2 * rw])
    gate = lora[:, 2 * rw:3 * rw]
    if vmix is not None:
        v_first, v0, v1, v2 = vmix
        t = _bdot(v, v1)
        yield
        pre = _bdot(t, v2)
        yield
        v = v + (v_first - v) * _sigmoid(v0 + pre)
    out.extend([r, k, v, a_sig, log_decay, gate])


def _mix_rows(*args):
    out = []
    for _ in _mix_rows_stages(*args, out):
        pass
    return out


def _run_together(*progs):
    progs = list(progs)
    while progs:
        for prog in list(progs):
            if next(prog, StopIteration) is StopIteration:
                progs.remove(prog)


def _group_norm_bonus(y, r, kmod, v, rk, lnw, lnb, n):
    mean = jnp.mean(y, axis=-1, keepdims=True)
    yc = y - mean
    var = jnp.mean(yc * yc, axis=-1, keepdims=True)
    y = yc * lax.rsqrt(var + n * GROUPNORM_EPS_PER_CHANNEL) * lnw + lnb
    return y + jnp.sum(r * kmod * rk, axis=-1, keepdims=True) * v


def _key_features(k, a_sig, kk_w, ka_w):
    kk = k * kk_w
    nrm = jnp.sqrt(jnp.sum(kk * kk, axis=-1, keepdims=True))
    kk = kk / jnp.maximum(nrm, KK_NORM_FLOOR)
    kmod = k * (1.0 + (a_sig - 1.0) * ka_w)
    return kk, kmod


def _softmax_rows(s):
    m = jnp.max(s, axis=-1, keepdims=True)
    e = jnp.exp(s - m)
    return e / jnp.sum(e, axis=-1, keepdims=True)


def _mix_in_seq_kernel(has_vmix, lora_dims, *refs):
    it = iter(refs)
    x_ref, gmix_ref, win_ref, sprev_ref, cprev_ref, mu_ref, w0_ref, a0_ref, wl_ref, cw_ref = (
        next(it) for _ in range(10))
    if has_vmix:
        vf_ref, v0_ref, v1_ref, v2_ref = (next(it) for _ in range(4))
    r_o, k_o, v_o, a_o, ld_o, g_o, yb_o, xn_o, cnew_o, pc_scr, uc_scr = (next(it) for _ in range(11))

    t = pl.program_id(1)
    tm = x_ref.shape[0]
    rc = mu_ref.shape[1]
    cw = cw_ref.shape[1]
    rw = w0_ref.shape[1]

    @pl.when(t == 0)
    def _():
        sprev = jnp.broadcast_to(sprev_ref[...], (8, sprev_ref.shape[1]))
        pc_scr[...] = _bdot(sprev, win_ref[:, 0:rc])
        uc_scr[0:2, :] = cprev_ref[...]

    n_sub = max(1, tm // MIX_SUB_ROWS)
    sub = tm // n_sub
    col_chunks = [(0, 2 * rw), (2 * rw, rc), (rc, rc + cw), (rc + cw, rc + 3 * cw)]
    xns = [_rms(x_ref[i * sub:(i + 1) * sub, :], gmix_ref[...]) for i in range(n_sub)]
    carry = dict(p=pc_scr[0:1, :], u=[uc_scr[0:1, :], uc_scr[1:2, :]])

    def project(i, out):
        xb = xns[i].astype(BF16)
        for c0, c1 in col_chunks:
            out.append(jnp.dot(xb, win_ref[:, c0:c1], preferred_element_type=F32))
            yield

    def features(i, parts):
        rows = slice(i * sub, (i + 1) * sub)
        p_cur = jnp.concatenate(parts[0:2], axis=1)
        p_shift = _shift_rows(p_cur, [carry["p"]])
        vmix = None
        if has_vmix:
            vmix = (vf_ref[rows, :], v0_ref[...], v1_ref[...], v2_ref[...])
        mixed = []
        yield from _mix_rows_stages(p_cur, p_shift, mu_ref[...], w0_ref[...], a0_ref[...], wl_ref[...],
                                    lora_dims, rw, vmix, mixed)
        for ref, val in zip((r_o, k_o, v_o, a_o, ld_o, g_o), mixed):
            ref[rows, :] = val
        yield
        gate_b = parts[2]
        u = parts[3][:, 0:cw] * parts[3][:, cw:2 * cw]
        u1 = _shift_rows(u, [carry["u"][1]])
        u2 = _shift_rows(u, carry["u"])
        yb_o[rows, :] = gate_b * (cw_ref[0:1, :] * u2 + cw_ref[1:2, :] * u1 + cw_ref[2:3, :] * u)
        carry["p"] = p_cur[sub - 1:sub, :]
        carry["u"] = [u[sub - 2:sub - 1, :], u[sub - 1:sub, :]]

    parts = []
    _run_together(project(0, parts))
    for i in range(n_sub):
        nxt = []
        if i + 1 < n_sub:
            _run_together(features(i, parts), project(i + 1, nxt))
        else:
            _run_together(features(i, parts))
        parts = nxt

    pc_scr[0:1, :] = carry["p"]
    uc_scr[0:1, :] = carry["u"][0]
    uc_scr[1:2, :] = carry["u"][1]

    @pl.when(t == pl.num_programs(1) - 1)
    def _():
        xn_o[...] = xns[-1][sub - 1:sub, :]
        cnew_o[0:1, :] = carry["u"][0]
        cnew_o[1:2, :] = carry["u"][1]


def _split3(x):
    hi = x.astype(BF16)
    r1 = x - hi.astype(F32)
    mid = r1.astype(BF16)
    lo = (r1 - mid.astype(F32)).astype(BF16)
    return hi, mid, lo


def _dot3(a, b, dims=(((1,), (0,)), ((), ()))):
    a_hi = a.astype(BF16)
    a_lo = (a - a_hi.astype(F32)).astype(BF16)
    b_hi = b.astype(BF16)
    b_lo = (b - b_hi.astype(F32)).astype(BF16)
    d = functools.partial(lax.dot_general, dimension_numbers=dims, preferred_element_type=F32)
    return d(a_hi, b_hi) + d(a_lo, b_hi) + d(a_hi, b_lo)


def _dot1(a, b, dims=(((1,), (0,)), ((), ()))):
    return lax.dot_general(a.astype(BF16), b.astype(BF16), dims, preferred_element_type=F32)


_NT = (((1,), (1,)), ((), ()))
_TN = (((0,), (0,)), ((), ()))
_WKV_DOT = _dot1


def _wkv_seq_kernel(hs, r_ref, k_ref, v_ref, a_ref, ld_ref, kkw_ref, kaw_ref, rkw_ref, lnw_ref, lnb_ref,
                    s0_ref, y_o, s_o, st_scr):
    t = pl.program_id(1)
    tt, rw = r_ref.shape
    c = CHUNK
    pack = WKV_HEAD_PACK
    pw_lanes = pack * hs
    n_pairs = rw // pw_lanes
    assert c == hs, "chunk length equals the head size in the packed layout"
    shift = hs.bit_length() - 1
    assert 1 << shift == hs

    lane_head = lax.broadcasted_iota(jnp.int32, (1, pw_lanes), 1) >> shift
    row_p = lax.broadcasted_iota(jnp.int32, (pw_lanes, pw_lanes), 0)
    col_p = lax.broadcasted_iota(jnp.int32, (pw_lanes, pw_lanes), 1)
    same_head = (row_p >> shift) == (col_p >> shift)
    head_ones = jnp.where(same_head, 1.0, 0.0).astype(BF16)
    eye_p = (row_p == col_p).astype(F32)
    row_c = lax.broadcasted_iota(jnp.int32, (c, pw_lanes), 0)
    col_c = lax.broadcasted_iota(jnp.int32, (c, pw_lanes), 1) & (hs - 1)
    tril_strict = row_c > col_c
    tril_incl = row_c >= col_c
    eye_c = (row_c == col_c).astype(F32)
    row_t = lax.broadcasted_iota(jnp.int32, (c, 3 * c), 0)
    col_t = lax.broadcasted_iota(jnp.int32, (c, 3 * c), 1)
    col_t = jnp.where(col_t < c, col_t, jnp.where(col_t < 2 * c, col_t - c, col_t - 2 * c))
    tril3 = jnp.where(row_t >= col_t, 1.0, 0.0).astype(BF16)
    ones2 = jnp.concatenate([head_ones] * 2, axis=0)

    def blockdiag(x):
        return jnp.concatenate([jnp.where(lane_head == i, x, 0.0) for i in range(pack)], axis=0)

    def head_sums(x):
        hi = x.astype(BF16)
        lo = (x - hi.astype(F32)).astype(BF16)
        return jnp.dot(jnp.concatenate([hi, lo], axis=1), ones2, preferred_element_type=F32)

    @pl.when(t == 0)
    def _():
        z = jnp.zeros((hs, hs), F32)
        for p in range(n_pairs):
            blk = jnp.concatenate(
                [jnp.concatenate([s0_ref[pack * p + i] if j == i else z for j in range(pack)], axis=1)
                 for i in range(pack)], axis=0)
            st_scr[p] = blk.T

    n_chunks = tt // c
    cat = jnp.concatenate
    dot = _WKV_DOT

    def each(f, *lists):
        return [f(*xs) for xs in zip(*lists)]

    def cumsum_rows(x):
        hi, mid, lo = _split3(x)
        return jnp.dot(tril3, cat([hi, mid, lo], axis=0), preferred_element_type=F32)

    def prepare(chunks, out):
        items = [(ci, p) for ci in chunks for p in range(n_pairs)]
        tiles = lambda ref: [ref[ci * c:(ci + 1) * c, p * pw_lanes:(p + 1) * pw_lanes] for ci, p in items]
        params = lambda ref: [ref[:, p * pw_lanes:(p + 1) * pw_lanes] for _, p in items]
        r, k, v, a_sig, ld = (tiles(ref) for ref in (r_ref, k_ref, v_ref, a_ref, ld_ref))
        kkw, kaw, rkw = (params(ref) for ref in (kkw_ref, kaw_ref, rkw_ref))
        kk = each(lambda k_, w_: k_ * w_, k, kkw)
        kmod = each(lambda k_, a_, w_: k_ * (1.0 + (a_ - 1.0) * w_), k, a_sig, kaw)
        sums = each(lambda kk_, r_, km_, w_: head_sums(cat([kk_ * kk_, r_ * km_ * w_], axis=0)), kk, r, kmod, rkw)
        yield
        kk = each(lambda kk_, s_: kk_ / jnp.maximum(jnp.sqrt(s_[0:c]), KK_NORM_FLOOR), kk, sums)
        b = each(lambda kk_, a_: kk_ * a_, kk, a_sig)
        cum = each(cumsum_rows, ld)
        yield
        at = each(lambda kk_, cu, l_: -kk_ * jnp.exp(cu - l_), kk, cum, ld)
        rt = each(lambda r_, cu: r_ * jnp.exp(cu), r, cum)
        inv = each(lambda cu: jnp.exp(-cu), cum)
        bt = each(lambda b_, i_: b_ * i_, b, inv)
        kt = each(lambda km_, i_: km_ * i_, kmod, inv)
        yield
        to_end = each(lambda cu: jnp.exp(cu[c - 1:c, :] - cu), cum)
        bk_end = each(lambda b_, km_, te: cat([b_ * te, km_ * te], axis=0), b, kmod, to_end)
        p_end_col = each(lambda cu: jnp.sum(eye_p * jnp.exp(cu[c - 1:c, :]), axis=-1, keepdims=True), cum)
        gm = each(lambda at_, rt_, bt_, kt_: dot(cat([at_, rt_], axis=0),
                                                 cat([blockdiag(bt_), blockdiag(kt_)], axis=0), _NT),
                  at, rt, bt, kt)
        yield
        pc = pack * c
        l_ab = [jnp.where(tril_strict, g[0:c, 0:pc], 0.0) for g in gm]
        l_ak = [jnp.where(tril_strict, g[0:c, pc:2 * pc], 0.0) for g in gm]
        m_rb = [jnp.where(tril_incl, g[c:2 * c, 0:pc], 0.0) for g in gm]
        m_rk = [jnp.where(tril_incl, g[c:2 * c, pc:2 * pc], 0.0) for g in gm]
        t_inv = [eye_c + l_ for l_ in l_ab]
        pw = l_ab
        span = 2
        while span < c:
            pw = each(lambda q: dot(q, blockdiag(q)), pw)
            yield
            t_inv = each(lambda t_, q: t_ + dot(q, blockdiag(t_)), t_inv, pw)
            yield
            span *= 2
        out.update(items=items, r=r, v=v, kmod=kmod, bonus=[s_[c:2 * c] for s_ in sums], at=at, rt=rt,
                   l_ak=l_ak, m_rb=m_rb, m_rk=m_rk, t_inv=t_inv, bk_end=bk_end, p_end_col=p_end_col)

    def state_pass(d):
        items = d["items"]
        y = []
        for i0 in range(0, len(items), n_pairs):
            idx = range(i0, i0 + n_pairs)
            st = [st_scr[p] for p in range(n_pairs)]
            vb = [blockdiag(d["v"][i]) for i in idx]
            x = [dot(cat([d["at"][i], d["l_ak"][i]], axis=1), cat([st[p], vb[p]], axis=0))
                 for p, i in enumerate(idx)]
            yield
            u = [dot(d["t_inv"][i], blockdiag(x[p])) for p, i in enumerate(idx)]
            yield
            y += [dot(cat([d["rt"][i], d["m_rk"][i], d["m_rb"][i]], axis=1),
                      cat([st[p], vb[p], blockdiag(u[p])], axis=0)) for p, i in enumerate(idx)]
            upd = [dot(d["bk_end"][i], cat([u[p], d["v"][i]], axis=0), _TN) for p, i in enumerate(idx)]
            for p, i in enumerate(idx):
                st_scr[p] = d["p_end_col"][i] * st[p] + jnp.where(same_head, upd[p], 0.0)
            yield
        s2 = each(lambda y_: head_sums(cat([y_, y_ * y_], axis=0)), y)
        yield
        for (ci, p), y_, s_, bo_, v_ in zip(items, y, s2, d["bonus"], d["v"]):
            ln = slice(p * pw_lanes, (p + 1) * pw_lanes)
            mean = s_[0:c] * (1.0 / hs)
            var = s_[c:2 * c] * (1.0 / hs) - mean * mean
            yn = (y_ - mean) * lax.rsqrt(var + hs * GROUPNORM_EPS_PER_CHANNEL) * lnw_ref[:, ln] + lnb_ref[:, ln]
            y_o[ci * c:(ci + 1) * c, ln] = yn + bo_ * v_

    def run_together(*progs):
        progs = list(progs)
        while progs:
            for prog in list(progs):
                if next(prog, StopIteration) is StopIteration:
                    progs.remove(prog)

    halves = [range(h, min(h + WKV_BATCH_CHUNKS, n_chunks)) for h in range(0, n_chunks, WKV_BATCH_CHUNKS)]
    prev = None
    for chunks in halves:
        cur = {}
        if prev is None:
            run_together(prepare(chunks, cur))
        else:
            run_together(prepare(chunks, cur), state_pass(prev))
        prev = cur
    run_together(state_pass(prev))

    @pl.when(t == pl.num_programs(1) - 1)
    def _():
        for p in range(n_pairs):
            s_grp = st_scr[p].T
            for i in range(pack):
                s_o[pack * p + i] = s_grp[i * hs:(i + 1) * hs, i * hs:(i + 1) * hs]


def _mix_out_attn_seq_kernel(n_xheads, x_ref, y_ref, g_ref, yb_ref, wout_ref, gx_ref, wq_ref,
                             mk_ref, mv_ref, wo_ref, x2_o):
    mixed = jnp.concatenate([y_ref[...] * g_ref[...], yb_ref[...]], axis=-1)
    x1 = x_ref[...] + _bdot(mixed, wout_ref[...])
    q = _bdot(_rms(x1, gx_ref[...]), wq_ref[...])
    d = q.shape[1]
    xd = d // n_xheads
    scale = xd ** -0.5
    outs = []
    for h in range(n_xheads):
        sl = slice(h * xd, (h + 1) * xd)
        s = lax.dot_general(q[:, sl].astype(BF16), mk_ref[:, sl].astype(BF16),
                            (((1,), (1,)), ((), ())), preferred_element_type=F32) * scale
        outs.append(_bdot(_softmax_rows(s), mv_ref[:, sl]))
    o = jnp.concatenate(outs, axis=-1)
    x2_o[...] = x1 + _bdot(o, wo_ref[...])


def _ffn_seq_kernel(final, *refs):
    it = iter(refs)
    x_ref, gf_ref, wup_ref, fprev_ref, fcw_ref, wdown_ref = (next(it) for _ in range(6))
    gfin_ref = next(it) if final else None
    y_o, fnew_o, carry = (next(it) for _ in range(3))

    t = pl.program_id(1)
    tm = x_ref.shape[0]
    f = wdown_ref.shape[0]
    fc = FFN_COL_TILE

    @pl.when(t == 0)
    def _():
        carry[0:2, :] = fprev_ref[...]

    x = x_ref[...]
    xf = _rms(x, gf_ref[...]).astype(BF16)
    acc = x

    def up_cols(c0):
        return jnp.dot(xf, wup_ref[:, c0:c0 + fc], preferred_element_type=F32)

    def conv_cols(up, c0):
        p0 = carry[0:1, c0:c0 + fc]
        p1 = carry[1:2, c0:c0 + fc]
        up1 = _shift_rows(up, [p1])
        up2 = _shift_rows(up, [p0, p1])
        carry[0:2, c0:c0 + fc] = up[tm - 2:tm, :]
        return (fcw_ref[0:1, c0:c0 + fc] * up2 + fcw_ref[1:2, c0:c0 + fc] * up1
                + fcw_ref[2:3, c0:c0 + fc] * up)

    n_tiles = f // fc
    ahead = [(up_cols(i * fc), up_cols(f + i * fc)) for i in range(min(FFN_LOOKAHEAD, n_tiles))]
    for j in range(n_tiles):
        c0 = j * fc
        cur = ahead.pop(0)
        if j + FFN_LOOKAHEAD < n_tiles:
            ahead.append((up_cols(c0 + FFN_LOOKAHEAD * fc), up_cols(f + c0 + FFN_LOOKAHEAD * fc)))
        u = conv_cols(cur[0], c0)
        gt = conv_cols(cur[1], f + c0)
        hidden = gt * _sigmoid(gt) * u
        acc = acc + _bdot(hidden, wdown_ref[c0:c0 + fc, :])
    if final:
        acc = _rms(acc, gfin_ref[...])
    y_o[...] = acc

    @pl.when(t == pl.num_programs(1) - 1)
    def _():
        fnew_o[...] = carry[0:2, :]


def _mem_kv_kernel(m_ref, g_ref, wk_ref, wv_ref, k_o, v_o):
    mn = _rms(m_ref[...], g_ref[...])
    k_o[...] = _bdot(mn, wk_ref[...])
    v_o[...] = _bdot(mn, wv_ref[...])


def _mix_in_step_kernel(has_vmix, lora_dims, *refs):
    it = iter(refs)
    x_ref, gmix_ref, win_ref, sprev_ref, cprev_ref, mu_ref, w0_ref, a0_ref, wl_ref, cw_ref = (
        next(it) for _ in range(10))
    if has_vmix:
        vf_ref, v0_ref, v1_ref, v2_ref = (next(it) for _ in range(4))
    r_o, k_o, v_o, a_o, ld_o, vrow_o, g_o, yb_o, xn_o, cnew_o = (next(it) for _ in range(10))

    nb = x_ref.shape[0]
    rc = mu_ref.shape[1]
    cw = cw_ref.shape[1]
    rw = w0_ref.shape[1]

    xn = _rms(x_ref[...], gmix_ref[...])
    xn_o[...] = xn
    stacked = jnp.concatenate([xn, sprev_ref[...]], axis=0)
    proj = _bdot(stacked, win_ref[...])
    p_cur = proj[0:nb, 0:rc]
    p_shift = proj[nb:2 * nb, 0:rc]
    vmix = None
    if has_vmix:
        vmix = (vf_ref[...], v0_ref[...], v1_ref[...], v2_ref[...])
    r, k, v, a_sig, log_decay, gate = _mix_rows(
        p_cur, p_shift, mu_ref[...], w0_ref[...], a0_ref[...], wl_ref[...], lora_dims, rw, vmix)
    r_o[...] = r.T
    k_o[...] = k.T
    v_o[...] = v.T
    a_o[...] = a_sig.T
    ld_o[...] = log_decay.T
    vrow_o[...] = v
    g_o[...] = gate

    gate_b = proj[0:nb, rc:rc + cw]
    gate_c = proj[0:nb, rc + cw:rc + 2 * cw]
    h_in = proj[0:nb, rc + 2 * cw:rc + 3 * cw]
    u = gate_c * h_in
    u2 = cprev_ref[:, 0, :]
    u1 = cprev_ref[:, 1, :]
    yb_o[...] = gate_b * (cw_ref[0:1, :] * u2 + cw_ref[1:2, :] * u1 + cw_ref[2:3, :] * u)
    cnew_o[:, 0, :] = u1
    cnew_o[:, 1, :] = u


def _wkv_step_kernel(aliased, *refs):
    r_ref, k_ref, v_ref, a_ref, ld_ref, kkw_ref, kaw_ref, rkw_ref, lnw_ref, lnb_ref, s_ref = refs[:11]
    y_o, s_o, y_scr = refs[12:] if aliased else refs[11:]
    n = s_ref.shape[0]

    @pl.when(pl.program_id(0) == 0)
    def _():
        r, k, v, a_sig, ld = r_ref[...], k_ref[...], v_ref[...], a_ref[...], ld_ref[...]
        kk = k * kkw_ref[...]
        kk = kk / jnp.maximum(jnp.sqrt(jnp.sum(kk * kk, axis=0, keepdims=True)), KK_NORM_FLOOR)
        kmod = k * (1.0 + (a_sig - 1.0) * kaw_ref[...])
        a = -kk
        b = kk * a_sig
        decay = jnp.exp(ld)
        for vi in range(n):
            s = s_ref[vi]
            sa = jnp.sum(s * a, axis=0, keepdims=True)
            s_new = s * decay + sa * b + v[vi:vi + 1, :] * kmod
            s_o[vi] = s_new
            y_scr[vi:vi + 1, :] = jnp.sum(s_new * r, axis=0, keepdims=True)
        y = y_scr[...]
        mean = jnp.mean(y, axis=0, keepdims=True)
        yc = y - mean
        var = jnp.mean(yc * yc, axis=0, keepdims=True)
        y = yc * lax.rsqrt(var + n * GROUPNORM_EPS_PER_CHANNEL) * lnw_ref[...] + lnb_ref[...]
        y_o[...] = y + jnp.sum(r * kmod * rkw_ref[...], axis=0, keepdims=True) * v

    @pl.when(pl.program_id(0) > 0)
    def _():
        s_o[...] = jnp.zeros(s_o.shape, F32)


def _mix_out_q_step_kernel(x_ref, ya_ref, g_ref, yb_ref, wout_ref, gx_ref, wq_ref, x1_o, q_o):
    mixed = jnp.concatenate([ya_ref[...].T * g_ref[...], yb_ref[...]], axis=-1)
    x1 = x_ref[...] + _bdot(mixed, wout_ref[...])
    x1_o[...] = x1
    q_o[...] = _bdot(_rms(x1, gx_ref[...]), wq_ref[...])


def _attn_step_kernel(q_ref, mk_ref, mv_ref, o_o):
    bt = q_ref.shape[0]
    m, nh, xd = mk_ref.shape[1:]
    scale = xd ** -0.5
    fold = SUBLANES // nh if SUBLANES % nh == 0 and m % max(SUBLANES // nh, 1) == 0 else 1

    def folded(x):
        return [x[j * nh:(j + 1) * nh] for j in range(fold)]

    def body(i, carry):
        q = jnp.concatenate([q_ref[i]] * fold, axis=0)
        k = mk_ref[i].reshape(m // fold, fold * nh, xd)
        v = mv_ref[i].reshape(m // fold, fold * nh, xd)
        s = jnp.sum(k * q, axis=-1, keepdims=True) * scale
        mx = functools.reduce(jnp.maximum, folded(jnp.max(s, axis=0)))
        e = jnp.exp(s - jnp.concatenate([mx] * fold, axis=0))
        den = sum(folded(jnp.sum(e, axis=0)))
        acc = sum(folded(jnp.sum(e * v, axis=0)))
        o_o[i] = acc / den
        return carry

    lax.fori_loop(0, bt, body, 0)


def _ffn_step_kernel(final, aliased, *refs):
    it = iter(refs)
    x1_ref, o_ref, wo_ref, gf_ref, wup_ref, fprev_ref, fcw_ref, wdown_ref = (next(it) for _ in range(8))
    gfin_ref = next(it) if final else None
    if aliased:
        next(it)
    y_o, fnew_o = (next(it) for _ in range(2))
    f = wdown_ref.shape[0]
    f2 = 2 * f

    @pl.when(pl.program_id(0) == 0)
    def _():
        x2 = x1_ref[...] + _bdot(o_ref[...], wo_ref[...])
        up = _bdot(_rms(x2, gf_ref[...]), wup_ref[...])
        up2 = fprev_ref[:, 0, :]
        up1 = fprev_ref[:, 1, :]
        upc = fcw_ref[0:1, :] * up2 + fcw_ref[1:2, :] * up1 + fcw_ref[2:3, :] * up
        fnew_o[:, 0, :] = up1
        fnew_o[:, 1, :] = up
        u = upc[:, 0:f]
        gt = upc[:, f:f2]
        x3 = x2 + _bdot(gt * _sigmoid(gt) * u, wdown_ref[...])
        if final:
            x3 = _rms(x3, gfin_ref[...])
        y_o[...] = x3

    @pl.when(pl.program_id(0) > 0)
    def _():
        fnew_o[...] = jnp.zeros(fnew_o.shape, F32)


def _params(*sem):
    return pltpu.CompilerParams(dimension_semantics=sem, vmem_limit_bytes=V7X_VMEM_LIMIT_BYTES)


class _Layered(NamedTuple):
    stacked: jax.Array
    layer: int

    @property
    def shape(self):
        return self.stacked.shape[1:]


def _operand(x):
    return x.stacked if isinstance(x, _Layered) else x


def _whole(x):
    if isinstance(x, _Layered):
        nd, l = x.stacked.ndim, x.layer
        return pl.BlockSpec((None,) + x.shape, lambda *_: (l,) + (0,) * (nd - 1), pipeline_mode=pl.Buffered(1))
    nd = x.ndim
    return pl.BlockSpec(x.shape, lambda *_: (0,) * nd, pipeline_mode=pl.Buffered(1))


def _sds(shape):
    return jax.ShapeDtypeStruct(shape, F32)


def _stacked_weights(p):
    depth = p["w_in"].shape[0]
    rows = lambda a: a.reshape(a.shape[0], 1, -1)
    rw = p["w0"].shape[1]
    d_decay, d_aaa, d_gate = p["w2"].shape[1], p["a2"].shape[1], p["g2"].shape[1]
    pad_cols = lambda a, before, after: jnp.pad(a, ((0, 0), (0, 0), (before, after)))
    w_lora = jnp.concatenate([pad_cols(p["w2"], 0, 2 * rw), pad_cols(p["a2"], rw, rw), pad_cols(p["g2"], 2 * rw, 0)],
                             axis=1)
    lanes = 128
    mv = p["v1"].shape[2]
    mvp = -(-mv // lanes) * lanes
    w = dict(
        norm_mix=rows(p["norm_mix"]), w_in=p["w_in"].astype(BF16), mu=rows(p["mu_shift"]), w0=rows(p["w0"]),
        a0=rows(p["a0"]), w_lora=w_lora.astype(BF16), conv_w=p["conv_w"], k_k=p["k_k"], k_a=p["k_a"],
        r_k=p["r_k"], ln_w=p["ln_x_w"], ln_b=p["ln_x_b"], w_out=p["w_out"].astype(BF16), norm_x=rows(p["norm_x"]),
        wq=p["wq"].astype(BF16), wo=p["wo"].astype(BF16), norm_ffn=rows(p["norm_ffn"]),
        w_up=p["w_up"].astype(BF16), ffn_conv_w=p["ffn_conv_w"], w_down=p["w_down"].astype(BF16),
        norm_mem=rows(p["norm_mem"]), wk=p["wk"].astype(BF16), wv=p["wv"].astype(BF16),
    )
    layers = []
    for l in range(depth):
        wl = {k_: _Layered(v_, l) for k_, v_ in w.items()}
        wl["norm_final"] = p["norm_final"].reshape(1, -1)
        wl["lora_dims"] = (d_decay, d_aaa, d_gate)
        if l > 0:
            wl["v0"] = _Layered(rows(p["v0"]), l - 1)
            wl["v1"] = _Layered(jnp.pad(p["v1"], ((0, 0), (0, 0), (0, mvp - mv))).astype(BF16), l - 1)
            wl["v2"] = _Layered(jnp.pad(p["v2"], ((0, 0), (0, mvp - mv), (0, 0))).astype(BF16), l - 1)
        layers.append(wl)
    return layers


def _head_rows(x, n_heads):
    st = x.stacked
    return _Layered(st.reshape(st.shape[0], n_heads, st.shape[1] // n_heads, 1), x.layer)


def _mem_kv(mem, w):
    rows, d = mem.shape
    tm = min(512, rows)
    args = [mem, w["norm_mem"], w["wk"], w["wv"]]
    return pl.pallas_call(
        _mem_kv_kernel,
        grid=(rows // tm,),
        in_specs=[pl.BlockSpec((tm, d), lambda i: (i, 0))] + [_whole(a) for a in args[1:]],
        out_specs=[pl.BlockSpec((tm, d), lambda i: (i, 0))] * 2,
        out_shape=[_sds((rows, d))] * 2,
        compiler_params=_params("parallel"),
        name="mem_kv",
    )(*map(_operand, args))


def _prompt_layer(l, x, mem_k, mem_v, shift0, wkv0, conv0, ffn0, v_first, w, n_heads, n_xheads, final):
    mem_k, mem_v, shift0, wkv0, conv0, ffn0 = (a[l] for a in (mem_k, mem_v, shift0, wkv0, conv0, ffn0))
    bsz, t, d = x.shape
    rw = w["w0"].shape[1]
    hs = rw // n_heads
    cw = w["conv_w"].shape[1]
    rc = w["mu"].shape[1]
    tm = min(ROW_TILE, t)
    nt = t // tm
    has_vmix = v_first is not None

    tile = lambda n: pl.BlockSpec((None, tm, n), lambda b, i: (b, i, 0))
    per_b = lambda s: pl.BlockSpec((None,) + s, lambda b, i: (b,) + (0,) * len(s))

    ins = [x, w["norm_mix"], w["w_in"], shift0[:, None, :], conv0, w["mu"], w["w0"], w["a0"], w["w_lora"],
           w["conv_w"]]
    specs = [tile(d), _whole(w["norm_mix"]), _whole(w["w_in"]), per_b((1, d)), per_b((2, cw)),
             _whole(w["mu"]), _whole(w["w0"]), _whole(w["a0"]), _whole(w["w_lora"]), _whole(w["conv_w"])]
    if has_vmix:
        ins += [v_first, w["v0"], w["v1"], w["v2"]]
        specs += [tile(rw), _whole(w["v0"]), _whole(w["v1"]), _whole(w["v2"])]
    out_shape = [_sds((bsz, t, rw))] * 6 + [_sds((bsz, t, cw)), _sds((bsz, 1, d)), _sds((bsz, 2, cw))]
    out_specs = [tile(rw)] * 6 + [tile(cw), per_b((1, d)), per_b((2, cw))]
    r, k, v, a_sig, ld, gate, y_b, xn_last, conv_new = pl.pallas_call(
        functools.partial(_mix_in_seq_kernel, has_vmix, w["lora_dims"]),
        grid=(bsz, nt), in_specs=specs, out_specs=out_specs, out_shape=out_shape,
        scratch_shapes=[pltpu.VMEM((8, rc), F32), pltpu.VMEM((8, cw), F32)],
        compiler_params=_params("parallel", "arbitrary"),
        name="mix_in_seq",
    )(*map(_operand, ins))
    if not has_vmix:
        v_first = v

    head_params = [_Layered(a.stacked[:, None, :], a.layer) for a in
                   (w["k_k"], w["k_a"], w["r_k"], w["ln_w"], w["ln_b"])]
    sblk = per_b((n_heads, hs, hs))
    wt = min(WKV_ROW_TILE, t)
    wtile = pl.BlockSpec((None, wt, rw), lambda b, i: (b, i, 0))
    y_a, wkv_new = pl.pallas_call(
        functools.partial(_wkv_seq_kernel, hs),
        grid=(bsz, t // wt),
        in_specs=[wtile] * 5 + [_whole(a) for a in head_params] + [sblk],
        out_specs=[wtile, sblk],
        out_shape=[_sds((bsz, t, rw)), _sds((bsz, n_heads, hs, hs))],
        scratch_shapes=[pltpu.VMEM((rw // (WKV_HEAD_PACK * hs), WKV_HEAD_PACK * hs, WKV_HEAD_PACK * hs), F32)],
        compiler_params=_params("parallel", "arbitrary"),
        name="wkv_seq",
    )(r, k, v, a_sig, ld, *map(_operand, head_params), wkv0)

    m = mem_k.shape[1]
    x2 = pl.pallas_call(
        functools.partial(_mix_out_attn_seq_kernel, n_xheads),
        grid=(bsz, nt),
        in_specs=[tile(d), tile(rw), tile(rw), tile(cw), _whole(w["w_out"]), _whole(w["norm_x"]), _whole(w["wq"]),
                  per_b((m, d)), per_b((m, d)), _whole(w["wo"])],
        out_specs=tile(d), out_shape=_sds((bsz, t, d)),
        compiler_params=_params("parallel", "arbitrary"),
        name="mix_out_attn_seq",
    )(*map(_operand, (x, y_a, gate, y_b, w["w_out"], w["norm_x"], w["wq"], mem_k, mem_v, w["wo"])))

    f2 = w["w_up"].shape[1]
    ins = [x2, w["norm_ffn"], w["w_up"], ffn0, w["ffn_conv_w"], w["w_down"]]
    ft = min(FFN_ROW_TILE, t)
    ftile = pl.BlockSpec((None, ft, d), lambda b, i: (b, i, 0))
    specs = [ftile, _whole(w["norm_ffn"]), _whole(w["w_up"]), per_b((2, f2)), _whole(w["ffn_conv_w"]),
             _whole(w["w_down"])]
    if final:
        ins.append(w["norm_final"])
        specs.append(_whole(w["norm_final"]))
    x3, ffn_new = pl.pallas_call(
        functools.partial(_ffn_seq_kernel, final),
        grid=(bsz, t // ft), in_specs=specs,
        out_specs=[ftile, per_b((2, f2))], out_shape=[_sds((bsz, t, d)), _sds((bsz, 2, f2))],
        scratch_shapes=[pltpu.VMEM((8, f2), F32)],
        compiler_params=_params("parallel", "arbitrary"),
        name="ffn_seq",
    )(*map(_operand, ins))
    return x3, xn_last[:, 0, :], wkv_new, conv_new, ffn_new, v_first


def _sample_layer(l, x, mem_k, mem_v, shift0, wkv_t, conv0, ffn0, carried, w, n_heads, n_xheads, final):
    v_first, wkv_all, ffn_all = carried
    nb, d = x.shape
    rw = w["w0"].shape[1]
    hs = rw // n_heads
    cw = w["conv_w"].shape[1]
    has_vmix = v_first is not None
    one = lambda a: pl.BlockSpec(a.shape, lambda i: (0,) * a.ndim, pipeline_mode=pl.Buffered(1))
    layer_of = lambda a: pl.BlockSpec((None,) + a.shape[1:], lambda i: (l,) + (0,) * (a.ndim - 1),
                                      pipeline_mode=pl.Buffered(1))

    ins = [x, w["norm_mix"], w["w_in"], shift0, conv0, w["mu"], w["w0"], w["a0"], w["w_lora"], w["conv_w"]]
    specs = [one(x), _whole(w["norm_mix"]), _whole(w["w_in"]), layer_of(shift0), layer_of(conv0)] + [
        _whole(a) for a in ins[5:]]
    if has_vmix:
        ins += [v_first, w["v0"], w["v1"], w["v2"]]
        specs += [one(v_first), _whole(w["v0"]), _whole(w["v1"]), _whole(w["v2"])]
    out_shape = [_sds((rw, nb))] * 5 + [_sds((nb, rw))] * 2 + [_sds((nb, cw)), _sds((nb, d)), _sds((nb, 2, cw))]
    r, k, v, a_sig, ld, v_rows, gate, y_b, xn, conv_new = pl.pallas_call(
        functools.partial(_mix_in_step_kernel, has_vmix, w["lora_dims"]),
        grid=(1,), in_specs=specs, out_specs=[one(s_) for s_ in out_shape], out_shape=out_shape,
        compiler_params=_params("arbitrary"),
        name="mix_in_step",
    )(*map(_operand, ins))
    if not has_vmix:
        v_first = v_rows

    aliased = wkv_all is not None
    n_clear = 0 if aliased else wkv_t.shape[0] - 1 - l
    last = n_heads - 1
    hsel = lambda c, h: jnp.where(c == 0, h, last)
    heads = lambda a: a.reshape(n_heads, hs, nb)
    hblk = pl.BlockSpec((None, hs, nb), lambda c, h: (hsel(c, h), 0, 0))
    head_params = [_head_rows(w[n_], n_heads) for n_ in ("k_k", "k_a", "r_k", "ln_w", "ln_b")]
    pblk = pl.BlockSpec((None, None, hs, 1), lambda c, h: (l, hsel(c, h), 0, 0))
    sblk = pl.BlockSpec((None, None, hs, hs, nb), lambda c, h: (l, hsel(c, h), 0, 0, 0))
    soblk = pl.BlockSpec((None, None, hs, hs, nb), lambda c, h: (l + c, h, 0, 0, 0))
    ins = [heads(r), heads(k), heads(v), heads(a_sig), heads(ld)] + [a.stacked for a in head_params] + [wkv_t]
    specs = [hblk] * 5 + [pblk] * 5 + [sblk]
    if aliased:
        ins.append(wkv_all)
        specs.append(pl.BlockSpec(memory_space=pl.ANY))
    y_a, wkv_all = pl.pallas_call(
        functools.partial(_wkv_step_kernel, aliased),
        grid=(1 + n_clear, n_heads),
        in_specs=specs, out_specs=[hblk, soblk],
        out_shape=[_sds((n_heads, hs, nb)), _sds(wkv_t.shape)],
        scratch_shapes=[pltpu.VMEM((hs, nb), F32)],
        input_output_aliases={len(ins) - 1: 1} if aliased else {},
        compiler_params=_params("arbitrary", "arbitrary"),
        name="wkv_step",
    )(*ins)

    ins = [x, y_a.reshape(rw, nb), gate, y_b, w["w_out"], w["norm_x"], w["wq"]]
    x1, q = pl.pallas_call(
        _mix_out_q_step_kernel,
        grid=(1,), in_specs=[one(a) for a in ins[:4]] + [_whole(a) for a in ins[4:]],
        out_specs=[one(x)] * 2, out_shape=[_sds((nb, d))] * 2,
        compiler_params=_params("arbitrary"),
        name="mix_out_q_step",
    )(*map(_operand, ins))

    m, xd = mem_k.shape[2], mem_k.shape[4]
    ab = min(ATTN_BATCH_TILE, nb)
    qblk = pl.BlockSpec((ab, n_xheads, xd), lambda i: (i, 0, 0))
    mblk = pl.BlockSpec((None, ab, m, n_xheads, xd), lambda i: (l, i, 0, 0, 0))
    o = pl.pallas_call(
        _attn_step_kernel,
        grid=(nb // ab,),
        in_specs=[qblk, mblk, mblk], out_specs=qblk, out_shape=_sds((nb, n_xheads, xd)),
        compiler_params=_params("parallel"),
        name="attn_step",
    )(q.reshape(nb, n_xheads, xd), mem_k, mem_v)
    o = o.reshape(nb, d)

    ins = [x1, o, w["wo"], w["norm_ffn"], w["w_up"], ffn0, w["ffn_conv_w"], w["w_down"]]
    fblk = pl.BlockSpec((None,) + ffn0.shape[1:], lambda c: (l, 0, 0, 0))
    foblk = pl.BlockSpec((None,) + ffn0.shape[1:], lambda c: (l + c, 0, 0, 0))
    specs = [one(x1), one(o), _whole(w["wo"]), _whole(w["norm_ffn"]), _whole(w["w_up"]), fblk,
             _whole(w["ffn_conv_w"]), _whole(w["w_down"])]
    if final:
        ins.append(w["norm_final"])
        specs.append(_whole(w["norm_final"]))
    ffn_aliased = ffn_all is not None
    if ffn_aliased:
        ins.append(ffn_all)
        specs.append(pl.BlockSpec(memory_space=pl.ANY))
    x3, ffn_all = pl.pallas_call(
        functools.partial(_ffn_step_kernel, final, ffn_aliased),
        grid=(1 if ffn_aliased else ffn0.shape[0] - l,), in_specs=specs, out_specs=[one(x), foblk],
        out_shape=[_sds((nb, d)), _sds(ffn0.shape)],
        input_output_aliases={len(ins) - 1: 1} if ffn_aliased else {},
        compiler_params=_params("arbitrary"),
        name="ffn_step",
    )(*map(_operand, ins))
    return x3, xn, conv_new, (v_first, wkv_all, ffn_all)


def kernel(x_prompt, x_sample, mem_prompt, state_shift, state_wkv, state_conv, state_ffn, cache_mem_k, cache_mem_v, norm_mix, w_in, mu_shift, w0, w2, a0, a2, g2, v0, v1, v2, k_k, k_a, r_k, ln_x_w, ln_x_b, conv_w, w_out, norm_x, norm_mem, wq, wk, wv, wo, norm_ffn, w_up, ffn_conv_w, w_down, norm_final):
    p = dict(norm_mix=norm_mix, w_in=w_in, mu_shift=mu_shift, w0=w0, w2=w2, a0=a0, a2=a2, g2=g2, v0=v0, v1=v1,
             v2=v2, k_k=k_k, k_a=k_a, r_k=r_k, ln_x_w=ln_x_w, ln_x_b=ln_x_b, conv_w=conv_w, w_out=w_out,
             norm_x=norm_x, norm_mem=norm_mem, wq=wq, wk=wk, wv=wv, wo=wo, norm_ffn=norm_ffn, w_up=w_up,
             ffn_conv_w=ffn_conv_w, w_down=w_down, norm_final=norm_final)
    depth = w_in.shape[0]
    n_heads = state_wkv.shape[2]
    hs = state_wkv.shape[3]
    n_xheads, xd = cache_mem_k.shape[3], cache_mem_k.shape[4]
    assert x_sample.shape[1] == 1, "the sample group advances one token per sequence"
    weights = _stacked_weights(p)

    bp, tp, d = x_prompt.shape
    m = mem_prompt.shape[1]
    mks, mvs = [], []
    for l in range(depth):
        mk, mv = _mem_kv(mem_prompt.reshape(bp * m, d), weights[l])
        mks.append(mk.reshape(bp, m, d))
        mvs.append(mv.reshape(bp, m, d))
    cw = conv_w.shape[2]
    f2 = w_up.shape[2]
    zeros = lambda *s_: jnp.zeros((depth, bp) + s_, F32)
    x, v_first = x_prompt, None
    shs, wks, cvs, ffs = [], [], [], []
    for l in range(depth):
        x, sh, s_, cs, fs, v_first = _prompt_layer(
            l, x, mks, mvs, zeros(d), zeros(n_heads, hs, hs), zeros(state_conv.shape[2], cw),
            zeros(state_ffn.shape[2], f2), v_first, weights[l], n_heads, n_xheads, l == depth - 1)
        shs.append(sh)
        wks.append(s_)
        cvs.append(cs)
        ffs.append(fs)
    y_p = x
    mem_k_p = jnp.stack(mks).reshape(depth, bp, m, n_xheads, xd)
    mem_v_p = jnp.stack(mvs).reshape(depth, bp, m, n_xheads, xd)

    nb = x_sample.shape[0]
    wkv_t = jnp.transpose(state_wkv, (0, 2, 3, 4, 1))
    x, carried = x_sample.reshape(nb, d), (None, None, None)
    xns, cns = [], []
    for l in range(depth):
        x, xn, cn, carried = _sample_layer(l, x, cache_mem_k, cache_mem_v, state_shift, wkv_t, state_conv,
                                           state_ffn, carried, weights[l], n_heads, n_xheads, l == depth - 1)
        xns.append(xn)
        cns.append(cn)
    _, wkv_all, ffn_s = carried
    wkv_s = jnp.transpose(wkv_all, (0, 4, 1, 2, 3))
    return (y_p, x.reshape(x_sample.shape), jnp.stack(shs), jnp.stack(wks), jnp.stack(cvs), jnp.stack(ffs),
            mem_k_p, mem_v_p, jnp.stack(xns), wkv_s, jnp.stack(cns), ffn_s)
```

```python
import functools
from typing import NamedTuple

import jax
import jax.numpy as jnp
from jax import lax
from jax.experimental import pallas as pl
from jax.experimental.pallas import tpu as pltpu

F32 = jnp.float32
BF16 = jnp.bfloat16
HIGHEST = lax.Precision.HIGHEST

RMS_EPS = 1e-6
GROUPNORM_EPS_PER_CHANNEL = 1e-5
KK_NORM_FLOOR = 1e-12

ROW_TILE = 512
WKV_ROW_TILE = 1024
FFN_ROW_TILE = 256
MIX_SUB_ROWS = 256
CHUNK = 64
WKV_HEAD_PACK = 2
WKV_BATCH_CHUNKS = 4
ATTN_BATCH_TILE = 4
WKV_BATCH_TILE = 8
FFN_COL_TILE = 256
FFN_LOOKAHEAD = 3
V7X_VMEM_LIMIT_BYTES = 56 * 1024 * 1024
SUBLANES = 8


def _rms(x, g):
    ms = jnp.mean(x * x, axis=-1, keepdims=True)
    return x * lax.rsqrt(ms + RMS_EPS) * g


def _bdot(a, b):
    return jnp.dot(a.astype(BF16), b.astype(BF16), preferred_element_type=F32)


def _hdot(a, b, dims=(((1,), (0,)), ((), ()))):
    return lax.dot_general(a, b, dims, precision=HIGHEST, preferred_element_type=F32)


def _sigmoid(x):
    return 1.0 / (1.0 + jnp.exp(-x))


def _softplus(x):
    return jnp.maximum(x, 0.0) + jnp.log1p(jnp.exp(-jnp.abs(x)))


def _shift_rows(x, prev_rows):
    s = len(prev_rows)
    rolled = pltpu.roll(x, s, axis=0)
    top = rolled[0:SUBLANES, :]
    row = lax.broadcasted_iota(jnp.int32, top.shape, 0)
    for i, pr in enumerate(prev_rows):
        top = jnp.where(row == i, pr, top)
    return jnp.concatenate([top, rolled[SUBLANES:, :]], axis=0)


def _mix_rows_stages(p_cur, p_shift, mu, w0, a0, w_lora, lora_dims, rw, vmix, out):
    d_decay, d_aaa, _ = lora_dims
    z = p_cur + (p_shift - p_cur) * mu
    r = z[:, 0:rw]
    k = z[:, rw:2 * rw]
    v = z[:, 2 * rw:3 * rw]
    lo = z[:, 3 * rw:]
    lane = lax.broadcasted_iota(jnp.int32, lo.shape, 1)
    feat = jnp.where(lane < d_decay, jnp.tanh(lo),
                     jnp.where(lane < d_decay + d_aaa, lo, _sigmoid(lo)))
    yield
    lora = _bdot(feat, w_lora)
    yield
    w = -_softplus(-(w0 + lora[:, 0:rw])) - 0.5
    log_decay = -jnp.exp(w)
    a_sig = _sigmoid(a0 + lora[:, rw:2 * rw])
    gate = lora[:, 2 * rw:3 * rw]
    if vmix is not None:
        v_first, v0, v1, v2 = vmix
        t = _bdot(v, v1)
        yield
        pre = _bdot(t, v2)
        yield
        v = v + (v_first - v) * _sigmoid(v0 + pre)
    out.extend([r, k, v, a_sig, log_decay, gate])


def _mix_rows(*args):
    out = []
    for _ in _mix_rows_stages(*args, out):
        pass
    return out


def _run_together(*progs):
    progs = list(progs)
    while progs:
        for prog in list(progs):
            if next(prog, StopIteration) is StopIteration:
                progs.remove(prog)


def _group_norm_bonus(y, r, kmod, v, rk, lnw, lnb, n):
    mean = jnp.mean(y, axis=-1, keepdims=True)
    yc = y - mean
    var = jnp.mean(yc * yc, axis=-1, keepdims=True)
    y = yc * lax.rsqrt(var + n * GROUPNORM_EPS_PER_CHANNEL) * lnw + lnb
    return y + jnp.sum(r * kmod * rk, axis=-1, keepdims=True) * v


def _key_features(k, a_sig, kk_w, ka_w):
    kk = k * kk_w
    nrm = jnp.sqrt(jnp.sum(kk * kk, axis=-1, keepdims=True))
    kk = kk / jnp.maximum(nrm, KK_NORM_FLOOR)
    kmod = k * (1.0 + (a_sig - 1.0) * ka_w)
    return kk, kmod


def _softmax_rows(s):
    m = jnp.max(s, axis=-1, keepdims=True)
    e = jnp.exp(s - m)
    return e / jnp.sum(e, axis=-1, keepdims=True)


def _mix_in_seq_kernel(has_vmix, lora_dims, *refs):
    it = iter(refs)
    x_ref, gmix_ref, win_ref, sprev_ref, cprev_ref, mu_ref, w0_ref, a0_ref, wl_ref, cw_ref = (
        next(it) for _ in range(10))
    if has_vmix:
        vf_ref, v0_ref, v1_ref, v2_ref = (next(it) for _ in range(4))
    r_o, k_o, v_o, a_o, ld_o, g_o, yb_o, xn_o, cnew_o, pc_scr, uc_scr = (next(it) for _ in range(11))

    t = pl.program_id(1)
    tm = x_ref.shape[0]
    rc = mu_ref.shape[1]
    cw = cw_ref.shape[1]
    rw = w0_ref.shape[1]

    @pl.when(t == 0)
    def _():
        sprev = jnp.broadcast_to(sprev_ref[...], (8, sprev_ref.shape[1]))
        pc_scr[...] = _bdot(sprev, win_ref[:, 0:rc])
        uc_scr[0:2, :] = cprev_ref[...]

    n_sub = max(1, tm // MIX_SUB_ROWS)
    sub = tm // n_sub
    col_chunks = [(0, 2 * rw), (2 * rw, rc), (rc, rc + cw), (rc + cw, rc + 3 * cw)]
    xns = [_rms(x_ref[i * sub:(i + 1) * sub, :], gmix_ref[...]) for i in range(n_sub)]
    carry = dict(p=pc_scr[0:1, :], u=[uc_scr[0:1, :], uc_scr[1:2, :]])

    def project(i, out):
        xb = xns[i].astype(BF16)
        for c0, c1 in col_chunks:
            out.append(jnp.dot(xb, win_ref[:, c0:c1], preferred_element_type=F32))
            yield

    def features(i, parts):
        rows = slice(i * sub, (i + 1) * sub)
        p_cur = jnp.concatenate(parts[0:2], axis=1)
        p_shift = _shift_rows(p_cur, [carry["p"]])
        vmix = None
        if has_vmix:
            vmix = (vf_ref[rows, :], v0_ref[...], v1_ref[...], v2_ref[...])
        mixed = []
        yield from _mix_rows_stages(p_cur, p_shift, mu_ref[...], w0_ref[...], a0_ref[...], wl_ref[...],
                                    lora_dims, rw, vmix, mixed)
        for ref, val in zip((r_o, k_o, v_o, a_o, ld_o, g_o), mixed):
            ref[rows, :] = val
        yield
        gate_b = parts[2]
        u = parts[3][:, 0:cw] * parts[3][:, cw:2 * cw]
        u1 = _shift_rows(u, [carry["u"][1]])
        u2 = _shift_rows(u, carry["u"])
        yb_o[rows, :] = gate_b * (cw_ref[0:1, :] * u2 + cw_ref[1:2, :] * u1 + cw_ref[2:3, :] * u)
        carry["p"] = p_cur[sub - 1:sub, :]
        carry["u"] = [u[sub - 2:sub - 1, :], u[sub - 1:sub, :]]

    parts = []
    _run_together(project(0, parts))
    for i in range(n_sub):
        nxt = []
        if i + 1 < n_sub:
            _run_together(features(i, parts), project(i + 1, nxt))
        else:
            _run_together(features(i, parts))
        parts = nxt

    pc_scr[0:1, :] = carry["p"]
    uc_scr[0:1, :] = carry["u"][0]
    uc_scr[1:2, :] = carry["u"][1]

    @pl.when(t == pl.num_programs(1) - 1)
    def _():
        xn_o[...] = xns[-1][sub - 1:sub, :]
        cnew_o[0:1, :] = carry["u"][0]
        cnew_o[1:2, :] = carry["u"][1]


def _split3(x):
    hi = x.astype(BF16)
    r1 = x - hi.astype(F32)
    mid = r1.astype(BF16)
    lo = (r1 - mid.astype(F32)).astype(BF16)
    return hi, mid, lo


def _dot3(a, b, dims=(((1,), (0,)), ((), ()))):
    a_hi = a.astype(BF16)
    a_lo = (a - a_hi.astype(F32)).astype(BF16)
    b_hi = b.astype(BF16)
    b_lo = (b - b_hi.astype(F32)).astype(BF16)
    d = functools.partial(lax.dot_general, dimension_numbers=dims, preferred_element_type=F32)
    return d(a_hi, b_hi) + d(a_lo, b_hi) + d(a_hi, b_lo)


def _dot1(a, b, dims=(((1,), (0,)), ((), ()))):
    return lax.dot_general(a.astype(BF16), b.astype(BF16), dims, preferred_element_type=F32)


_NT = (((1,), (1,)), ((), ()))
_TN = (((0,), (0,)), ((), ()))
_WKV_DOT = _dot1


def _wkv_seq_kernel(hs, r_ref, k_ref, v_ref, a_ref, ld_ref, kkw_ref, kaw_ref, rkw_ref, lnw_ref, lnb_ref,
                    s0_ref, y_o, s_o, st_scr):
    t = pl.program_id(1)
    tt, rw = r_ref.shape
    c = CHUNK
    pack = WKV_HEAD_PACK
    pw_lanes = pack * hs
    n_pairs = rw // pw_lanes
    assert c == hs, "chunk length equals the head size in the packed layout"
    shift = hs.bit_length() - 1
    assert 1 << shift == hs

    lane_head = lax.broadcasted_iota(jnp.int32, (1, pw_lanes), 1) >> shift
    row_p = lax.broadcasted_iota(jnp.int32, (pw_lanes, pw_lanes), 0)
    col_p = lax.broadcasted_iota(jnp.int32, (pw_lanes, pw_lanes), 1)
    same_head = (row_p >> shift) == (col_p >> shift)
    head_ones = jnp.where(same_head, 1.0, 0.0).astype(BF16)
    eye_p = (row_p == col_p).astype(F32)
    row_c = lax.broadcasted_iota(jnp.int32, (c, pw_lanes), 0)
    col_c = lax.broadcasted_iota(jnp.int32, (c, pw_lanes), 1) & (hs - 1)
    tril_strict = row_c > col_c
    tril_incl = row_c >= col_c
    eye_c = (row_c == col_c).astype(F32)
    row_t = lax.broadcasted_iota(jnp.int32, (c, 3 * c), 0)
    col_t = lax.broadcasted_iota(jnp.int32, (c, 3 * c), 1)
    col_t = jnp.where(col_t < c, col_t, jnp.where(col_t < 2 * c, col_t - c, col_t - 2 * c))
    tril3 = jnp.where(row_t >= col_t, 1.0, 0.0).astype(BF16)
    ones2 = jnp.concatenate([head_ones] * 2, axis=0)

    def blockdiag(x):
        return jnp.concatenate([jnp.where(lane_head == i, x, 0.0) for i in range(pack)], axis=0)

    def head_sums(x):
        hi = x.astype(BF16)
        lo = (x - hi.astype(F32)).astype(BF16)
        return jnp.dot(jnp.concatenate([hi, lo], axis=1), ones2, preferred_element_type=F32)

    @pl.when(t == 0)
    def _():
        z = jnp.zeros((hs, hs), F32)
        for p in range(n_pairs):
            blk = jnp.concatenate(
                [jnp.concatenate([s0_ref[pack * p + i] if j == i else z for j in range(pack)], axis=1)
                 for i in range(pack)], axis=0)
            st_scr[p] = blk.T

    n_chunks = tt // c
    cat = jnp.concatenate
    dot = _WKV_DOT

    def each(f, *lists):
        return [f(*xs) for xs in zip(*lists)]

    def cumsum_rows(x):
        hi, mid, lo = _split3(x)
        return jnp.dot(tril3, cat([hi, mid, lo], axis=0), preferred_element_type=F32)

    def prepare(chunks, out):
        items = [(ci, p) for ci in chunks for p in range(n_pairs)]
        tiles = lambda ref: [ref[ci * c:(ci + 1) * c, p * pw_lanes:(p + 1) * pw_lanes] for ci, p in items]
        params = lambda ref: [ref[:, p * pw_lanes:(p + 1) * pw_lanes] for _, p in items]
        r, k, v, a_sig, ld = (tiles(ref) for ref in (r_ref, k_ref, v_ref, a_ref, ld_ref))
        kkw, kaw, rkw = (params(ref) for ref in (kkw_ref, kaw_ref, rkw_ref))
        kk = each(lambda k_, w_: k_ * w_, k, kkw)
        kmod = each(lambda k_, a_, w_: k_ * (1.0 + (a_ - 1.0) * w_), k, a_sig, kaw)
        sums = each(lambda kk_, r_, km_, w_: head_sums(cat([kk_ * kk_, r_ * km_ * w_], axis=0)), kk, r, kmod, rkw)
        yield
        kk = each(lambda kk_, s_: kk_ / jnp.maximum(jnp.sqrt(s_[0:c]), KK_NORM_FLOOR), kk, sums)
        b = each(lambda kk_, a_: kk_ * a_, kk, a_sig)
        cum = each(cumsum_rows, ld)
        yield
        at = each(lambda kk_, cu, l_: -kk_ * jnp.exp(cu - l_), kk, cum, ld)
        rt = each(lambda r_, cu: r_ * jnp.exp(cu), r, cum)
        inv = each(lambda cu: jnp.exp(-cu), cum)
        bt = each(lambda b_, i_: b_ * i_, b, inv)
        kt = each(lambda km_, i_: km_ * i_, kmod, inv)
        yield
        to_end = each(lambda cu: jnp.exp(cu[c - 1:c, :] - cu), cum)
        bk_end = each(lambda b_, km_, te: cat([b_ * te, km_ * te], axis=0), b, kmod, to_end)
        p_end_col = each(lambda cu: jnp.sum(eye_p * jnp.exp(cu[c - 1:c, :]), axis=-1, keepdims=True), cum)
        gm = each(lambda at_, rt_, bt_, kt_: dot(cat([at_, rt_], axis=0),
                                                 cat([blockdiag(bt_), blockdiag(kt_)], axis=0), _NT),
                  at, rt, bt, kt)
        yield
        pc = pack * c
        l_ab = [jnp.where(tril_strict, g[0:c, 0:pc], 0.0) for g in gm]
        l_ak = [jnp.where(tril_strict, g[0:c, pc:2 * pc], 0.0) for g in gm]
        m_rb = [jnp.where(tril_incl, g[c:2 * c, 0:pc], 0.0) for g in gm]
        m_rk = [jnp.where(tril_incl, g[c:2 * c, pc:2 * pc], 0.0) for g in gm]
        t_inv = [eye_c + l_ for l_ in l_ab]
        pw = each(lambda q: dot(q, blockdiag(q)), l_ab)
        yield
        span = 4
        while span <= c:
            last = span == c
            both = each(lambda q, t_: dot(q, blockdiag(t_) if last else
                                          cat([blockdiag(t_), blockdiag(q)], axis=1)), pw, t_inv)
            t_inv = each(lambda t_, m_: t_ + m_[:, 0:pc], t_inv, both)
            if not last:
                pw = [m_[:, pc:2 * pc] for m_ in both]
            yield
            span *= 2
        out.update(items=items, r=r, v=v, kmod=kmod, bonus=[s_[c:2 * c] for s_ in sums], at=at, rt=rt,
                   l_ak=l_ak, m_rb=m_rb, m_rk=m_rk, t_inv=t_inv, bk_end=bk_end, p_end_col=p_end_col)

    def state_pass(d):
        items = d["items"]
        y = []
        for i0 in range(0, len(items), n_pairs):
            idx = range(i0, i0 + n_pairs)
            st = [st_scr[p] for p in range(n_pairs)]
            vb = [blockdiag(d["v"][i]) for i in idx]
            x = [dot(cat([d["at"][i], d["l_ak"][i]], axis=1), cat([st[p], vb[p]], axis=0))
                 for p, i in enumerate(idx)]
            yield
            u = [dot(d["t_inv"][i], blockdiag(x[p])) for p, i in enumerate(idx)]
            yield
            y += [dot(cat([d["rt"][i], d["m_rk"][i], d["m_rb"][i]], axis=1),
                      cat([st[p], vb[p], blockdiag(u[p])], axis=0)) for p, i in enumerate(idx)]
            upd = [dot(d["bk_end"][i], cat([u[p], d["v"][i]], axis=0), _TN) for p, i in enumerate(idx)]
            for p, i in enumerate(idx):
                st_scr[p] = d["p_end_col"][i] * st[p] + jnp.where(same_head, upd[p], 0.0)
            yield
        s2 = each(lambda y_: head_sums(cat([y_, y_ * y_], axis=0)), y)
        yield
        for (ci, p), y_, s_, bo_, v_ in zip(items, y, s2, d["bonus"], d["v"]):
            ln = slice(p * pw_lanes, (p + 1) * pw_lanes)
            mean = s_[0:c] * (1.0 / hs)
            var = s_[c:2 * c] * (1.0 / hs) - mean * mean
            yn = (y_ - mean) * lax.rsqrt(var + hs * GROUPNORM_EPS_PER_CHANNEL) * lnw_ref[:, ln] + lnb_ref[:, ln]
            y_o[ci * c:(ci + 1) * c, ln] = yn + bo_ * v_

    def run_together(*progs):
        progs = list(progs)
        while progs:
            for prog in list(progs):
                if next(prog, StopIteration) is StopIteration:
                    progs.remove(prog)

    halves = [range(h, min(h + WKV_BATCH_CHUNKS, n_chunks)) for h in range(0, n_chunks, WKV_BATCH_CHUNKS)]
    prev = None
    for chunks in halves:
        cur = {}
        if prev is None:
            run_together(prepare(chunks, cur))
        else:
            run_together(prepare(chunks, cur), state_pass(prev))
        prev = cur
    run_together(state_pass(prev))

    @pl.when(t == pl.num_programs(1) - 1)
    def _():
        for p in range(n_pairs):
            s_grp = st_scr[p].T
            for i in range(pack):
                s_o[pack * p + i] = s_grp[i * hs:(i + 1) * hs, i * hs:(i + 1) * hs]


def _mix_out_attn_seq_kernel(n_xheads, x_ref, y_ref, g_ref, yb_ref, wout_ref, gx_ref, wq_ref,
                             mk_ref, mv_ref, wo_ref, x2_o):
    mixed = jnp.concatenate([y_ref[...] * g_ref[...], yb_ref[...]], axis=-1)
    x1 = x_ref[...] + _bdot(mixed, wout_ref[...])
    q = _bdot(_rms(x1, gx_ref[...]), wq_ref[...])
    d = q.shape[1]
    xd = d // n_xheads
    scale = xd ** -0.5
    outs = []
    for h in range(n_xheads):
        sl = slice(h * xd, (h + 1) * xd)
        s = lax.dot_general(q[:, sl].astype(BF16), mk_ref[:, sl].astype(BF16),
                            (((1,), (1,)), ((), ())), preferred_element_type=F32) * scale
        outs.append(_bdot(_softmax_rows(s), mv_ref[:, sl]))
    o = jnp.concatenate(outs, axis=-1)
    x2_o[...] = x1 + _bdot(o, wo_ref[...])


def _ffn_seq_kernel(final, *refs):
    it = iter(refs)
    x_ref, gf_ref, wup_ref, fprev_ref, fcw_ref, wdown_ref = (next(it) for _ in range(6))
    gfin_ref = next(it) if final else None
    y_o, fnew_o, carry = (next(it) for _ in range(3))

    t = pl.program_id(1)
    tm = x_ref.shape[0]
    f = wdown_ref.shape[0]
    fc = FFN_COL_TILE

    @pl.when(t == 0)
    def _():
        carry[0:2, :] = fprev_ref[...]

    x = x_ref[...]
    xf = _rms(x, gf_ref[...]).astype(BF16)
    acc = x

    def up_cols(c0):
        return jnp.dot(xf, wup_ref[:, c0:c0 + fc], preferred_element_type=F32)

    def conv_cols(up, c0):
        p0 = carry[0:1, c0:c0 + fc]
        p1 = carry[1:2, c0:c0 + fc]
        up1 = _shift_rows(up, [p1])
        up2 = _shift_rows(up, [p0, p1])
        carry[0:2, c0:c0 + fc] = up[tm - 2:tm, :]
        return (fcw_ref[0:1, c0:c0 + fc] * up2 + fcw_ref[1:2, c0:c0 + fc] * up1
                + fcw_ref[2:3, c0:c0 + fc] * up)

    n_tiles = f // fc
    ahead = [(up_cols(i * fc), up_cols(f + i * fc)) for i in range(min(FFN_LOOKAHEAD, n_tiles))]
    for j in range(n_tiles):
        c0 = j * fc
        cur = ahead.pop(0)
        if j + FFN_LOOKAHEAD < n_tiles:
            ahead.append((up_cols(c0 + FFN_LOOKAHEAD * fc), up_cols(f + c0 + FFN_LOOKAHEAD * fc)))
        u = conv_cols(cur[0], c0)
        gt = conv_cols(cur[1], f + c0)
        hidden = gt * _sigmoid(gt) * u
        acc = acc + _bdot(hidden, wdown_ref[c0:c0 + fc, :])
    if final:
        acc = _rms(acc, gfin_ref[...])
    y_o[...] = acc

    @pl.when(t == pl.num_programs(1) - 1)
    def _():
        fnew_o[...] = carry[0:2, :]


def _mem_kv_kernel(m_ref, g_ref, wk_ref, wv_ref, k_o, v_o, kh_o, vh_o):
    mn = _rms(m_ref[...], g_ref[...])
    nseq, m, nh, xd = kh_o.shape
    for w_ref, flat_o, heads_o in ((wk_ref, k_o, kh_o), (wv_ref, v_o, vh_o)):
        val = _bdot(mn, w_ref[...])
        flat_o[...] = val
        for s in range(nseq):
            for h in range(nh):
                heads_o[s, :, h, :] = val[s * m:(s + 1) * m, h * xd:(h + 1) * xd]


def _mix_in_step_kernel(has_vmix, lora_dims, *refs):
    it = iter(refs)
    x_ref, gmix_ref, win_ref, sprev_ref, cprev_ref, mu_ref, w0_ref, a0_ref, wl_ref, cw_ref = (
        next(it) for _ in range(10))
    if has_vmix:
        vf_ref, v0_ref, v1_ref, v2_ref = (next(it) for _ in range(4))
    r_o, k_o, v_o, a_o, ld_o, vrow_o, g_o, yb_o, xn_o, cnew_o = (next(it) for _ in range(10))

    nb = x_ref.shape[0]
    rc = mu_ref.shape[1]
    cw = cw_ref.shape[1]
    rw = w0_ref.shape[1]

    xn = _rms(x_ref[...], gmix_ref[...])
    xn_o[...] = xn
    stacked = jnp.concatenate([xn, sprev_ref[...]], axis=0)
    proj = _bdot(stacked, win_ref[...])
    p_cur = proj[0:nb, 0:rc]
    p_shift = proj[nb:2 * nb, 0:rc]
    vmix = None
    if has_vmix:
        vmix = (vf_ref[...], v0_ref[...], v1_ref[...], v2_ref[...])
    r, k, v, a_sig, log_decay, gate = _mix_rows(
        p_cur, p_shift, mu_ref[...], w0_ref[...], a0_ref[...], wl_ref[...], lora_dims, rw, vmix)
    r_o[...] = r.T
    k_o[...] = k.T
    v_o[...] = v.T
    a_o[...] = a_sig.T
    ld_o[...] = log_decay.T
    vrow_o[...] = v
    g_o[...] = gate

    gate_b = proj[0:nb, rc:rc + cw]
    gate_c = proj[0:nb, rc + cw:rc + 2 * cw]
    h_in = proj[0:nb, rc + 2 * cw:rc + 3 * cw]
    u = gate_c * h_in
    u2 = cprev_ref[:, 0, :]
    u1 = cprev_ref[:, 1, :]
    yb_o[...] = gate_b * (cw_ref[0:1, :] * u2 + cw_ref[1:2, :] * u1 + cw_ref[2:3, :] * u)
    cnew_o[:, 0, :] = u1
    cnew_o[:, 1, :] = u


def _wkv_step_kernel(aliased, *refs):
    r_ref, k_ref, v_ref, a_ref, ld_ref, kkw_ref, kaw_ref, rkw_ref, lnw_ref, lnb_ref, s_ref = refs[:11]
    y_o, s_o, y_scr = refs[12:] if aliased else refs[11:]
    n = s_ref.shape[0]

    @pl.when(pl.program_id(0) == 0)
    def _():
        r, k, v, a_sig, ld = r_ref[...], k_ref[...], v_ref[...], a_ref[...], ld_ref[...]
        kk = k * kkw_ref[...]
        kk = kk / jnp.maximum(jnp.sqrt(jnp.sum(kk * kk, axis=0, keepdims=True)), KK_NORM_FLOOR)
        kmod = k * (1.0 + (a_sig - 1.0) * kaw_ref[...])
        a = -kk
        b = kk * a_sig
        decay = jnp.exp(ld)
        for vi in range(n):
            s = s_ref[vi]
            sa = jnp.sum(s * a, axis=0, keepdims=True)
            s_new = s * decay + sa * b + v[vi:vi + 1, :] * kmod
            s_o[vi] = s_new
            y_scr[vi:vi + 1, :] = jnp.sum(s_new * r, axis=0, keepdims=True)
        y = y_scr[...]
        mean = jnp.mean(y, axis=0, keepdims=True)
        yc = y - mean
        var = jnp.mean(yc * yc, axis=0, keepdims=True)
        y = yc * lax.rsqrt(var + n * GROUPNORM_EPS_PER_CHANNEL) * lnw_ref[...] + lnb_ref[...]
        y_o[...] = y + jnp.sum(r * kmod * rkw_ref[...], axis=0, keepdims=True) * v

    @pl.when(pl.program_id(0) > 0)
    def _():
        s_o[...] = jnp.zeros(s_o.shape, F32)


def _mix_out_q_step_kernel(x_ref, ya_ref, g_ref, yb_ref, wout_ref, gx_ref, wq_ref, x1_o, q_o):
    mixed = jnp.concatenate([ya_ref[...].T * g_ref[...], yb_ref[...]], axis=-1)
    x1 = x_ref[...] + _bdot(mixed, wout_ref[...])
    x1_o[...] = x1
    q_o[...] = _bdot(_rms(x1, gx_ref[...]), wq_ref[...])


def _attn_step_kernel(q_ref, mk_ref, mv_ref, o_o):
    bt = q_ref.shape[0]
    m, nh, xd = mk_ref.shape[1:]
    scale = xd ** -0.5
    fold = SUBLANES // nh if SUBLANES % nh == 0 and m % max(SUBLANES // nh, 1) == 0 else 1

    def folded(x):
        return [x[j * nh:(j + 1) * nh] for j in range(fold)]

    def body(i, carry):
        q = jnp.concatenate([q_ref[i]] * fold, axis=0)
        k = mk_ref[i].reshape(m // fold, fold * nh, xd)
        v = mv_ref[i].reshape(m // fold, fold * nh, xd)
        s = jnp.sum(k * q, axis=-1, keepdims=True) * scale
        mx = functools.reduce(jnp.maximum, folded(jnp.max(s, axis=0)))
        e = jnp.exp(s - jnp.concatenate([mx] * fold, axis=0))
        den = sum(folded(jnp.sum(e, axis=0)))
        acc = sum(folded(jnp.sum(e * v, axis=0)))
        o_o[i] = acc / den
        return carry

    lax.fori_loop(0, bt, body, 0)


def _ffn_step_kernel(final, aliased, *refs):
    it = iter(refs)
    x1_ref, o_ref, wo_ref, gf_ref, wup_ref, fprev_ref, fcw_ref, wdown_ref = (next(it) for _ in range(8))
    gfin_ref = next(it) if final else None
    if aliased:
        next(it)
    y_o, fnew_o = (next(it) for _ in range(2))
    f = wdown_ref.shape[0]
    f2 = 2 * f

    @pl.when(pl.program_id(0) == 0)
    def _():
        x2 = x1_ref[...] + _bdot(o_ref[...], wo_ref[...])
        up = _bdot(_rms(x2, gf_ref[...]), wup_ref[...])
        up2 = fprev_ref[:, 0, :]
        up1 = fprev_ref[:, 1, :]
        upc = fcw_ref[0:1, :] * up2 + fcw_ref[1:2, :] * up1 + fcw_ref[2:3, :] * up
        fnew_o[:, 0, :] = up1
        fnew_o[:, 1, :] = up
        u = upc[:, 0:f]
        gt = upc[:, f:f2]
        x3 = x2 + _bdot(gt * _sigmoid(gt) * u, wdown_ref[...])
        if final:
            x3 = _rms(x3, gfin_ref[...])
        y_o[...] = x3

    @pl.when(pl.program_id(0) > 0)
    def _():
        fnew_o[...] = jnp.zeros(fnew_o.shape, F32)


def _params(*sem):
    return pltpu.CompilerParams(dimension_semantics=sem, vmem_limit_bytes=V7X_VMEM_LIMIT_BYTES)


class _Layered(NamedTuple):
    stacked: jax.Array
    layer: int

    @property
    def shape(self):
        return self.stacked.shape[1:]


def _operand(x):
    return x.stacked if isinstance(x, _Layered) else x


def _whole(x):
    if isinstance(x, _Layered):
        nd, l = x.stacked.ndim, x.layer
        return pl.BlockSpec((None,) + x.shape, lambda *_: (l,) + (0,) * (nd - 1), pipeline_mode=pl.Buffered(1))
    nd = x.ndim
    return pl.BlockSpec(x.shape, lambda *_: (0,) * nd, pipeline_mode=pl.Buffered(1))


def _sds(shape):
    return jax.ShapeDtypeStruct(shape, F32)


def _stacked_weights(p):
    depth = p["w_in"].shape[0]
    rows = lambda a: a.reshape(a.shape[0], 1, -1)
    rw = p["w0"].shape[1]
    d_decay, d_aaa, d_gate = p["w2"].shape[1], p["a2"].shape[1], p["g2"].shape[1]
    pad_cols = lambda a, before, after: jnp.pad(a, ((0, 0), (0, 0), (before, after)))
    w_lora = jnp.concatenate([pad_cols(p["w2"], 0, 2 * rw), pad_cols(p["a2"], rw, rw), pad_cols(p["g2"], 2 * rw, 0)],
                             axis=1)
    lanes = 128
    mv = p["v1"].shape[2]
    mvp = -(-mv // lanes) * lanes
    w = dict(
        norm_mix=rows(p["norm_mix"]), w_in=p["w_in"].astype(BF16), mu=rows(p["mu_shift"]), w0=rows(p["w0"]),
        a0=rows(p["a0"]), w_lora=w_lora.astype(BF16), conv_w=p["conv_w"], k_k=p["k_k"], k_a=p["k_a"],
        r_k=p["r_k"], ln_w=p["ln_x_w"], ln_b=p["ln_x_b"], w_out=p["w_out"].astype(BF16), norm_x=rows(p["norm_x"]),
        wq=p["wq"].astype(BF16), wo=p["wo"].astype(BF16), norm_ffn=rows(p["norm_ffn"]),
        w_up=p["w_up"].astype(BF16), ffn_conv_w=p["ffn_conv_w"], w_down=p["w_down"].astype(BF16),
        norm_mem=rows(p["norm_mem"]), wk=p["wk"].astype(BF16), wv=p["wv"].astype(BF16),
    )
    layers = []
    for l in range(depth):
        wl = {k_: _Layered(v_, l) for k_, v_ in w.items()}
        wl["norm_final"] = p["norm_final"].reshape(1, -1)
        wl["lora_dims"] = (d_decay, d_aaa, d_gate)
        if l > 0:
            wl["v0"] = _Layered(rows(p["v0"]), l - 1)
            wl["v1"] = _Layered(jnp.pad(p["v1"], ((0, 0), (0, 0), (0, mvp - mv))).astype(BF16), l - 1)
            wl["v2"] = _Layered(jnp.pad(p["v2"], ((0, 0), (0, mvp - mv), (0, 0))).astype(BF16), l - 1)
        layers.append(wl)
    return layers


def _head_rows(x, n_heads):
    st = x.stacked
    return _Layered(st.reshape(st.shape[0], n_heads, st.shape[1] // n_heads, 1), x.layer)


def _mem_kv(mem, w, n_xheads):
    bsz, m, d = mem.shape
    depth = w["wk"].stacked.shape[0]
    nseq = max(1, min(bsz, 512 // m))
    xd = d // n_xheads
    stacked = lambda a: pl.BlockSpec((None,) + a.shape, lambda l, i: (l,) + (0,) * len(a.shape))
    flat = pl.BlockSpec((None, nseq * m, d), lambda l, i: (l, i, 0))
    heads = pl.BlockSpec((None, nseq, m, n_xheads, xd), lambda l, i: (l, i, 0, 0, 0))
    args = [mem.reshape(bsz * m, d), w["norm_mem"], w["wk"], w["wv"]]
    k, v, kh, vh = pl.pallas_call(
        _mem_kv_kernel,
        grid=(depth, bsz // nseq),
        in_specs=[pl.BlockSpec((nseq * m, d), lambda l, i: (i, 0))] + [stacked(a) for a in args[1:]],
        out_specs=[flat, flat, heads, heads],
        out_shape=[_sds((depth, bsz * m, d))] * 2 + [_sds((depth, bsz, m, n_xheads, xd))] * 2,
        compiler_params=_params("arbitrary", "arbitrary"),
        name="mem_kv",
    )(*map(_operand, args))
    return k.reshape(depth, bsz, m, d), v.reshape(depth, bsz, m, d), kh, vh


def _prompt_layer(l, x, mem_k, mem_v, shift0, wkv0, conv0, ffn0, v_first, w, n_heads, n_xheads, final):
    shift0, wkv0, conv0, ffn0 = (a[l] for a in (shift0, wkv0, conv0, ffn0))
    bsz, t, d = x.shape
    rw = w["w0"].shape[1]
    hs = rw // n_heads
    cw = w["conv_w"].shape[1]
    rc = w["mu"].shape[1]
    tm = min(ROW_TILE, t)
    nt = t // tm
    has_vmix = v_first is not None

    tile = lambda n: pl.BlockSpec((None, tm, n), lambda b, i: (b, i, 0))
    per_b = lambda s: pl.BlockSpec((None,) + s, lambda b, i: (b,) + (0,) * len(s))

    ins = [x, w["norm_mix"], w["w_in"], shift0[:, None, :], conv0, w["mu"], w["w0"], w["a0"], w["w_lora"],
           w["conv_w"]]
    specs = [tile(d), _whole(w["norm_mix"]), _whole(w["w_in"]), per_b((1, d)), per_b((2, cw)),
             _whole(w["mu"]), _whole(w["w0"]), _whole(w["a0"]), _whole(w["w_lora"]), _whole(w["conv_w"])]
    if has_vmix:
        ins += [v_first, w["v0"], w["v1"], w["v2"]]
        specs += [tile(rw), _whole(w["v0"]), _whole(w["v1"]), _whole(w["v2"])]
    out_shape = [_sds((bsz, t, rw))] * 6 + [_sds((bsz, t, cw)), _sds((bsz, 1, d)), _sds((bsz, 2, cw))]
    out_specs = [tile(rw)] * 6 + [tile(cw), per_b((1, d)), per_b((2, cw))]
    r, k, v, a_sig, ld, gate, y_b, xn_last, conv_new = pl.pallas_call(
        functools.partial(_mix_in_seq_kernel, has_vmix, w["lora_dims"]),
        grid=(bsz, nt), in_specs=specs, out_specs=out_specs, out_shape=out_shape,
        scratch_shapes=[pltpu.VMEM((8, rc), F32), pltpu.VMEM((8, cw), F32)],
        compiler_params=_params("parallel", "arbitrary"),
        name="mix_in_seq",
    )(*map(_operand, ins))
    if not has_vmix:
        v_first = v

    head_params = [_Layered(a.stacked[:, None, :], a.layer) for a in
                   (w["k_k"], w["k_a"], w["r_k"], w["ln_w"], w["ln_b"])]
    sblk = per_b((n_heads, hs, hs))
    wt = min(WKV_ROW_TILE, t)
    wtile = pl.BlockSpec((None, wt, rw), lambda b, i: (b, i, 0))
    y_a, wkv_new = pl.pallas_call(
        functools.partial(_wkv_seq_kernel, hs),
        grid=(bsz, t // wt),
        in_specs=[wtile] * 5 + [_whole(a) for a in head_params] + [sblk],
        out_specs=[wtile, sblk],
        out_shape=[_sds((bsz, t, rw)), _sds((bsz, n_heads, hs, hs))],
        scratch_shapes=[pltpu.VMEM((rw // (WKV_HEAD_PACK * hs), WKV_HEAD_PACK * hs, WKV_HEAD_PACK * hs), F32)],
        compiler_params=_params("parallel", "arbitrary"),
        name="wkv_seq",
    )(r, k, v, a_sig, ld, *map(_operand, head_params), wkv0)

    mblk = pl.BlockSpec((None, None) + mem_k.shape[2:], lambda b, i: (l, b, 0, 0))
    x2 = pl.pallas_call(
        functools.partial(_mix_out_attn_seq_kernel, n_xheads),
        grid=(bsz, nt),
        in_specs=[tile(d), tile(rw), tile(rw), tile(cw), _whole(w["w_out"]), _whole(w["norm_x"]), _whole(w["wq"]),
                  mblk, mblk, _whole(w["wo"])],
        out_specs=tile(d), out_shape=_sds((bsz, t, d)),
        compiler_params=_params("parallel", "arbitrary"),
        name="mix_out_attn_seq",
    )(*map(_operand, (x, y_a, gate, y_b, w["w_out"], w["norm_x"], w["wq"], mem_k, mem_v, w["wo"])))

    f2 = w["w_up"].shape[1]
    ins = [x2, w["norm_ffn"], w["w_up"], ffn0, w["ffn_conv_w"], w["w_down"]]
    ft = min(FFN_ROW_TILE, t)
    ftile = pl.BlockSpec((None, ft, d), lambda b, i: (b, i, 0))
    specs = [ftile, _whole(w["norm_ffn"]), _whole(w["w_up"]), per_b((2, f2)), _whole(w["ffn_conv_w"]),
             _whole(w["w_down"])]
    if final:
        ins.append(w["norm_final"])
        specs.append(_whole(w["norm_final"]))
    x3, ffn_new = pl.pallas_call(
        functools.partial(_ffn_seq_kernel, final),
        grid=(bsz, t // ft), in_specs=specs,
        out_specs=[ftile, per_b((2, f2))], out_shape=[_sds((bsz, t, d)), _sds((bsz, 2, f2))],
        scratch_shapes=[pltpu.VMEM((8, f2), F32)],
        compiler_params=_params("parallel", "arbitrary"),
        name="ffn_seq",
    )(*map(_operand, ins))
    return x3, xn_last[:, 0, :], wkv_new, conv_new, ffn_new, v_first


def _sample_layer(l, x, mem_k, mem_v, shift0, wkv_t, conv0, ffn0, carried, w, n_heads, n_xheads, final):
    v_first, wkv_all, ffn_all = carried
    nb, d = x.shape
    rw = w["w0"].shape[1]
    hs = rw // n_heads
    cw = w["conv_w"].shape[1]
    has_vmix = v_first is not None
    one = lambda a: pl.BlockSpec(a.shape, lambda i: (0,) * a.ndim, pipeline_mode=pl.Buffered(1))
    layer_of = lambda a: pl.BlockSpec((None,) + a.shape[1:], lambda i: (l,) + (0,) * (a.ndim - 1),
                                      pipeline_mode=pl.Buffered(1))

    ins = [x, w["norm_mix"], w["w_in"], shift0, conv0, w["mu"], w["w0"], w["a0"], w["w_lora"], w["conv_w"]]
    specs = [one(x), _whole(w["norm_mix"]), _whole(w["w_in"]), layer_of(shift0), layer_of(conv0)] + [
        _whole(a) for a in ins[5:]]
    if has_vmix:
        ins += [v_first, w["v0"], w["v1"], w["v2"]]
        specs += [one(v_first), _whole(w["v0"]), _whole(w["v1"]), _whole(w["v2"])]
    out_shape = [_sds((rw, nb))] * 5 + [_sds((nb, rw))] * 2 + [_sds((nb, cw)), _sds((nb, d)), _sds((nb, 2, cw))]
    r, k, v, a_sig, ld, v_rows, gate, y_b, xn, conv_new = pl.pallas_call(
        functools.partial(_mix_in_step_kernel, has_vmix, w["lora_dims"]),
        grid=(1,), in_specs=specs, out_specs=[one(s_) for s_ in out_shape], out_shape=out_shape,
        compiler_params=_params("arbitrary"),
        name="mix_in_step",
    )(*map(_operand, ins))
    if not has_vmix:
        v_first = v_rows

    aliased = wkv_all is not None
    n_clear = 0 if aliased else wkv_t.shape[0] - 1 - l
    last = n_heads - 1
    hsel = lambda c, h: jnp.where(c == 0, h, last)
    heads = lambda a: a.reshape(n_heads, hs, nb)
    hblk = pl.BlockSpec((None, hs, nb), lambda c, h: (hsel(c, h), 0, 0))
    head_params = [_head_rows(w[n_], n_heads) for n_ in ("k_k", "k_a", "r_k", "ln_w", "ln_b")]
    pblk = pl.BlockSpec((None, None, hs, 1), lambda c, h: (l, hsel(c, h), 0, 0))
    sblk = pl.BlockSpec((None, None, hs, hs, nb), lambda c, h: (l, hsel(c, h), 0, 0, 0))
    soblk = pl.BlockSpec((None, None, hs, hs, nb), lambda c, h: (l + c, h, 0, 0, 0))
    ins = [heads(r), heads(k), heads(v), heads(a_sig), heads(ld)] + [a.stacked for a in head_params] + [wkv_t]
    specs = [hblk] * 5 + [pblk] * 5 + [sblk]
    if aliased:
        ins.append(wkv_all)
        specs.append(pl.BlockSpec(memory_space=pl.ANY))
    y_a, wkv_all = pl.pallas_call(
        functools.partial(_wkv_step_kernel, aliased),
        grid=(1 + n_clear, n_heads),
        in_specs=specs, out_specs=[hblk, soblk],
        out_shape=[_sds((n_heads, hs, nb)), _sds(wkv_t.shape)],
        scratch_shapes=[pltpu.VMEM((hs, nb), F32)],
        input_output_aliases={len(ins) - 1: 1} if aliased else {},
        compiler_params=_params("arbitrary", "arbitrary"),
        name="wkv_step",
    )(*ins)

    ins = [x, y_a.reshape(rw, nb), gate, y_b, w["w_out"], w["norm_x"], w["wq"]]
    x1, q = pl.pallas_call(
        _mix_out_q_step_kernel,
        grid=(1,), in_specs=[one(a) for a in ins[:4]] + [_whole(a) for a in ins[4:]],
        out_specs=[one(x)] * 2, out_shape=[_sds((nb, d))] * 2,
        compiler_params=_params("arbitrary"),
        name="mix_out_q_step",
    )(*map(_operand, ins))

    m, xd = mem_k.shape[2], mem_k.shape[4]
    ab = min(ATTN_BATCH_TILE, nb)
    qblk = pl.BlockSpec((ab, n_xheads, xd), lambda i: (i, 0, 0))
    mblk = pl.BlockSpec((None, ab, m, n_xheads, xd), lambda i: (l, i, 0, 0, 0))
    o = pl.pallas_call(
        _attn_step_kernel,
        grid=(nb // ab,),
        in_specs=[qblk, mblk, mblk], out_specs=qblk, out_shape=_sds((nb, n_xheads, xd)),
        compiler_params=_params("parallel"),
        name="attn_step",
    )(q.reshape(nb, n_xheads, xd), mem_k, mem_v)
    o = o.reshape(nb, d)

    ins = [x1, o, w["wo"], w["norm_ffn"], w["w_up"], ffn0, w["ffn_conv_w"], w["w_down"]]
    fblk = pl.BlockSpec((None,) + ffn0.shape[1:], lambda c: (l, 0, 0, 0))
    foblk = pl.BlockSpec((None,) + ffn0.shape[1:], lambda c: (l + c, 0, 0, 0))
    specs = [one(x1), one(o), _whole(w["wo"]), _whole(w["norm_ffn"]), _whole(w["w_up"]), fblk,
             _whole(w["ffn_conv_w"]), _whole(w["w_down"])]
    if final:
        ins.append(w["norm_final"])
        specs.append(_whole(w["norm_final"]))
    ffn_aliased = ffn_all is not None
    if ffn_aliased:
        ins.append(ffn_all)
        specs.append(pl.BlockSpec(memory_space=pl.ANY))
    x3, ffn_all = pl.pallas_call(
        functools.partial(_ffn_step_kernel, final, ffn_aliased),
        grid=(1 if ffn_aliased else ffn0.shape[0] - l,), in_specs=specs, out_specs=[one(x), foblk],
        out_shape=[_sds((nb, d)), _sds(ffn0.shape)],
        input_output_aliases={len(ins) - 1: 1} if ffn_aliased else {},
        compiler_params=_params("arbitrary"),
        name="ffn_step",
    )(*map(_operand, ins))
    return x3, xn, conv_new, (v_first, wkv_all, ffn_all)


def kernel(x_prompt, x_sample, mem_prompt, state_shift, state_wkv, state_conv, state_ffn, cache_mem_k, cache_mem_v, norm_mix, w_in, mu_shift, w0, w2, a0, a2, g2, v0, v1, v2, k_k, k_a, r_k, ln_x_w, ln_x_b, conv_w, w_out, norm_x, norm_mem, wq, wk, wv, wo, norm_ffn, w_up, ffn_conv_w, w_down, norm_final):
    p = dict(norm_mix=norm_mix, w_in=w_in, mu_shift=mu_shift, w0=w0, w2=w2, a0=a0, a2=a2, g2=g2, v0=v0, v1=v1,
             v2=v2, k_k=k_k, k_a=k_a, r_k=r_k, ln_x_w=ln_x_w, ln_x_b=ln_x_b, conv_w=conv_w, w_out=w_out,
             norm_x=norm_x, norm_mem=norm_mem, wq=wq, wk=wk, wv=wv, wo=wo, norm_ffn=norm_ffn, w_up=w_up,
             ffn_conv_w=ffn_conv_w, w_down=w_down, norm_final=norm_final)
    depth = w_in.shape[0]
    n_heads = state_wkv.shape[2]
    hs = state_wkv.shape[3]
    n_xheads, xd = cache_mem_k.shape[3], cache_mem_k.shape[4]
    assert x_sample.shape[1] == 1, "the sample group advances one token per sequence"
    weights = _stacked_weights(p)

    bp, tp, d = x_prompt.shape
    m = mem_prompt.shape[1]
    mks, mvs, mem_k_p, mem_v_p = _mem_kv(mem_prompt, weights[0], n_xheads)
    cw = conv_w.shape[2]
    f2 = w_up.shape[2]
    zeros = lambda *s_: jnp.zeros((depth, bp) + s_, F32)
    x, v_first = x_prompt, None
    shs, wks, cvs, ffs = [], [], [], []
    for l in range(depth):
        x, sh, s_, cs, fs, v_first = _prompt_layer(
            l, x, mks, mvs, zeros(d), zeros(n_heads, hs, hs), zeros(state_conv.shape[2], cw),
            zeros(state_ffn.shape[2], f2), v_first, weights[l], n_heads, n_xheads, l == depth - 1)
        shs.append(sh)
        wks.append(s_)
        cvs.append(cs)
        ffs.append(fs)
    y_p = x

    nb = x_sample.shape[0]
    wkv_t = jnp.transpose(state_wkv, (0, 2, 3, 4, 1))
    x, carried = x_sample.reshape(nb, d), (None, None, None)
    xns, cns = [], []
    for l in range(depth):
        x, xn, cn, carried = _sample_layer(l, x, cache_mem_k, cache_mem_v, state_shift, wkv_t, state_conv,
                                           state_ffn, carried, weights[l], n_heads, n_xheads, l == depth - 1)
        xns.append(xn)
        cns.append(cn)
    _, wkv_all, ffn_s = carried
    wkv_s = jnp.transpose(wkv_all, (0, 4, 1, 2, 3))
    return (y_p, x.reshape(x_sample.shape), jnp.stack(shs), jnp.stack(wks), jnp.stack(cvs), jnp.stack(ffs),
            mem_k_p, mem_v_p, jnp.stack(xns), wkv_s, jnp.stack(cns), ffn_s)
```

```python
import functools
from typing import NamedTuple

import jax
import jax.numpy as jnp
from jax import lax
from jax.experimental import pallas as pl
from jax.experimental.pallas import tpu as pltpu

F32 = jnp.float32
BF16 = jnp.bfloat16
HIGHEST = lax.Precision.HIGHEST

RMS_EPS = 1e-6
GROUPNORM_EPS_PER_CHANNEL = 1e-5
KK_NORM_FLOOR = 1e-12

ROW_TILE = 512
WKV_ROW_TILE = 1024
FFN_ROW_TILE = 512
FFN_SUB_ROWS = 256
MIX_SUB_ROWS = 256
CHUNK = 64
WKV_HEAD_PACK = 2
WKV_BATCH_CHUNKS = 4
ATTN_BATCH_TILE = 4
WKV_BATCH_TILE = 8
FFN_COL_TILE = 256
FFN_LOOKAHEAD = 4
V7X_VMEM_LIMIT_BYTES = 56 * 1024 * 1024
SUBLANES = 8


def _rms(x, g):
    ms = jnp.mean(x * x, axis=-1, keepdims=True)
    return x * lax.rsqrt(ms + RMS_EPS) * g


def _bdot(a, b):
    return jnp.dot(a.astype(BF16), b.astype(BF16), preferred_element_type=F32)


def _hdot(a, b, dims=(((1,), (0,)), ((), ()))):
    return lax.dot_general(a, b, dims, precision=HIGHEST, preferred_element_type=F32)


def _sigmoid(x):
    return 1.0 / (1.0 + jnp.exp(-x))


def _softplus(x):
    return jnp.maximum(x, 0.0) + jnp.log1p(jnp.exp(-jnp.abs(x)))


def _shift_rows(x, prev_rows):
    s = len(prev_rows)
    rolled = pltpu.roll(x, s, axis=0)
    top = rolled[0:SUBLANES, :]
    row = lax.broadcasted_iota(jnp.int32, top.shape, 0)
    for i, pr in enumerate(prev_rows):
        top = jnp.where(row == i, pr, top)
    return jnp.concatenate([top, rolled[SUBLANES:, :]], axis=0)


def _mix_rows_stages(p_cur, p_shift, mu, w0, a0, w_lora, lora_dims, rw, vmix, out):
    d_decay, d_aaa, _ = lora_dims
    z = p_cur + (p_shift - p_cur) * mu
    r = z[:, 0:rw]
    k = z[:, rw:2 * rw]
    v = z[:, 2 * rw:3 * rw]
    lo = z[:, 3 * rw:]
    lane = lax.broadcasted_iota(jnp.int32, lo.shape, 1)
    feat = jnp.where(lane < d_decay, jnp.tanh(lo),
                     jnp.where(lane < d_decay + d_aaa, lo, _sigmoid(lo)))
    yield
    lora = _bdot(feat, w_lora)
    yield
    w = -_softplus(-(w0 + lora[:, 0:rw])) - 0.5
    log_decay = -jnp.exp(w)
    a_sig = _sigmoid(a0 + lora[:, rw:2 * rw])
    gate = lora[:, 2 * rw:3 * rw]
    if vmix is not None:
        v_first, v0, v1, v2 = vmix
        t = _bdot(v, v1)
        yield
        pre = _bdot(t, v2)
        yield
        v = v + (v_first - v) * _sigmoid(v0 + pre)
    out.extend([r, k, v, a_sig, log_decay, gate])


def _mix_rows(*args):
    out = []
    for _ in _mix_rows_stages(*args, out):
        pass
    return out


def _run_together(*progs):
    progs = list(progs)
    while progs:
        for prog in list(progs):
            if next(prog, StopIteration) is StopIteration:
                progs.remove(prog)


def _group_norm_bonus(y, r, kmod, v, rk, lnw, lnb, n):
    mean = jnp.mean(y, axis=-1, keepdims=True)
    yc = y - mean
    var = jnp.mean(yc * yc, axis=-1, keepdims=True)
    y = yc * lax.rsqrt(var + n * GROUPNORM_EPS_PER_CHANNEL) * lnw + lnb
    return y + jnp.sum(r * kmod * rk, axis=-1, keepdims=True) * v


def _key_features(k, a_sig, kk_w, ka_w):
    kk = k * kk_w
    nrm = jnp.sqrt(jnp.sum(kk * kk, axis=-1, keepdims=True))
    kk = kk / jnp.maximum(nrm, KK_NORM_FLOOR)
    kmod = k * (1.0 + (a_sig - 1.0) * ka_w)
    return kk, kmod


def _softmax_rows(s):
    m = jnp.max(s, axis=-1, keepdims=True)
    e = jnp.exp(s - m)
    return e / jnp.sum(e, axis=-1, keepdims=True)


def _mix_in_seq_kernel(has_vmix, lora_dims, *refs):
    it = iter(refs)
    x_ref, gmix_ref, win_ref, sprev_ref, cprev_ref, mu_ref, w0_ref, a0_ref, wl_ref, cw_ref = (
        next(it) for _ in range(10))
    if has_vmix:
        vf_ref, v0_ref, v1_ref, v2_ref = (next(it) for _ in range(4))
    r_o, k_o, v_o, a_o, ld_o, g_o, yb_o, xn_o, cnew_o, pc_scr, uc_scr = (next(it) for _ in range(11))

    t = pl.program_id(1)
    tm = x_ref.shape[0]
    rc = mu_ref.shape[1]
    cw = cw_ref.shape[1]
    rw = w0_ref.shape[1]

    @pl.when(t == 0)
    def _():
        sprev = jnp.broadcast_to(sprev_ref[...], (8, sprev_ref.shape[1]))
        pc_scr[...] = _bdot(sprev, win_ref[:, 0:rc])
        uc_scr[0:2, :] = cprev_ref[...]

    n_sub = max(1, tm // MIX_SUB_ROWS)
    sub = tm // n_sub
    col_chunks = [(0, 2 * rw), (2 * rw, rc), (rc, rc + cw), (rc + cw, rc + 3 * cw)]
    xns = [_rms(x_ref[i * sub:(i + 1) * sub, :], gmix_ref[...]) for i in range(n_sub)]
    carry = dict(p=pc_scr[0:1, :], u=[uc_scr[0:1, :], uc_scr[1:2, :]])

    def project(i, out):
        xb = xns[i].astype(BF16)
        for c0, c1 in col_chunks:
            out.append(jnp.dot(xb, win_ref[:, c0:c1], preferred_element_type=F32))
            yield

    def features(i, parts):
        rows = slice(i * sub, (i + 1) * sub)
        p_cur = jnp.concatenate(parts[0:2], axis=1)
        p_shift = _shift_rows(p_cur, [carry["p"]])
        vmix = None
        if has_vmix:
            vmix = (vf_ref[rows, :], v0_ref[...], v1_ref[...], v2_ref[...])
        mixed = []
        yield from _mix_rows_stages(p_cur, p_shift, mu_ref[...], w0_ref[...], a0_ref[...], wl_ref[...],
                                    lora_dims, rw, vmix, mixed)
        for ref, val in zip((r_o, k_o, v_o, a_o, ld_o, g_o), mixed):
            ref[rows, :] = val
        yield
        gate_b = parts[2]
        u = parts[3][:, 0:cw] * parts[3][:, cw:2 * cw]
        u1 = _shift_rows(u, [carry["u"][1]])
        u2 = _shift_rows(u, carry["u"])
        yb_o[rows, :] = gate_b * (cw_ref[0:1, :] * u2 + cw_ref[1:2, :] * u1 + cw_ref[2:3, :] * u)
        carry["p"] = p_cur[sub - 1:sub, :]
        carry["u"] = [u[sub - 2:sub - 1, :], u[sub - 1:sub, :]]

    parts = []
    _run_together(project(0, parts))
    for i in range(n_sub):
        nxt = []
        if i + 1 < n_sub:
            _run_together(features(i, parts), project(i + 1, nxt))
        else:
            _run_together(features(i, parts))
        parts = nxt

    pc_scr[0:1, :] = carry["p"]
    uc_scr[0:1, :] = carry["u"][0]
    uc_scr[1:2, :] = carry["u"][1]

    @pl.when(t == pl.num_programs(1) - 1)
    def _():
        xn_o[...] = xns[-1][sub - 1:sub, :]
        cnew_o[0:1, :] = carry["u"][0]
        cnew_o[1:2, :] = carry["u"][1]


def _split3(x):
    hi = x.astype(BF16)
    r1 = x - hi.astype(F32)
    mid = r1.astype(BF16)
    lo = (r1 - mid.astype(F32)).astype(BF16)
    return hi, mid, lo


def _dot3(a, b, dims=(((1,), (0,)), ((), ()))):
    a_hi = a.astype(BF16)
    a_lo = (a - a_hi.astype(F32)).astype(BF16)
    b_hi = b.astype(BF16)
    b_lo = (b - b_hi.astype(F32)).astype(BF16)
    d = functools.partial(lax.dot_general, dimension_numbers=dims, preferred_element_type=F32)
    return d(a_hi, b_hi) + d(a_lo, b_hi) + d(a_hi, b_lo)


def _dot1(a, b, dims=(((1,), (0,)), ((), ()))):
    return lax.dot_general(a.astype(BF16), b.astype(BF16), dims, preferred_element_type=F32)


_NT = (((1,), (1,)), ((), ()))
_TN = (((0,), (0,)), ((), ()))
_WKV_DOT = _dot1


def _wkv_seq_kernel(hs, r_ref, k_ref, v_ref, a_ref, ld_ref, kkw_ref, kaw_ref, rkw_ref, lnw_ref, lnb_ref,
                    s0_ref, y_o, s_o, st_scr):
    t = pl.program_id(1)
    tt, rw = r_ref.shape
    c = CHUNK
    pack = WKV_HEAD_PACK
    pw_lanes = pack * hs
    n_pairs = rw // pw_lanes
    assert c == hs, "chunk length equals the head size in the packed layout"
    shift = hs.bit_length() - 1
    assert 1 << shift == hs

    lane_head = lax.broadcasted_iota(jnp.int32, (1, pw_lanes), 1) >> shift
    row_p = lax.broadcasted_iota(jnp.int32, (pw_lanes, pw_lanes), 0)
    col_p = lax.broadcasted_iota(jnp.int32, (pw_lanes, pw_lanes), 1)
    same_head = (row_p >> shift) == (col_p >> shift)
    head_ones = jnp.where(same_head, 1.0, 0.0).astype(BF16)
    eye_p = (row_p == col_p).astype(F32)
    row_c = lax.broadcasted_iota(jnp.int32, (c, pw_lanes), 0)
    col_c = lax.broadcasted_iota(jnp.int32, (c, pw_lanes), 1) & (hs - 1)
    tril_strict = row_c > col_c
    tril_incl = row_c >= col_c
    eye_c = (row_c == col_c).astype(F32)
    row_t = lax.broadcasted_iota(jnp.int32, (c, 3 * c), 0)
    col_t = lax.broadcasted_iota(jnp.int32, (c, 3 * c), 1)
    col_t = jnp.where(col_t < c, col_t, jnp.where(col_t < 2 * c, col_t - c, col_t - 2 * c))
    tril3 = jnp.where(row_t >= col_t, 1.0, 0.0).astype(BF16)
    ones2 = jnp.concatenate([head_ones] * 2, axis=0)

    def blockdiag(x):
        return jnp.concatenate([jnp.where(lane_head == i, x, 0.0) for i in range(pack)], axis=0)

    def head_sums(x):
        hi = x.astype(BF16)
        lo = (x - hi.astype(F32)).astype(BF16)
        return jnp.dot(jnp.concatenate([hi, lo], axis=1), ones2, preferred_element_type=F32)

    @pl.when(t == 0)
    def _():
        z = jnp.zeros((hs, hs), F32)
        for p in range(n_pairs):
            blk = jnp.concatenate(
                [jnp.concatenate([s0_ref[pack * p + i] if j == i else z for j in range(pack)], axis=1)
                 for i in range(pack)], axis=0)
            st_scr[p] = blk.T

    n_chunks = tt // c
    cat = jnp.concatenate
    dot = _WKV_DOT

    def each(f, *lists):
        return [f(*xs) for xs in zip(*lists)]

    def cumsum_rows(x):
        hi, mid, lo = _split3(x)
        return jnp.dot(tril3, cat([hi, mid, lo], axis=0), preferred_element_type=F32)

    def prepare(chunks, out):
        items = [(ci, p) for ci in chunks for p in range(n_pairs)]
        tiles = lambda ref: [ref[ci * c:(ci + 1) * c, p * pw_lanes:(p + 1) * pw_lanes] for ci, p in items]
        params = lambda ref: [ref[:, p * pw_lanes:(p + 1) * pw_lanes] for _, p in items]
        r, k, v, a_sig, ld = (tiles(ref) for ref in (r_ref, k_ref, v_ref, a_ref, ld_ref))
        kkw, kaw, rkw = (params(ref) for ref in (kkw_ref, kaw_ref, rkw_ref))
        kk = each(lambda k_, w_: k_ * w_, k, kkw)
        kmod = each(lambda k_, a_, w_: k_ * (1.0 + (a_ - 1.0) * w_), k, a_sig, kaw)
        sums = each(lambda kk_, r_, km_, w_: head_sums(cat([kk_ * kk_, r_ * km_ * w_], axis=0)), kk, r, kmod, rkw)
        yield
        kk = each(lambda kk_, s_: kk_ / jnp.maximum(jnp.sqrt(s_[0:c]), KK_NORM_FLOOR), kk, sums)
        b = each(lambda kk_, a_: kk_ * a_, kk, a_sig)
        cum = each(cumsum_rows, ld)
        yield
        at = each(lambda kk_, cu, l_: -kk_ * jnp.exp(cu - l_), kk, cum, ld)
        rt = each(lambda r_, cu: r_ * jnp.exp(cu), r, cum)
        inv = each(lambda cu: jnp.exp(-cu), cum)
        bt = each(lambda b_, i_: b_ * i_, b, inv)
        kt = each(lambda km_, i_: km_ * i_, kmod, inv)
        yield
        to_end = each(lambda cu: jnp.exp(cu[c - 1:c, :] - cu), cum)
        bk_end = each(lambda b_, km_, te: cat([b_ * te, km_ * te], axis=0), b, kmod, to_end)
        p_end_col = each(lambda cu: jnp.sum(eye_p * jnp.exp(cu[c - 1:c, :]), axis=-1, keepdims=True), cum)
        gm = each(lambda at_, rt_, bt_, kt_: dot(cat([at_, rt_], axis=0),
                                                 cat([blockdiag(bt_), blockdiag(kt_)], axis=0), _NT),
                  at, rt, bt, kt)
        yield
        pc = pack * c
        l_ab = [jnp.where(tril_strict, g[0:c, 0:pc], 0.0) for g in gm]
        l_ak = [jnp.where(tril_strict, g[0:c, pc:2 * pc], 0.0) for g in gm]
        m_rb = [jnp.where(tril_incl, g[c:2 * c, 0:pc], 0.0) for g in gm]
        m_rk = [jnp.where(tril_incl, g[c:2 * c, pc:2 * pc], 0.0) for g in gm]
        t_inv = [eye_c + l_ for l_ in l_ab]
        pw = each(lambda q: dot(q, blockdiag(q)), l_ab)
        yield
        span = 4
        while span <= c:
            last = span == c
            both = each(lambda q, t_: dot(q, blockdiag(t_) if last else
                                          cat([blockdiag(t_), blockdiag(q)], axis=1)), pw, t_inv)
            t_inv = each(lambda t_, m_: t_ + m_[:, 0:pc], t_inv, both)
            if not last:
                pw = [m_[:, pc:2 * pc] for m_ in both]
            yield
            span *= 2
        out.update(items=items, r=r, v=v, kmod=kmod, bonus=[s_[c:2 * c] for s_ in sums], at=at, rt=rt,
                   l_ak=l_ak, m_rb=m_rb, m_rk=m_rk, t_inv=t_inv, bk_end=bk_end, p_end_col=p_end_col)

    def state_pass(d):
        items = d["items"]
        y = []
        for i0 in range(0, len(items), n_pairs):
            idx = range(i0, i0 + n_pairs)
            st = [st_scr[p] for p in range(n_pairs)]
            vb = [blockdiag(d["v"][i]) for i in idx]
            x = [dot(cat([d["at"][i], d["l_ak"][i]], axis=1), cat([st[p], vb[p]], axis=0))
                 for p, i in enumerate(idx)]
            yield
            u = [dot(d["t_inv"][i], blockdiag(x[p])) for p, i in enumerate(idx)]
            yield
            y += [dot(cat([d["rt"][i], d["m_rk"][i], d["m_rb"][i]], axis=1),
                      cat([st[p], vb[p], blockdiag(u[p])], axis=0)) for p, i in enumerate(idx)]
            upd = [dot(d["bk_end"][i], cat([u[p], d["v"][i]], axis=0), _TN) for p, i in enumerate(idx)]
            for p, i in enumerate(idx):
                st_scr[p] = d["p_end_col"][i] * st[p] + jnp.where(same_head, upd[p], 0.0)
            yield
        s2 = each(lambda y_: head_sums(cat([y_, y_ * y_], axis=0)), y)
        yield
        for (ci, p), y_, s_, bo_, v_ in zip(items, y, s2, d["bonus"], d["v"]):
            ln = slice(p * pw_lanes, (p + 1) * pw_lanes)
            mean = s_[0:c] * (1.0 / hs)
            var = s_[c:2 * c] * (1.0 / hs) - mean * mean
            yn = (y_ - mean) * lax.rsqrt(var + hs * GROUPNORM_EPS_PER_CHANNEL) * lnw_ref[:, ln] + lnb_ref[:, ln]
            y_o[ci * c:(ci + 1) * c, ln] = yn + bo_ * v_

    def run_together(*progs):
        progs = list(progs)
        while progs:
            for prog in list(progs):
                if next(prog, StopIteration) is StopIteration:
                    progs.remove(prog)

    halves = [range(h, min(h + WKV_BATCH_CHUNKS, n_chunks)) for h in range(0, n_chunks, WKV_BATCH_CHUNKS)]
    prev = None
    for chunks in halves:
        cur = {}
        if prev is None:
            run_together(prepare(chunks, cur))
        else:
            run_together(prepare(chunks, cur), state_pass(prev))
        prev = cur
    run_together(state_pass(prev))

    @pl.when(t == pl.num_programs(1) - 1)
    def _():
        for p in range(n_pairs):
            s_grp = st_scr[p].T
            for i in range(pack):
                s_o[pack * p + i] = s_grp[i * hs:(i + 1) * hs, i * hs:(i + 1) * hs]


def _mix_out_attn_seq_kernel(n_xheads, x_ref, y_ref, g_ref, yb_ref, wout_ref, gx_ref, wq_ref,
                             mk_ref, mv_ref, wo_ref, x2_o):
    tm, d = x_ref.shape
    xd = d // n_xheads
    scale = xd ** -0.5
    n_sub = max(1, tm // MIX_SUB_ROWS)
    sub = tm // n_sub
    mk = mk_ref[...].astype(BF16)
    mv = mv_ref[...].astype(BF16)

    def sub_tile(i):
        rows = slice(i * sub, (i + 1) * sub)
        mixed = jnp.concatenate([y_ref[rows, :] * g_ref[rows, :], yb_ref[rows, :]], axis=-1)
        x1 = x_ref[rows, :] + _bdot(mixed, wout_ref[...])
        yield
        q = _bdot(_rms(x1, gx_ref[...]), wq_ref[...]).astype(BF16)
        yield
        heads = [slice(h * xd, (h + 1) * xd) for h in range(n_xheads)]
        s = [lax.dot_general(q[:, sl], mk[:, sl], _NT, preferred_element_type=F32) * scale for sl in heads]
        yield
        p = [_softmax_rows(s_).astype(BF16) for s_ in s]
        yield
        o = jnp.concatenate([jnp.dot(p_, mv[:, sl], preferred_element_type=F32) for p_, sl in zip(p, heads)],
                            axis=-1)
        yield
        x2_o[rows, :] = x1 + _bdot(o, wo_ref[...])

    _run_together(*[sub_tile(i) for i in range(n_sub)])


def _ffn_seq_kernel(final, *refs):
    it = iter(refs)
    x_ref, gf_ref, wup_ref, fprev_ref, fcw_ref, wdown_ref = (next(it) for _ in range(6))
    gfin_ref = next(it) if final else None
    y_o, fnew_o, carry = (next(it) for _ in range(3))

    t = pl.program_id(1)
    tm = x_ref.shape[0]
    f = wdown_ref.shape[0]
    fc = FFN_COL_TILE

    @pl.when(t == 0)
    def _():
        carry[0:2, :] = fprev_ref[...]

    n_sub = max(1, tm // FFN_SUB_ROWS)
    sub = tm // n_sub
    n_tiles = f // fc
    xs, xfs = {}, {}

    def up_item(s, j):
        if s not in xs:
            xs[s] = x_ref[s * sub:(s + 1) * sub, :]
            xfs[s] = _rms(xs[s], gf_ref[...]).astype(BF16)
        c0 = j * fc
        return (jnp.dot(xfs[s], wup_ref[:, c0:c0 + fc], preferred_element_type=F32),
                jnp.dot(xfs[s], wup_ref[:, f + c0:f + c0 + fc], preferred_element_type=F32))

    def conv_cols(up, c0):
        p0 = carry[0:1, c0:c0 + fc]
        p1 = carry[1:2, c0:c0 + fc]
        up1 = _shift_rows(up, [p1])
        up2 = _shift_rows(up, [p0, p1])
        carry[0:2, c0:c0 + fc] = up[sub - 2:sub, :]
        return (fcw_ref[0:1, c0:c0 + fc] * up2 + fcw_ref[1:2, c0:c0 + fc] * up1
                + fcw_ref[2:3, c0:c0 + fc] * up)

    items = [(s, j) for s in range(n_sub) for j in range(n_tiles)]
    ahead = [up_item(*it_) for it_ in items[:FFN_LOOKAHEAD]]
    acc = None
    for idx, (s, j) in enumerate(items):
        c0 = j * fc
        cur = ahead.pop(0)
        if idx + FFN_LOOKAHEAD < len(items):
            ahead.append(up_item(*items[idx + FFN_LOOKAHEAD]))
        if j == 0:
            acc = xs[s]
        u = conv_cols(cur[0], c0)
        gt = conv_cols(cur[1], f + c0)
        hidden = gt * _sigmoid(gt) * u
        acc = acc + _bdot(hidden, wdown_ref[c0:c0 + fc, :])
        if j == n_tiles - 1:
            if final:
                acc = _rms(acc, gfin_ref[...])
            y_o[s * sub:(s + 1) * sub, :] = acc

    @pl.when(t == pl.num_programs(1) - 1)
    def _():
        fnew_o[...] = carry[0:2, :]


def _mem_kv_kernel(m_ref, g_ref, wk_ref, wv_ref, k_o, v_o, kh_o, vh_o):
    mn = _rms(m_ref[...], g_ref[...])
    nseq, m, nh, xd = kh_o.shape
    for w_ref, flat_o, heads_o in ((wk_ref, k_o, kh_o), (wv_ref, v_o, vh_o)):
        val = _bdot(mn, w_ref[...])
        flat_o[...] = val
        for s in range(nseq):
            for h in range(nh):
                heads_o[s, :, h, :] = val[s * m:(s + 1) * m, h * xd:(h + 1) * xd]


def _mix_in_step_kernel(has_vmix, lora_dims, *refs):
    it = iter(refs)
    x_ref, gmix_ref, win_ref, sprev_ref, cprev_ref, mu_ref, w0_ref, a0_ref, wl_ref, cw_ref = (
        next(it) for _ in range(10))
    if has_vmix:
        vf_ref, v0_ref, v1_ref, v2_ref = (next(it) for _ in range(4))
    r_o, k_o, v_o, a_o, ld_o, vrow_o, g_o, yb_o, xn_o, cnew_o = (next(it) for _ in range(10))

    nb = x_ref.shape[0]
    rc = mu_ref.shape[1]
    cw = cw_ref.shape[1]
    rw = w0_ref.shape[1]

    xn = _rms(x_ref[...], gmix_ref[...])
    xn_o[...] = xn
    stacked = jnp.concatenate([xn, sprev_ref[...]], axis=0)
    proj = _bdot(stacked, win_ref[...])
    p_cur = proj[0:nb, 0:rc]
    p_shift = proj[nb:2 * nb, 0:rc]
    vmix = None
    if has_vmix:
        vmix = (vf_ref[...], v0_ref[...], v1_ref[...], v2_ref[...])
    r, k, v, a_sig, log_decay, gate = _mix_rows(
        p_cur, p_shift, mu_ref[...], w0_ref[...], a0_ref[...], wl_ref[...], lora_dims, rw, vmix)
    r_o[...] = r.T
    k_o[...] = k.T
    v_o[...] = v.T
    a_o[...] = a_sig.T
    ld_o[...] = log_decay.T
    vrow_o[...] = v
    g_o[...] = gate

    gate_b = proj[0:nb, rc:rc + cw]
    gate_c = proj[0:nb, rc + cw:rc + 2 * cw]
    h_in = proj[0:nb, rc + 2 * cw:rc + 3 * cw]
    u = gate_c * h_in
    u2 = cprev_ref[:, 0, :]
    u1 = cprev_ref[:, 1, :]
    yb_o[...] = gate_b * (cw_ref[0:1, :] * u2 + cw_ref[1:2, :] * u1 + cw_ref[2:3, :] * u)
    cnew_o[:, 0, :] = u1
    cnew_o[:, 1, :] = u


def _wkv_step_kernel(aliased, *refs):
    r_ref, k_ref, v_ref, a_ref, ld_ref, kkw_ref, kaw_ref, rkw_ref, lnw_ref, lnb_ref, s_ref = refs[:11]
    y_o, s_o, y_scr = refs[12:] if aliased else refs[11:]
    n = s_ref.shape[0]

    @pl.when(pl.program_id(0) == 0)
    def _():
        r, k, v, a_sig, ld = r_ref[...], k_ref[...], v_ref[...], a_ref[...], ld_ref[...]
        kk = k * kkw_ref[...]
        kk = kk / jnp.maximum(jnp.sqrt(jnp.sum(kk * kk, axis=0, keepdims=True)), KK_NORM_FLOOR)
        kmod = k * (1.0 + (a_sig - 1.0) * kaw_ref[...])
        a = -kk
        b = kk * a_sig
        decay = jnp.exp(ld)
        for vi in range(n):
            s = s_ref[vi]
            sa = jnp.sum(s * a, axis=0, keepdims=True)
            s_new = s * decay + sa * b + v[vi:vi + 1, :] * kmod
            s_o[vi] = s_new
            y_scr[vi:vi + 1, :] = jnp.sum(s_new * r, axis=0, keepdims=True)
        y = y_scr[...]
        mean = jnp.mean(y, axis=0, keepdims=True)
        yc = y - mean
        var = jnp.mean(yc * yc, axis=0, keepdims=True)
        y = yc * lax.rsqrt(var + n * GROUPNORM_EPS_PER_CHANNEL) * lnw_ref[...] + lnb_ref[...]
        y_o[...] = y + jnp.sum(r * kmod * rkw_ref[...], axis=0, keepdims=True) * v

    @pl.when(pl.program_id(0) > 0)
    def _():
        s_o[...] = jnp.zeros(s_o.shape, F32)


def _mix_out_q_step_kernel(x_ref, ya_ref, g_ref, yb_ref, wout_ref, gx_ref, wq_ref, x1_o, q_o):
    mixed = jnp.concatenate([ya_ref[...].T * g_ref[...], yb_ref[...]], axis=-1)
    x1 = x_ref[...] + _bdot(mixed, wout_ref[...])
    x1_o[...] = x1
    q_o[...] = _bdot(_rms(x1, gx_ref[...]), wq_ref[...])


def _attn_step_kernel(q_ref, mk_ref, mv_ref, o_o):
    bt = q_ref.shape[0]
    m, nh, xd = mk_ref.shape[1:]
    scale = xd ** -0.5
    fold = SUBLANES // nh if SUBLANES % nh == 0 and m % max(SUBLANES // nh, 1) == 0 else 1

    def folded(x):
        return [x[j * nh:(j + 1) * nh] for j in range(fold)]

    def body(i, carry):
        q = jnp.concatenate([q_ref[i]] * fold, axis=0)
        k = mk_ref[i].reshape(m // fold, fold * nh, xd)
        v = mv_ref[i].reshape(m // fold, fold * nh, xd)
        s = jnp.sum(k * q, axis=-1, keepdims=True) * scale
        mx = functools.reduce(jnp.maximum, folded(jnp.max(s, axis=0)))
        e = jnp.exp(s - jnp.concatenate([mx] * fold, axis=0))
        den = sum(folded(jnp.sum(e, axis=0)))
        acc = sum(folded(jnp.sum(e * v, axis=0)))
        o_o[i] = acc / den
        return carry

    lax.fori_loop(0, bt, body, 0)


def _ffn_step_kernel(final, aliased, *refs):
    it = iter(refs)
    x1_ref, o_ref, wo_ref, gf_ref, wup_ref, fprev_ref, fcw_ref, wdown_ref = (next(it) for _ in range(8))
    gfin_ref = next(it) if final else None
    if aliased:
        next(it)
    y_o, fnew_o = (next(it) for _ in range(2))
    f = wdown_ref.shape[0]
    f2 = 2 * f

    @pl.when(pl.program_id(0) == 0)
    def _():
        x2 = x1_ref[...] + _bdot(o_ref[...], wo_ref[...])
        up = _bdot(_rms(x2, gf_ref[...]), wup_ref[...])
        up2 = fprev_ref[:, 0, :]
        up1 = fprev_ref[:, 1, :]
        upc = fcw_ref[0:1, :] * up2 + fcw_ref[1:2, :] * up1 + fcw_ref[2:3, :] * up
        fnew_o[:, 0, :] = up1
        fnew_o[:, 1, :] = up
        u = upc[:, 0:f]
        gt = upc[:, f:f2]
        x3 = x2 + _bdot(gt * _sigmoid(gt) * u, wdown_ref[...])
        if final:
            x3 = _rms(x3, gfin_ref[...])
        y_o[...] = x3

    @pl.when(pl.program_id(0) > 0)
    def _():
        fnew_o[...] = jnp.zeros(fnew_o.shape, F32)


def _params(*sem):
    return pltpu.CompilerParams(dimension_semantics=sem, vmem_limit_bytes=V7X_VMEM_LIMIT_BYTES)


class _Layered(NamedTuple):
    stacked: jax.Array
    layer: int

    @property
    def shape(self):
        return self.stacked.shape[1:]


def _operand(x):
    return x.stacked if isinstance(x, _Layered) else x


def _whole(x):
    if isinstance(x, _Layered):
        nd, l = x.stacked.ndim, x.layer
        return pl.BlockSpec((None,) + x.shape, lambda *_: (l,) + (0,) * (nd - 1), pipeline_mode=pl.Buffered(1))
    nd = x.ndim
    return pl.BlockSpec(x.shape, lambda *_: (0,) * nd, pipeline_mode=pl.Buffered(1))


def _sds(shape):
    return jax.ShapeDtypeStruct(shape, F32)


def _stacked_weights(p):
    depth = p["w_in"].shape[0]
    rows = lambda a: a.reshape(a.shape[0], 1, -1)
    rw = p["w0"].shape[1]
    d_decay, d_aaa, d_gate = p["w2"].shape[1], p["a2"].shape[1], p["g2"].shape[1]
    pad_cols = lambda a, before, after: jnp.pad(a, ((0, 0), (0, 0), (before, after)))
    w_lora = jnp.concatenate([pad_cols(p["w2"], 0, 2 * rw), pad_cols(p["a2"], rw, rw), pad_cols(p["g2"], 2 * rw, 0)],
                             axis=1)
    lanes = 128
    mv = p["v1"].shape[2]
    mvp = -(-mv // lanes) * lanes
    w = dict(
        norm_mix=rows(p["norm_mix"]), w_in=p["w_in"].astype(BF16), mu=rows(p["mu_shift"]), w0=rows(p["w0"]),
        a0=rows(p["a0"]), w_lora=w_lora.astype(BF16), conv_w=p["conv_w"], k_k=p["k_k"], k_a=p["k_a"],
        r_k=p["r_k"], ln_w=p["ln_x_w"], ln_b=p["ln_x_b"], w_out=p["w_out"].astype(BF16), norm_x=rows(p["norm_x"]),
        wq=p["wq"].astype(BF16), wo=p["wo"].astype(BF16), norm_ffn=rows(p["norm_ffn"]),
        w_up=p["w_up"].astype(BF16), ffn_conv_w=p["ffn_conv_w"], w_down=p["w_down"].astype(BF16),
        norm_mem=rows(p["norm_mem"]), wk=p["wk"].astype(BF16), wv=p["wv"].astype(BF16),
    )
    layers = []
    for l in range(depth):
        wl = {k_: _Layered(v_, l) for k_, v_ in w.items()}
        wl["norm_final"] = p["norm_final"].reshape(1, -1)
        wl["lora_dims"] = (d_decay, d_aaa, d_gate)
        if l > 0:
            wl["v0"] = _Layered(rows(p["v0"]), l - 1)
            wl["v1"] = _Layered(jnp.pad(p["v1"], ((0, 0), (0, 0), (0, mvp - mv))).astype(BF16), l - 1)
            wl["v2"] = _Layered(jnp.pad(p["v2"], ((0, 0), (0, mvp - mv), (0, 0))).astype(BF16), l - 1)
        layers.append(wl)
    return layers


def _head_rows(x, n_heads):
    st = x.stacked
    return _Layered(st.reshape(st.shape[0], n_heads, st.shape[1] // n_heads, 1), x.layer)


def _mem_kv(mem, w, n_xheads):
    bsz, m, d = mem.shape
    depth = w["wk"].stacked.shape[0]
    nseq = max(1, min(bsz, 512 // m))
    xd = d // n_xheads
    stacked = lambda a: pl.BlockSpec((None,) + a.shape, lambda l, i: (l,) + (0,) * len(a.shape))
    flat = pl.BlockSpec((None, nseq * m, d), lambda l, i: (l, i, 0))
    heads = pl.BlockSpec((None, nseq, m, n_xheads, xd), lambda l, i: (l, i, 0, 0, 0))
    args = [mem.reshape(bsz * m, d), w["norm_mem"], w["wk"], w["wv"]]
    k, v, kh, vh = pl.pallas_call(
        _mem_kv_kernel,
        grid=(depth, bsz // nseq),
        in_specs=[pl.BlockSpec((nseq * m, d), lambda l, i: (i, 0))] + [stacked(a) for a in args[1:]],
        out_specs=[flat, flat, heads, heads],
        out_shape=[_sds((depth, bsz * m, d))] * 2 + [_sds((depth, bsz, m, n_xheads, xd))] * 2,
        compiler_params=_params("arbitrary", "arbitrary"),
        name="mem_kv",
    )(*map(_operand, args))
    return k.reshape(depth, bsz, m, d), v.reshape(depth, bsz, m, d), kh, vh


def _prompt_layer(l, x, mem_k, mem_v, shift0, wkv0, conv0, ffn0, v_first, w, n_heads, n_xheads, final):
    shift0, wkv0, conv0, ffn0 = (a[l] for a in (shift0, wkv0, conv0, ffn0))
    bsz, t, d = x.shape
    rw = w["w0"].shape[1]
    hs = rw // n_heads
    cw = w["conv_w"].shape[1]
    rc = w["mu"].shape[1]
    tm = min(ROW_TILE, t)
    nt = t // tm
    has_vmix = v_first is not None

    tile = lambda n: pl.BlockSpec((None, tm, n), lambda b, i: (b, i, 0))
    per_b = lambda s: pl.BlockSpec((None,) + s, lambda b, i: (b,) + (0,) * len(s))

    ins = [x, w["norm_mix"], w["w_in"], shift0[:, None, :], conv0, w["mu"], w["w0"], w["a0"], w["w_lora"],
           w["conv_w"]]
    specs = [tile(d), _whole(w["norm_mix"]), _whole(w["w_in"]), per_b((1, d)), per_b((2, cw)),
             _whole(w["mu"]), _whole(w["w0"]), _whole(w["a0"]), _whole(w["w_lora"]), _whole(w["conv_w"])]
    if has_vmix:
        ins += [v_first, w["v0"], w["v1"], w["v2"]]
        specs += [tile(rw), _whole(w["v0"]), _whole(w["v1"]), _whole(w["v2"])]
    out_shape = [_sds((bsz, t, rw))] * 6 + [_sds((bsz, t, cw)), _sds((bsz, 1, d)), _sds((bsz, 2, cw))]
    out_specs = [tile(rw)] * 6 + [tile(cw), per_b((1, d)), per_b((2, cw))]
    r, k, v, a_sig, ld, gate, y_b, xn_last, conv_new = pl.pallas_call(
        functools.partial(_mix_in_seq_kernel, has_vmix, w["lora_dims"]),
        grid=(bsz, nt), in_specs=specs, out_specs=out_specs, out_shape=out_shape,
        scratch_shapes=[pltpu.VMEM((8, rc), F32), pltpu.VMEM((8, cw), F32)],
        compiler_params=_params("parallel", "arbitrary"),
        name="mix_in_seq",
    )(*map(_operand, ins))
    if not has_vmix:
        v_first = v

    head_params = [_Layered(a.stacked[:, None, :], a.layer) for a in
                   (w["k_k"], w["k_a"], w["r_k"], w["ln_w"], w["ln_b"])]
    sblk = per_b((n_heads, hs, hs))
    wt = min(WKV_ROW_TILE, t)
    wtile = pl.BlockSpec((None, wt, rw), lambda b, i: (b, i, 0))
    y_a, wkv_new = pl.pallas_call(
        functools.partial(_wkv_seq_kernel, hs),
        grid=(bsz, t // wt),
        in_specs=[wtile] * 5 + [_whole(a) for a in head_params] + [sblk],
        out_specs=[wtile, sblk],
        out_shape=[_sds((bsz, t, rw)), _sds((bsz, n_heads, hs, hs))],
        scratch_shapes=[pltpu.VMEM((rw // (WKV_HEAD_PACK * hs), WKV_HEAD_PACK * hs, WKV_HEAD_PACK * hs), F32)],
        compiler_params=_params("parallel", "arbitrary"),
        name="wkv_seq",
    )(r, k, v, a_sig, ld, *map(_operand, head_params), wkv0)

    mblk = pl.BlockSpec((None, None) + mem_k.shape[2:], lambda b, i: (l, b, 0, 0))
    x2 = pl.pallas_call(
        functools.partial(_mix_out_attn_seq_kernel, n_xheads),
        grid=(bsz, nt),
        in_specs=[tile(d), tile(rw), tile(rw), tile(cw), _whole(w["w_out"]), _whole(w["norm_x"]), _whole(w["wq"]),
                  mblk, mblk, _whole(w["wo"])],
        out_specs=tile(d), out_shape=_sds((bsz, t, d)),
        compiler_params=_params("parallel", "arbitrary"),
        name="mix_out_attn_seq",
    )(*map(_operand, (x, y_a, gate, y_b, w["w_out"], w["norm_x"], w["wq"], mem_k, mem_v, w["wo"])))

    f2 = w["w_up"].shape[1]
    ins = [x2, w["norm_ffn"], w["w_up"], ffn0, w["ffn_conv_w"], w["w_down"]]
    ft = min(FFN_ROW_TILE, t)
    ftile = pl.BlockSpec((None, ft, d), lambda b, i: (b, i, 0))
    specs = [ftile, _whole(w["norm_ffn"]), _whole(w["w_up"]), per_b((2, f2)), _whole(w["ffn_conv_w"]),
             _whole(w["w_down"])]
    if final:
        ins.append(w["norm_final"])
        specs.append(_whole(w["norm_final"]))
    x3, ffn_new = pl.pallas_call(
        functools.partial(_ffn_seq_kernel, final),
        grid=(bsz, t // ft), in_specs=specs,
        out_specs=[ftile, per_b((2, f2))], out_shape=[_sds((bsz, t, d)), _sds((bsz, 2, f2))],
        scratch_shapes=[pltpu.VMEM((8, f2), F32)],
        compiler_params=_params("parallel", "arbitrary"),
        name="ffn_seq",
    )(*map(_operand, ins))
    return x3, xn_last[:, 0, :], wkv_new, conv_new, ffn_new, v_first


def _sample_layer(l, x, mem_k, mem_v, shift0, wkv_t, conv0, ffn0, carried, w, n_heads, n_xheads, final):
    v_first, wkv_all, ffn_all = carried
    nb, d = x.shape
    rw = w["w0"].shape[1]
    hs = rw // n_heads
    cw = w["conv_w"].shape[1]
    has_vmix = v_first is not None
    one = lambda a: pl.BlockSpec(a.shape, lambda i: (0,) * a.ndim, pipeline_mode=pl.Buffered(1))
    layer_of = lambda a: pl.BlockSpec((None,) + a.shape[1:], lambda i: (l,) + (0,) * (a.ndim - 1),
                                      pipeline_mode=pl.Buffered(1))

    ins = [x, w["norm_mix"], w["w_in"], shift0, conv0, w["mu"], w["w0"], w["a0"], w["w_lora"], w["conv_w"]]
    specs = [one(x), _whole(w["norm_mix"]), _whole(w["w_in"]), layer_of(shift0), layer_of(conv0)] + [
        _whole(a) for a in ins[5:]]
    if has_vmix:
        ins += [v_first, w["v0"], w["v1"], w["v2"]]
        specs += [one(v_first), _whole(w["v0"]), _whole(w["v1"]), _whole(w["v2"])]
    out_shape = [_sds((rw, nb))] * 5 + [_sds((nb, rw))] * 2 + [_sds((nb, cw)), _sds((nb, d)), _sds((nb, 2, cw))]
    r, k, v, a_sig, ld, v_rows, gate, y_b, xn, conv_new = pl.pallas_call(
        functools.partial(_mix_in_step_kernel, has_vmix, w["lora_dims"]),
        grid=(1,), in_specs=specs, out_specs=[one(s_) for s_ in out_shape], out_shape=out_shape,
        compiler_params=_params("arbitrary"),
        name="mix_in_step",
    )(*map(_operand, ins))
    if not has_vmix:
        v_first = v_rows

    aliased = wkv_all is not None
    n_clear = 0 if aliased else wkv_t.shape[0] - 1 - l
    last = n_heads - 1
    hsel = lambda c, h: jnp.where(c == 0, h, last)
    heads = lambda a: a.reshape(n_heads, hs, nb)
    hblk = pl.BlockSpec((None, hs, nb), lambda c, h: (hsel(c, h), 0, 0))
    head_params = [_head_rows(w[n_], n_heads) for n_ in ("k_k", "k_a", "r_k", "ln_w", "ln_b")]
    pblk = pl.BlockSpec((None, None, hs, 1), lambda c, h: (l, hsel(c, h), 0, 0))
    sblk = pl.BlockSpec((None, None, hs, hs, nb), lambda c, h: (l, hsel(c, h), 0, 0, 0))
    soblk = pl.BlockSpec((None, None, hs, hs, nb), lambda c, h: (l + c, h, 0, 0, 0))
    ins = [heads(r), heads(k), heads(v), heads(a_sig), heads(ld)] + [a.stacked for a in head_params] + [wkv_t]
    specs = [hblk] * 5 + [pblk] * 5 + [sblk]
    if aliased:
        ins.append(wkv_all)
        specs.append(pl.BlockSpec(memory_space=pl.ANY))
    y_a, wkv_all = pl.pallas_call(
        functools.partial(_wkv_step_kernel, aliased),
        grid=(1 + n_clear, n_heads),
        in_specs=specs, out_specs=[hblk, soblk],
        out_shape=[_sds((n_heads, hs, nb)), _sds(wkv_t.shape)],
        scratch_shapes=[pltpu.VMEM((hs, nb), F32)],
        input_output_aliases={len(ins) - 1: 1} if aliased else {},
        compiler_params=_params("arbitrary", "arbitrary"),
        name="wkv_step",
    )(*ins)

    ins = [x, y_a.reshape(rw, nb), gate, y_b, w["w_out"], w["norm_x"], w["wq"]]
    x1, q = pl.pallas_call(
        _mix_out_q_step_kernel,
        grid=(1,), in_specs=[one(a) for a in ins[:4]] + [_whole(a) for a in ins[4:]],
        out_specs=[one(x)] * 2, out_shape=[_sds((nb, d))] * 2,
        compiler_params=_params("arbitrary"),
        name="mix_out_q_step",
    )(*map(_operand, ins))

    m, xd = mem_k.shape[2], mem_k.shape[4]
    ab = min(ATTN_BATCH_TILE, nb)
    qblk = pl.BlockSpec((ab, n_xheads, xd), lambda i: (i, 0, 0))
    mblk = pl.BlockSpec((None, ab, m, n_xheads, xd), lambda i: (l, i, 0, 0, 0))
    o = pl.pallas_call(
        _attn_step_kernel,
        grid=(nb // ab,),
        in_specs=[qblk, mblk, mblk], out_specs=qblk, out_shape=_sds((nb, n_xheads, xd)),
        compiler_params=_params("parallel"),
        name="attn_step",
    )(q.reshape(nb, n_xheads, xd), mem_k, mem_v)
    o = o.reshape(nb, d)

    ins = [x1, o, w["wo"], w["norm_ffn"], w["w_up"], ffn0, w["ffn_conv_w"], w["w_down"]]
    fblk = pl.BlockSpec((None,) + ffn0.shape[1:], lambda c: (l, 0, 0, 0))
    foblk = pl.BlockSpec((None,) + ffn0.shape[1:], lambda c: (l + c, 0, 0, 0))
    specs = [one(x1), one(o), _whole(w["wo"]), _whole(w["norm_ffn"]), _whole(w["w_up"]), fblk,
             _whole(w["ffn_conv_w"]), _whole(w["w_down"])]
    if final:
        ins.append(w["norm_final"])
        specs.append(_whole(w["norm_final"]))
    ffn_aliased = ffn_all is not None
    if ffn_aliased:
        ins.append(ffn_all)
        specs.append(pl.BlockSpec(memory_space=pl.ANY))
    x3, ffn_all = pl.pallas_call(
        functools.partial(_ffn_step_kernel, final, ffn_aliased),
        grid=(1 if ffn_aliased else ffn0.shape[0] - l,), in_specs=specs, out_specs=[one(x), foblk],
        out_shape=[_sds((nb, d)), _sds(ffn0.shape)],
        input_output_aliases={len(ins) - 1: 1} if ffn_aliased else {},
        compiler_params=_params("arbitrary"),
        name="ffn_step",
    )(*map(_operand, ins))
    return x3, xn, conv_new, (v_first, wkv_all, ffn_all)


def kernel(x_prompt, x_sample, mem_prompt, state_shift, state_wkv, state_conv, state_ffn, cache_mem_k, cache_mem_v, norm_mix, w_in, mu_shift, w0, w2, a0, a2, g2, v0, v1, v2, k_k, k_a, r_k, ln_x_w, ln_x_b, conv_w, w_out, norm_x, norm_mem, wq, wk, wv, wo, norm_ffn, w_up, ffn_conv_w, w_down, norm_final):
    p = dict(norm_mix=norm_mix, w_in=w_in, mu_shift=mu_shift, w0=w0, w2=w2, a0=a0, a2=a2, g2=g2, v0=v0, v1=v1,
             v2=v2, k_k=k_k, k_a=k_a, r_k=r_k, ln_x_w=ln_x_w, ln_x_b=ln_x_b, conv_w=conv_w, w_out=w_out,
             norm_x=norm_x, norm_mem=norm_mem, wq=wq, wk=wk, wv=wv, wo=wo, norm_ffn=norm_ffn, w_up=w_up,
             ffn_conv_w=ffn_conv_w, w_down=w_down, norm_final=norm_final)
    depth = w_in.shape[0]
    n_heads = state_wkv.shape[2]
    hs = state_wkv.shape[3]
    n_xheads, xd = cache_mem_k.shape[3], cache_mem_k.shape[4]
    assert x_sample.shape[1] == 1, "the sample group advances one token per sequence"
    weights = _stacked_weights(p)

    bp, tp, d = x_prompt.shape
    m = mem_prompt.shape[1]
    mks, mvs, mem_k_p, mem_v_p = _mem_kv(mem_prompt, weights[0], n_xheads)
    cw = conv_w.shape[2]
    f2 = w_up.shape[2]
    zeros = lambda *s_: jnp.zeros((depth, bp) + s_, F32)
    x, v_first = x_prompt, None
    shs, wks, cvs, ffs = [], [], [], []
    for l in range(depth):
        x, sh, s_, cs, fs, v_first = _prompt_layer(
            l, x, mks, mvs, zeros(d), zeros(n_heads, hs, hs), zeros(state_conv.shape[2], cw),
            zeros(state_ffn.shape[2], f2), v_first, weights[l], n_heads, n_xheads, l == depth - 1)
        shs.append(sh)
        wks.append(s_)
        cvs.append(cs)
        ffs.append(fs)
    y_p = x

    nb = x_sample.shape[0]
    wkv_t = jnp.transpose(state_wkv, (0, 2, 3, 4, 1))
    x, carried = x_sample.reshape(nb, d), (None, None, None)
    xns, cns = [], []
    for l in range(depth):
        x, xn, cn, carried = _sample_layer(l, x, cache_mem_k, cache_mem_v, state_shift, wkv_t, state_conv,
                                           state_ffn, carried, weights[l], n_heads, n_xheads, l == depth - 1)
        xns.append(xn)
        cns.append(cn)
    _, wkv_all, ffn_s = carried
    wkv_s = jnp.transpose(wkv_all, (0, 4, 1, 2, 3))
    return (y_p, x.reshape(x_sample.shape), jnp.stack(shs), jnp.stack(wks), jnp.stack(cvs), jnp.stack(ffs),
            mem_k_p, mem_v_p, jnp.stack(xns), wkv_s, jnp.stack(cns), ffn_s)
```

```python
import functools
from typing import NamedTuple

import jax
import jax.numpy as jnp
from jax import lax
from jax.experimental import pallas as pl
from jax.experimental.pallas import tpu as pltpu

F32 = jnp.float32
BF16 = jnp.bfloat16
HIGHEST = lax.Precision.HIGHEST

RMS_EPS = 1e-6
GROUPNORM_EPS_PER_CHANNEL = 1e-5
KK_NORM_FLOOR = 1e-12

ROW_TILE = 512
WKV_ROW_TILE = 1024
FFN_ROW_TILE = 512
FFN_SUB_ROWS = 256
MIX_SUB_ROWS = 256
CHUNK = 64
WKV_HEAD_PACK = 2
WKV_BATCH_CHUNKS = 4
WKV_BATCH_TILE = 8
FFN_COL_TILE = 256
FFN_LOOKAHEAD = 4
V7X_VMEM_LIMIT_BYTES = 56 * 1024 * 1024
SUBLANES = 8


def _rms(x, g):
    ms = jnp.mean(x * x, axis=-1, keepdims=True)
    return x * lax.rsqrt(ms + RMS_EPS) * g


def _bdot(a, b):
    return jnp.dot(a.astype(BF16), b.astype(BF16), preferred_element_type=F32)


def _hdot(a, b, dims=(((1,), (0,)), ((), ()))):
    return lax.dot_general(a, b, dims, precision=HIGHEST, preferred_element_type=F32)


def _sigmoid(x):
    return 1.0 / (1.0 + jnp.exp(-x))


def _softplus(x):
    return jnp.maximum(x, 0.0) + jnp.log1p(jnp.exp(-jnp.abs(x)))


def _shift_rows(x, prev_rows):
    s = len(prev_rows)
    rolled = pltpu.roll(x, s, axis=0)
    top = rolled[0:SUBLANES, :]
    row = lax.broadcasted_iota(jnp.int32, top.shape, 0)
    for i, pr in enumerate(prev_rows):
        top = jnp.where(row == i, pr, top)
    return jnp.concatenate([top, rolled[SUBLANES:, :]], axis=0)


def _mix_rows_stages(p_cur, p_shift, mu, w0, a0, w_lora, lora_dims, rw, vmix, out):
    d_decay, d_aaa, _ = lora_dims
    z = p_cur + (p_shift - p_cur) * mu
    r = z[:, 0:rw]
    k = z[:, rw:2 * rw]
    v = z[:, 2 * rw:3 * rw]
    lo = z[:, 3 * rw:]
    lane = lax.broadcasted_iota(jnp.int32, lo.shape, 1)
    feat = jnp.where(lane < d_decay, jnp.tanh(lo),
                     jnp.where(lane < d_decay + d_aaa, lo, _sigmoid(lo)))
    yield
    lora = _bdot(feat, w_lora)
    yield
    w = -_softplus(-(w0 + lora[:, 0:rw])) - 0.5
    log_decay = -jnp.exp(w)
    a_sig = _sigmoid(a0 + lora[:, rw:2 * rw])
    gate = lora[:, 2 * rw:3 * rw]
    if vmix is not None:
        v_first, v0, v1, v2 = vmix
        t = _bdot(v, v1)
        yield
        pre = _bdot(t, v2)
        yield
        v = v + (v_first - v) * _sigmoid(v0 + pre)
    out.extend([r, k, v, a_sig, log_decay, gate])


def _mix_rows(*args):
    out = []
    for _ in _mix_rows_stages(*args, out):
        pass
    return out


def _run_together(*progs):
    progs = list(progs)
    while progs:
        for prog in list(progs):
            if next(prog, StopIteration) is StopIteration:
                progs.remove(prog)


def _group_norm_bonus(y, r, kmod, v, rk, lnw, lnb, n):
    mean = jnp.mean(y, axis=-1, keepdims=True)
    yc = y - mean
    var = jnp.mean(yc * yc, axis=-1, keepdims=True)
    y = yc * lax.rsqrt(var + n * GROUPNORM_EPS_PER_CHANNEL) * lnw + lnb
    return y + jnp.sum(r * kmod * rk, axis=-1, keepdims=True) * v


def _key_features(k, a_sig, kk_w, ka_w):
    kk = k * kk_w
    nrm = jnp.sqrt(jnp.sum(kk * kk, axis=-1, keepdims=True))
    kk = kk / jnp.maximum(nrm, KK_NORM_FLOOR)
    kmod = k * (1.0 + (a_sig - 1.0) * ka_w)
    return kk, kmod


def _softmax_rows(s):
    m = jnp.max(s, axis=-1, keepdims=True)
    e = jnp.exp(s - m)
    return e / jnp.sum(e, axis=-1, keepdims=True)


def _mix_in_seq_kernel(has_vmix, lora_dims, *refs):
    it = iter(refs)
    x_ref, gmix_ref, win_ref, sprev_ref, cprev_ref, mu_ref, w0_ref, a0_ref, wl_ref, cw_ref = (
        next(it) for _ in range(10))
    if has_vmix:
        vf_ref, v0_ref, v1_ref, v2_ref = (next(it) for _ in range(4))
    r_o, k_o, v_o, a_o, ld_o, g_o, yb_o, xn_o, cnew_o, pc_scr, uc_scr = (next(it) for _ in range(11))

    t = pl.program_id(1)
    tm = x_ref.shape[0]
    rc = mu_ref.shape[1]
    cw = cw_ref.shape[1]
    rw = w0_ref.shape[1]

    @pl.when(t == 0)
    def _():
        sprev = jnp.broadcast_to(sprev_ref[...], (8, sprev_ref.shape[1]))
        pc_scr[...] = _bdot(sprev, win_ref[:, 0:rc])
        uc_scr[0:2, :] = cprev_ref[...]

    n_sub = max(1, tm // MIX_SUB_ROWS)
    sub = tm // n_sub
    col_chunks = [(0, 2 * rw), (2 * rw, rc), (rc, rc + cw), (rc + cw, rc + 3 * cw)]
    xns = [_rms(x_ref[i * sub:(i + 1) * sub, :], gmix_ref[...]) for i in range(n_sub)]
    carry = dict(p=pc_scr[0:1, :], u=[uc_scr[0:1, :], uc_scr[1:2, :]])

    def project(i, out):
        xb = xns[i].astype(BF16)
        for c0, c1 in col_chunks:
            out.append(jnp.dot(xb, win_ref[:, c0:c1], preferred_element_type=F32))
            yield

    def features(i, parts):
        rows = slice(i * sub, (i + 1) * sub)
        p_cur = jnp.concatenate(parts[0:2], axis=1)
        p_shift = _shift_rows(p_cur, [carry["p"]])
        vmix = None
        if has_vmix:
            vmix = (vf_ref[rows, :], v0_ref[...], v1_ref[...], v2_ref[...])
        mixed = []
        yield from _mix_rows_stages(p_cur, p_shift, mu_ref[...], w0_ref[...], a0_ref[...], wl_ref[...],
                                    lora_dims, rw, vmix, mixed)
        for ref, val in zip((r_o, k_o, v_o, a_o, ld_o, g_o), mixed):
            ref[rows, :] = val
        yield
        gate_b = parts[2]
        u = parts[3][:, 0:cw] * parts[3][:, cw:2 * cw]
        u1 = _shift_rows(u, [carry["u"][1]])
        u2 = _shift_rows(u, carry["u"])
        yb_o[rows, :] = gate_b * (cw_ref[0:1, :] * u2 + cw_ref[1:2, :] * u1 + cw_ref[2:3, :] * u)
        carry["p"] = p_cur[sub - 1:sub, :]
        carry["u"] = [u[sub - 2:sub - 1, :], u[sub - 1:sub, :]]

    parts = []
    _run_together(project(0, parts))
    for i in range(n_sub):
        nxt = []
        if i + 1 < n_sub:
            _run_together(features(i, parts), project(i + 1, nxt))
        else:
            _run_together(features(i, parts))
        parts = nxt

    pc_scr[0:1, :] = carry["p"]
    uc_scr[0:1, :] = carry["u"][0]
    uc_scr[1:2, :] = carry["u"][1]

    @pl.when(t == pl.num_programs(1) - 1)
    def _():
        xn_o[...] = xns[-1][sub - 1:sub, :]
        cnew_o[0:1, :] = carry["u"][0]
        cnew_o[1:2, :] = carry["u"][1]


def _split3(x):
    hi = x.astype(BF16)
    r1 = x - hi.astype(F32)
    mid = r1.astype(BF16)
    lo = (r1 - mid.astype(F32)).astype(BF16)
    return hi, mid, lo


def _dot3(a, b, dims=(((1,), (0,)), ((), ()))):
    a_hi = a.astype(BF16)
    a_lo = (a - a_hi.astype(F32)).astype(BF16)
    b_hi = b.astype(BF16)
    b_lo = (b - b_hi.astype(F32)).astype(BF16)
    d = functools.partial(lax.dot_general, dimension_numbers=dims, preferred_element_type=F32)
    return d(a_hi, b_hi) + d(a_lo, b_hi) + d(a_hi, b_lo)


def _dot1(a, b, dims=(((1,), (0,)), ((), ()))):
    return lax.dot_general(a.astype(BF16), b.astype(BF16), dims, preferred_element_type=F32)


_NT = (((1,), (1,)), ((), ()))
_TN = (((0,), (0,)), ((), ()))
_WKV_DOT = _dot1


def _wkv_seq_kernel(hs, r_ref, k_ref, v_ref, a_ref, ld_ref, kkw_ref, kaw_ref, rkw_ref, lnw_ref, lnb_ref,
                    s0_ref, y_o, s_o, st_scr):
    t = pl.program_id(1)
    tt, rw = r_ref.shape
    c = CHUNK
    pack = WKV_HEAD_PACK
    pw_lanes = pack * hs
    n_pairs = rw // pw_lanes
    assert c == hs, "chunk length equals the head size in the packed layout"
    shift = hs.bit_length() - 1
    assert 1 << shift == hs

    lane_head = lax.broadcasted_iota(jnp.int32, (1, pw_lanes), 1) >> shift
    row_p = lax.broadcasted_iota(jnp.int32, (pw_lanes, pw_lanes), 0)
    col_p = lax.broadcasted_iota(jnp.int32, (pw_lanes, pw_lanes), 1)
    same_head = (row_p >> shift) == (col_p >> shift)
    head_ones = jnp.where(same_head, 1.0, 0.0).astype(BF16)
    eye_p = (row_p == col_p).astype(F32)
    row_c = lax.broadcasted_iota(jnp.int32, (c, pw_lanes), 0)
    col_c = lax.broadcasted_iota(jnp.int32, (c, pw_lanes), 1) & (hs - 1)
    tril_strict = row_c > col_c
    tril_incl = row_c >= col_c
    eye_c = (row_c == col_c).astype(F32)
    row_t = lax.broadcasted_iota(jnp.int32, (c, 3 * c), 0)
    col_t = lax.broadcasted_iota(jnp.int32, (c, 3 * c), 1)
    col_t = jnp.where(col_t < c, col_t, jnp.where(col_t < 2 * c, col_t - c, col_t - 2 * c))
    tril3 = jnp.where(row_t >= col_t, 1.0, 0.0).astype(BF16)
    ones2 = jnp.concatenate([head_ones] * 2, axis=0)

    def blockdiag(x):
        return jnp.concatenate([jnp.where(lane_head == i, x, 0.0) for i in range(pack)], axis=0)

    def head_sums(x):
        hi = x.astype(BF16)
        lo = (x - hi.astype(F32)).astype(BF16)
        return jnp.dot(jnp.concatenate([hi, lo], axis=1), ones2, preferred_element_type=F32)

    @pl.when(t == 0)
    def _():
        z = jnp.zeros((hs, hs), F32)
        for p in range(n_pairs):
            blk = jnp.concatenate(
                [jnp.concatenate([s0_ref[pack * p + i] if j == i else z for j in range(pack)], axis=1)
                 for i in range(pack)], axis=0)
            st_scr[p] = blk.T

    n_chunks = tt // c
    cat = jnp.concatenate
    dot = _WKV_DOT

    def each(f, *lists):
        return [f(*xs) for xs in zip(*lists)]

    def cumsum_rows(x):
        hi, mid, lo = _split3(x)
        return jnp.dot(tril3, cat([hi, mid, lo], axis=0), preferred_element_type=F32)

    def prepare(chunks, out):
        items = [(ci, p) for ci in chunks for p in range(n_pairs)]
        tiles = lambda ref: [ref[ci * c:(ci + 1) * c, p * pw_lanes:(p + 1) * pw_lanes] for ci, p in items]
        params = lambda ref: [ref[:, p * pw_lanes:(p + 1) * pw_lanes] for _, p in items]
        r, k, v, a_sig, ld = (tiles(ref) for ref in (r_ref, k_ref, v_ref, a_ref, ld_ref))
        kkw, kaw, rkw = (params(ref) for ref in (kkw_ref, kaw_ref, rkw_ref))
        kk = each(lambda k_, w_: k_ * w_, k, kkw)
        kmod = each(lambda k_, a_, w_: k_ * (1.0 + (a_ - 1.0) * w_), k, a_sig, kaw)
        sums = each(lambda kk_, r_, km_, w_: head_sums(cat([kk_ * kk_, r_ * km_ * w_], axis=0)), kk, r, kmod, rkw)
        yield
        kk = each(lambda kk_, s_: kk_ / jnp.maximum(jnp.sqrt(s_[0:c]), KK_NORM_FLOOR), kk, sums)
        b = each(lambda kk_, a_: kk_ * a_, kk, a_sig)
        cum = each(cumsum_rows, ld)
        yield
        at = each(lambda kk_, cu, l_: -kk_ * jnp.exp(cu - l_), kk, cum, ld)
        rt = each(lambda r_, cu: r_ * jnp.exp(cu), r, cum)
        inv = each(lambda cu: jnp.exp(-cu), cum)
        bt = each(lambda b_, i_: b_ * i_, b, inv)
        kt = each(lambda km_, i_: km_ * i_, kmod, inv)
        yield
        to_end = each(lambda cu: jnp.exp(cu[c - 1:c, :] - cu), cum)
        bk_end = each(lambda b_, km_, te: cat([b_ * te, km_ * te], axis=0), b, kmod, to_end)
        p_end_col = each(lambda cu: jnp.sum(eye_p * jnp.exp(cu[c - 1:c, :]), axis=-1, keepdims=True), cum)
        gm = each(lambda at_, rt_, bt_, kt_: dot(cat([at_, rt_], axis=0),
                                                 cat([blockdiag(bt_), blockdiag(kt_)], axis=0), _NT),
                  at, rt, bt, kt)
        yield
        pc = pack * c
        l_ab = [jnp.where(tril_strict, g[0:c, 0:pc], 0.0) for g in gm]
        l_ak = [jnp.where(tril_strict, g[0:c, pc:2 * pc], 0.0) for g in gm]
        m_rb = [jnp.where(tril_incl, g[c:2 * c, 0:pc], 0.0) for g in gm]
        m_rk = [jnp.where(tril_incl, g[c:2 * c, pc:2 * pc], 0.0) for g in gm]
        t_inv = [eye_c + l_ for l_ in l_ab]
        pw = each(lambda q: dot(q, blockdiag(q)), l_ab)
        yield
        span = 4
        while span <= c:
            last = span == c
            both = each(lambda q, t_: dot(q, blockdiag(t_) if last else
                                          cat([blockdiag(t_), blockdiag(q)], axis=1)), pw, t_inv)
            t_inv = each(lambda t_, m_: t_ + m_[:, 0:pc], t_inv, both)
            if not last:
                pw = [m_[:, pc:2 * pc] for m_ in both]
            yield
            span *= 2
        out.update(items=items, r=r, v=v, kmod=kmod, bonus=[s_[c:2 * c] for s_ in sums], at=at, rt=rt,
                   l_ak=l_ak, m_rb=m_rb, m_rk=m_rk, t_inv=t_inv, bk_end=bk_end, p_end_col=p_end_col)

    def state_pass(d):
        items = d["items"]
        y = []
        for i0 in range(0, len(items), n_pairs):
            idx = range(i0, i0 + n_pairs)
            st = [st_scr[p] for p in range(n_pairs)]
            vb = [blockdiag(d["v"][i]) for i in idx]
            x = [dot(cat([d["at"][i], d["l_ak"][i]], axis=1), cat([st[p], vb[p]], axis=0))
                 for p, i in enumerate(idx)]
            yield
            u = [dot(d["t_inv"][i], blockdiag(x[p])) for p, i in enumerate(idx)]
            yield
            y += [dot(cat([d["rt"][i], d["m_rk"][i], d["m_rb"][i]], axis=1),
                      cat([st[p], vb[p], blockdiag(u[p])], axis=0)) for p, i in enumerate(idx)]
            upd = [dot(d["bk_end"][i], cat([u[p], d["v"][i]], axis=0), _TN) for p, i in enumerate(idx)]
            for p, i in enumerate(idx):
                st_scr[p] = d["p_end_col"][i] * st[p] + jnp.where(same_head, upd[p], 0.0)
            yield
        s2 = each(lambda y_: head_sums(cat([y_, y_ * y_], axis=0)), y)
        yield
        for (ci, p), y_, s_, bo_, v_ in zip(items, y, s2, d["bonus"], d["v"]):
            ln = slice(p * pw_lanes, (p + 1) * pw_lanes)
            mean = s_[0:c] * (1.0 / hs)
            var = s_[c:2 * c] * (1.0 / hs) - mean * mean
            yn = (y_ - mean) * lax.rsqrt(var + hs * GROUPNORM_EPS_PER_CHANNEL) * lnw_ref[:, ln] + lnb_ref[:, ln]
            y_o[ci * c:(ci + 1) * c, ln] = yn + bo_ * v_

    def run_together(*progs):
        progs = list(progs)
        while progs:
            for prog in list(progs):
                if next(prog, StopIteration) is StopIteration:
                    progs.remove(prog)

    halves = [range(h, min(h + WKV_BATCH_CHUNKS, n_chunks)) for h in range(0, n_chunks, WKV_BATCH_CHUNKS)]
    prev = None
    for chunks in halves:
        cur = {}
        if prev is None:
            run_together(prepare(chunks, cur))
        else:
            run_together(prepare(chunks, cur), state_pass(prev))
        prev = cur
    run_together(state_pass(prev))

    @pl.when(t == pl.num_programs(1) - 1)
    def _():
        for p in range(n_pairs):
            s_grp = st_scr[p].T
            for i in range(pack):
                s_o[pack * p + i] = s_grp[i * hs:(i + 1) * hs, i * hs:(i + 1) * hs]


def _mix_out_attn_seq_kernel(n_xheads, x_ref, y_ref, g_ref, yb_ref, wout_ref, gx_ref, wq_ref,
                             mk_ref, mv_ref, wo_ref, x2_o):
    tm, d = x_ref.shape
    xd = d // n_xheads
    scale = xd ** -0.5
    n_sub = max(1, tm // MIX_SUB_ROWS)
    sub = tm // n_sub
    mk = mk_ref[...].astype(BF16)
    mv = mv_ref[...].astype(BF16)

    def sub_tile(i):
        rows = slice(i * sub, (i + 1) * sub)
        mixed = jnp.concatenate([y_ref[rows, :] * g_ref[rows, :], yb_ref[rows, :]], axis=-1)
        x1 = x_ref[rows, :] + _bdot(mixed, wout_ref[...])
        yield
        q = _bdot(_rms(x1, gx_ref[...]), wq_ref[...]).astype(BF16)
        yield
        heads = [slice(h * xd, (h + 1) * xd) for h in range(n_xheads)]
        s = [lax.dot_general(q[:, sl], mk[:, sl], _NT, preferred_element_type=F32) * scale for sl in heads]
        yield
        p = [_softmax_rows(s_).astype(BF16) for s_ in s]
        yield
        o = jnp.concatenate([jnp.dot(p_, mv[:, sl], preferred_element_type=F32) for p_, sl in zip(p, heads)],
                            axis=-1)
        yield
        x2_o[rows, :] = x1 + _bdot(o, wo_ref[...])

    _run_together(*[sub_tile(i) for i in range(n_sub)])


def _ffn_seq_kernel(final, *refs):
    it = iter(refs)
    x_ref, gf_ref, wup_ref, fprev_ref, fcw_ref, wdown_ref = (next(it) for _ in range(6))
    gfin_ref = next(it) if final else None
    q_ref, mk_ref, mv_ref = (next(it) for _ in range(3))
    y_o, fnew_o, o_o, carry = (next(it) for _ in range(4))

    t = pl.program_id(1)
    tm = x_ref.shape[0]
    f = wdown_ref.shape[0]
    fc = FFN_COL_TILE

    @pl.when(t == 0)
    def _():
        carry[0:2, :] = fprev_ref[...]

    n_sub = max(1, tm // FFN_SUB_ROWS)
    sub = tm // n_sub
    n_tiles = f // fc
    xs, xfs = {}, {}

    def up_item(s, j):
        if s not in xs:
            xs[s] = x_ref[s * sub:(s + 1) * sub, :]
            xfs[s] = _rms(xs[s], gf_ref[...]).astype(BF16)
        c0 = j * fc
        return (jnp.dot(xfs[s], wup_ref[:, c0:c0 + fc], preferred_element_type=F32),
                jnp.dot(xfs[s], wup_ref[:, f + c0:f + c0 + fc], preferred_element_type=F32))

    def conv_cols(up, c0):
        p0 = carry[0:1, c0:c0 + fc]
        p1 = carry[1:2, c0:c0 + fc]
        up1 = _shift_rows(up, [p1])
        up2 = _shift_rows(up, [p0, p1])
        carry[0:2, c0:c0 + fc] = up[sub - 2:sub, :]
        return (fcw_ref[0:1, c0:c0 + fc] * up2 + fcw_ref[1:2, c0:c0 + fc] * up1
                + fcw_ref[2:3, c0:c0 + fc] * up)

    items = [(s, j) for s in range(n_sub) for j in range(n_tiles)]
    ahead = [up_item(*it_) for it_ in items[:FFN_LOOKAHEAD]]
    attention = _sample_attention_stages(q_ref, mk_ref, mv_ref, o_o)
    acc = None
    for idx, (s, j) in enumerate(items):
        c0 = j * fc
        cur = ahead.pop(0)
        if idx + FFN_LOOKAHEAD < len(items):
            ahead.append(up_item(*items[idx + FFN_LOOKAHEAD]))
        if j == 0:
            acc = xs[s]
        u = conv_cols(cur[0], c0)
        gt = conv_cols(cur[1], f + c0)
        hidden = gt * _sigmoid(gt) * u
        acc = acc + _bdot(hidden, wdown_ref[c0:c0 + fc, :])
        next(attention, None)
        if j == n_tiles - 1:
            if final:
                acc = _rms(acc, gfin_ref[...])
            y_o[s * sub:(s + 1) * sub, :] = acc
    for _ in attention:
        pass

    @pl.when(t == pl.num_programs(1) - 1)
    def _():
        fnew_o[...] = carry[0:2, :]


def _mem_kv_kernel(m_ref, g_ref, wk_ref, wv_ref, k_o, v_o, kh_o, vh_o):
    mn = _rms(m_ref[...], g_ref[...])
    nseq, m, nh, xd = kh_o.shape
    for w_ref, flat_o, heads_o in ((wk_ref, k_o, kh_o), (wv_ref, v_o, vh_o)):
        val = _bdot(mn, w_ref[...])
        flat_o[...] = val
        for s in range(nseq):
            for h in range(nh):
                heads_o[s, :, h, :] = val[s * m:(s + 1) * m, h * xd:(h + 1) * xd]


def _mix_in_step_kernel(has_vmix, lora_dims, *refs):
    it = iter(refs)
    x_ref, gmix_ref, win_ref, sprev_ref, cprev_ref, mu_ref, w0_ref, a0_ref, wl_ref, cw_ref = (
        next(it) for _ in range(10))
    if has_vmix:
        vf_ref, v0_ref, v1_ref, v2_ref = (next(it) for _ in range(4))
    r_o, k_o, v_o, a_o, ld_o, vrow_o, g_o, yb_o, xn_o, cnew_o = (next(it) for _ in range(10))

    nb = x_ref.shape[0]
    rc = mu_ref.shape[1]
    cw = cw_ref.shape[1]
    rw = w0_ref.shape[1]

    xn = _rms(x_ref[...], gmix_ref[...])
    xn_o[...] = xn
    stacked = jnp.concatenate([xn, sprev_ref[...]], axis=0)
    proj = _bdot(stacked, win_ref[...])
    p_cur = proj[0:nb, 0:rc]
    p_shift = proj[nb:2 * nb, 0:rc]
    vmix = None
    if has_vmix:
        vmix = (vf_ref[...], v0_ref[...], v1_ref[...], v2_ref[...])
    r, k, v, a_sig, log_decay, gate = _mix_rows(
        p_cur, p_shift, mu_ref[...], w0_ref[...], a0_ref[...], wl_ref[...], lora_dims, rw, vmix)
    r_o[...] = r.T
    k_o[...] = k.T
    v_o[...] = v.T
    a_o[...] = a_sig.T
    ld_o[...] = log_decay.T
    vrow_o[...] = v
    g_o[...] = gate

    gate_b = proj[0:nb, rc:rc + cw]
    gate_c = proj[0:nb, rc + cw:rc + 2 * cw]
    h_in = proj[0:nb, rc + 2 * cw:rc + 3 * cw]
    u = gate_c * h_in
    u2 = cprev_ref[:, 0, :]
    u1 = cprev_ref[:, 1, :]
    yb_o[...] = gate_b * (cw_ref[0:1, :] * u2 + cw_ref[1:2, :] * u1 + cw_ref[2:3, :] * u)
    cnew_o[:, 0, :] = u1
    cnew_o[:, 1, :] = u


def _wkv_step_kernel(aliased, *refs):
    r_ref, k_ref, v_ref, a_ref, ld_ref, kkw_ref, kaw_ref, rkw_ref, lnw_ref, lnb_ref, s_ref = refs[:11]
    y_o, s_o, y_scr = refs[12:] if aliased else refs[11:]
    n = s_ref.shape[0]

    @pl.when(pl.program_id(0) == 0)
    def _():
        r, k, v, a_sig, ld = r_ref[...], k_ref[...], v_ref[...], a_ref[...], ld_ref[...]
        kk = k * kkw_ref[...]
        kk = kk / jnp.maximum(jnp.sqrt(jnp.sum(kk * kk, axis=0, keepdims=True)), KK_NORM_FLOOR)
        kmod = k * (1.0 + (a_sig - 1.0) * kaw_ref[...])
        a = -kk
        b = kk * a_sig
        decay = jnp.exp(ld)
        for vi in range(n):
            s = s_ref[vi]
            sa = jnp.sum(s * a, axis=0, keepdims=True)
            s_new = s * decay + sa * b + v[vi:vi + 1, :] * kmod
            s_o[vi] = s_new
            y_scr[vi:vi + 1, :] = jnp.sum(s_new * r, axis=0, keepdims=True)
        y = y_scr[...]
        mean = jnp.mean(y, axis=0, keepdims=True)
        yc = y - mean
        var = jnp.mean(yc * yc, axis=0, keepdims=True)
        y = yc * lax.rsqrt(var + n * GROUPNORM_EPS_PER_CHANNEL) * lnw_ref[...] + lnb_ref[...]
        y_o[...] = y + jnp.sum(r * kmod * rkw_ref[...], axis=0, keepdims=True) * v

    @pl.when(pl.program_id(0) > 0)
    def _():
        s_o[...] = jnp.zeros(s_o.shape, F32)


def _mix_out_q_step_kernel(x_ref, ya_ref, g_ref, yb_ref, wout_ref, gx_ref, wq_ref, x1_o, q_o):
    mixed = jnp.concatenate([ya_ref[...].T * g_ref[...], yb_ref[...]], axis=-1)
    x1 = x_ref[...] + _bdot(mixed, wout_ref[...])
    x1_o[...] = x1
    q_o[...] = _bdot(_rms(x1, gx_ref[...]), wq_ref[...])


def _sample_attention_stages(q_ref, mk_ref, mv_ref, o_o):
    bt = q_ref.shape[0]
    m, nh, xd = mk_ref.shape[1:]
    scale = xd ** -0.5
    fold = SUBLANES // nh if SUBLANES % nh == 0 and m % max(SUBLANES // nh, 1) == 0 else 1

    def folded(x):
        return [x[j * nh:(j + 1) * nh] for j in range(fold)]

    for i in range(bt):
        q = jnp.concatenate([q_ref[i]] * fold, axis=0)
        k = mk_ref[i].reshape(m // fold, fold * nh, xd)
        s = jnp.sum(k * q, axis=-1, keepdims=True) * scale
        yield
        mx = functools.reduce(jnp.maximum, folded(jnp.max(s, axis=0)))
        e = jnp.exp(s - jnp.concatenate([mx] * fold, axis=0))
        den = sum(folded(jnp.sum(e, axis=0)))
        yield
        v = mv_ref[i].reshape(m // fold, fold * nh, xd)
        acc = sum(folded(jnp.sum(e * v, axis=0)))
        o_o[i] = acc / den
        yield


def _ffn_step_kernel(final, aliased, *refs):
    it = iter(refs)
    x1_ref, o_ref, wo_ref, gf_ref, wup_ref, fprev_ref, fcw_ref, wdown_ref = (next(it) for _ in range(8))
    gfin_ref = next(it) if final else None
    if aliased:
        next(it)
    y_o, fnew_o = (next(it) for _ in range(2))
    f = wdown_ref.shape[0]
    f2 = 2 * f

    @pl.when(pl.program_id(0) == 0)
    def _():
        x2 = x1_ref[...] + _bdot(o_ref[...], wo_ref[...])
        up = _bdot(_rms(x2, gf_ref[...]), wup_ref[...])
        up2 = fprev_ref[:, 0, :]
        up1 = fprev_ref[:, 1, :]
        upc = fcw_ref[0:1, :] * up2 + fcw_ref[1:2, :] * up1 + fcw_ref[2:3, :] * up
        fnew_o[:, 0, :] = up1
        fnew_o[:, 1, :] = up
        u = upc[:, 0:f]
        gt = upc[:, f:f2]
        x3 = x2 + _bdot(gt * _sigmoid(gt) * u, wdown_ref[...])
        if final:
            x3 = _rms(x3, gfin_ref[...])
        y_o[...] = x3

    @pl.when(pl.program_id(0) > 0)
    def _():
        fnew_o[...] = jnp.zeros(fnew_o.shape, F32)


def _params(*sem):
    return pltpu.CompilerParams(dimension_semantics=sem, vmem_limit_bytes=V7X_VMEM_LIMIT_BYTES)


class _Layered(NamedTuple):
    stacked: jax.Array
    layer: int

    @property
    def shape(self):
        return self.stacked.shape[1:]


def _operand(x):
    return x.stacked if isinstance(x, _Layered) else x


def _whole(x):
    if isinstance(x, _Layered):
        nd, l = x.stacked.ndim, x.layer
        return pl.BlockSpec((None,) + x.shape, lambda *_: (l,) + (0,) * (nd - 1), pipeline_mode=pl.Buffered(1))
    nd = x.ndim
    return pl.BlockSpec(x.shape, lambda *_: (0,) * nd, pipeline_mode=pl.Buffered(1))


def _sds(shape):
    return jax.ShapeDtypeStruct(shape, F32)


def _stacked_weights(p):
    depth = p["w_in"].shape[0]
    rows = lambda a: a.reshape(a.shape[0], 1, -1)
    rw = p["w0"].shape[1]
    d_decay, d_aaa, d_gate = p["w2"].shape[1], p["a2"].shape[1], p["g2"].shape[1]
    pad_cols = lambda a, before, after: jnp.pad(a, ((0, 0), (0, 0), (before, after)))
    w_lora = jnp.concatenate([pad_cols(p["w2"], 0, 2 * rw), pad_cols(p["a2"], rw, rw), pad_cols(p["g2"], 2 * rw, 0)],
                             axis=1)
    lanes = 128
    mv = p["v1"].shape[2]
    mvp = -(-mv // lanes) * lanes
    w = dict(
        norm_mix=rows(p["norm_mix"]), w_in=p["w_in"].astype(BF16), mu=rows(p["mu_shift"]), w0=rows(p["w0"]),
        a0=rows(p["a0"]), w_lora=w_lora.astype(BF16), conv_w=p["conv_w"], k_k=p["k_k"], k_a=p["k_a"],
        r_k=p["r_k"], ln_w=p["ln_x_w"], ln_b=p["ln_x_b"], w_out=p["w_out"].astype(BF16), norm_x=rows(p["norm_x"]),
        wq=p["wq"].astype(BF16), wo=p["wo"].astype(BF16), norm_ffn=rows(p["norm_ffn"]),
        w_up=p["w_up"].astype(BF16), ffn_conv_w=p["ffn_conv_w"], w_down=p["w_down"].astype(BF16),
        norm_mem=rows(p["norm_mem"]), wk=p["wk"].astype(BF16), wv=p["wv"].astype(BF16),
    )
    layers = []
    for l in range(depth):
        wl = {k_: _Layered(v_, l) for k_, v_ in w.items()}
        wl["norm_final"] = p["norm_final"].reshape(1, -1)
        wl["lora_dims"] = (d_decay, d_aaa, d_gate)
        if l > 0:
            wl["v0"] = _Layered(rows(p["v0"]), l - 1)
            wl["v1"] = _Layered(jnp.pad(p["v1"], ((0, 0), (0, 0), (0, mvp - mv))).astype(BF16), l - 1)
            wl["v2"] = _Layered(jnp.pad(p["v2"], ((0, 0), (0, mvp - mv), (0, 0))).astype(BF16), l - 1)
        layers.append(wl)
    return layers


def _head_rows(x, n_heads):
    st = x.stacked
    return _Layered(st.reshape(st.shape[0], n_heads, st.shape[1] // n_heads, 1), x.layer)


def _mem_kv(mem, w, n_xheads):
    bsz, m, d = mem.shape
    depth = w["wk"].stacked.shape[0]
    nseq = max(1, min(bsz, 512 // m))
    xd = d // n_xheads
    stacked = lambda a: pl.BlockSpec((None,) + a.shape, lambda l, i: (l,) + (0,) * len(a.shape))
    flat = pl.BlockSpec((None, nseq * m, d), lambda l, i: (l, i, 0))
    heads = pl.BlockSpec((None, nseq, m, n_xheads, xd), lambda l, i: (l, i, 0, 0, 0))
    args = [mem.reshape(bsz * m, d), w["norm_mem"], w["wk"], w["wv"]]
    k, v, kh, vh = pl.pallas_call(
        _mem_kv_kernel,
        grid=(depth, bsz // nseq),
        in_specs=[pl.BlockSpec((nseq * m, d), lambda l, i: (i, 0))] + [stacked(a) for a in args[1:]],
        out_specs=[flat, flat, heads, heads],
        out_shape=[_sds((depth, bsz * m, d))] * 2 + [_sds((depth, bsz, m, n_xheads, xd))] * 2,
        compiler_params=_params("arbitrary", "arbitrary"),
        name="mem_kv",
    )(*map(_operand, args))
    return k.reshape(depth, bsz, m, d), v.reshape(depth, bsz, m, d), kh, vh


def _prompt_mixer(l, x, mem_k, mem_v, shift0, wkv0, conv0, v_first, w, n_heads, n_xheads):
    shift0, wkv0, conv0 = (a[l] for a in (shift0, wkv0, conv0))
    bsz, t, d = x.shape
    rw = w["w0"].shape[1]
    hs = rw // n_heads
    cw = w["conv_w"].shape[1]
    rc = w["mu"].shape[1]
    tm = min(ROW_TILE, t)
    nt = t // tm
    has_vmix = v_first is not None

    tile = lambda n: pl.BlockSpec((None, tm, n), lambda b, i: (b, i, 0))
    per_b = lambda s: pl.BlockSpec((None,) + s, lambda b, i: (b,) + (0,) * len(s))

    ins = [x, w["norm_mix"], w["w_in"], shift0[:, None, :], conv0, w["mu"], w["w0"], w["a0"], w["w_lora"],
           w["conv_w"]]
    specs = [tile(d), _whole(w["norm_mix"]), _whole(w["w_in"]), per_b((1, d)), per_b((2, cw)),
             _whole(w["mu"]), _whole(w["w0"]), _whole(w["a0"]), _whole(w["w_lora"]), _whole(w["conv_w"])]
    if has_vmix:
        ins += [v_first, w["v0"], w["v1"], w["v2"]]
        specs += [tile(rw), _whole(w["v0"]), _whole(w["v1"]), _whole(w["v2"])]
    out_shape = [_sds((bsz, t, rw))] * 6 + [_sds((bsz, t, cw)), _sds((bsz, 1, d)), _sds((bsz, 2, cw))]
    out_specs = [tile(rw)] * 6 + [tile(cw), per_b((1, d)), per_b((2, cw))]
    r, k, v, a_sig, ld, gate, y_b, xn_last, conv_new = pl.pallas_call(
        functools.partial(_mix_in_seq_kernel, has_vmix, w["lora_dims"]),
        grid=(bsz, nt), in_specs=specs, out_specs=out_specs, out_shape=out_shape,
        scratch_shapes=[pltpu.VMEM((8, rc), F32), pltpu.VMEM((8, cw), F32)],
        compiler_params=_params("parallel", "arbitrary"),
        name="mix_in_seq",
    )(*map(_operand, ins))
    if not has_vmix:
        v_first = v

    head_params = [_Layered(a.stacked[:, None, :], a.layer) for a in
                   (w["k_k"], w["k_a"], w["r_k"], w["ln_w"], w["ln_b"])]
    sblk = per_b((n_heads, hs, hs))
    wt = min(WKV_ROW_TILE, t)
    wtile = pl.BlockSpec((None, wt, rw), lambda b, i: (b, i, 0))
    y_a, wkv_new = pl.pallas_call(
        functools.partial(_wkv_seq_kernel, hs),
        grid=(bsz, t // wt),
        in_specs=[wtile] * 5 + [_whole(a) for a in head_params] + [sblk],
        out_specs=[wtile, sblk],
        out_shape=[_sds((bsz, t, rw)), _sds((bsz, n_heads, hs, hs))],
        scratch_shapes=[pltpu.VMEM((rw // (WKV_HEAD_PACK * hs), WKV_HEAD_PACK * hs, WKV_HEAD_PACK * hs), F32)],
        compiler_params=_params("parallel", "arbitrary"),
        name="wkv_seq",
    )(r, k, v, a_sig, ld, *map(_operand, head_params), wkv0)

    mblk = pl.BlockSpec((None, None) + mem_k.shape[2:], lambda b, i: (l, b, 0, 0))
    x2 = pl.pallas_call(
        functools.partial(_mix_out_attn_seq_kernel, n_xheads),
        grid=(bsz, nt),
        in_specs=[tile(d), tile(rw), tile(rw), tile(cw), _whole(w["w_out"]), _whole(w["norm_x"]), _whole(w["wq"]),
                  mblk, mblk, _whole(w["wo"])],
        out_specs=tile(d), out_shape=_sds((bsz, t, d)),
        compiler_params=_params("parallel", "arbitrary"),
        name="mix_out_attn_seq",
    )(*map(_operand, (x, y_a, gate, y_b, w["w_out"], w["norm_x"], w["wq"], mem_k, mem_v, w["wo"])))

    return x2, xn_last[:, 0, :], wkv_new, conv_new, v_first


def _prompt_ffn(l, x2, ffn0, q, mem_k, mem_v, w, final):
    bsz, t, d = x2.shape
    ffn0 = ffn0[l]
    f2 = w["w_up"].shape[1]
    ft = min(FFN_ROW_TILE, t)
    nt = t // ft
    nb, n_xheads, xd = q.shape
    m = mem_k.shape[2]
    assert nb % (bsz * nt) == 0, "sample sequences must split evenly over the FFN grid steps"
    ab = nb // (bsz * nt)
    per_b = lambda s: pl.BlockSpec((None,) + s, lambda b, i: (b,) + (0,) * len(s))
    ftile = pl.BlockSpec((None, ft, d), lambda b, i: (b, i, 0))
    qblk = pl.BlockSpec((ab, n_xheads, xd), lambda b, i: (b * nt + i, 0, 0))
    mblk = pl.BlockSpec((None, ab, m, n_xheads, xd), lambda b, i: (l, b * nt + i, 0, 0, 0))
    ins = [x2, w["norm_ffn"], w["w_up"], ffn0, w["ffn_conv_w"], w["w_down"]]
    specs = [ftile, _whole(w["norm_ffn"]), _whole(w["w_up"]), per_b((2, f2)), _whole(w["ffn_conv_w"]),
             _whole(w["w_down"])]
    if final:
        ins.append(w["norm_final"])
        specs.append(_whole(w["norm_final"]))
    ins += [q, mem_k, mem_v]
    specs += [qblk, mblk, mblk]
    x3, ffn_new, o = pl.pallas_call(
        functools.partial(_ffn_seq_kernel, final),
        grid=(bsz, nt), in_specs=specs,
        out_specs=[ftile, per_b((2, f2)), qblk],
        out_shape=[_sds((bsz, t, d)), _sds((bsz, 2, f2)), _sds((nb, n_xheads, xd))],
        scratch_shapes=[pltpu.VMEM((8, f2), F32)],
        compiler_params=_params("arbitrary", "arbitrary"),
        name="ffn_seq",
    )(*map(_operand, ins))
    return x3, ffn_new, o


def _sample_mixer(l, x, shift0, wkv_t, conv0, carried, w, n_heads, n_xheads):
    v_first, wkv_all = carried
    nb, d = x.shape
    rw = w["w0"].shape[1]
    hs = rw // n_heads
    cw = w["conv_w"].shape[1]
    has_vmix = v_first is not None
    one = lambda a: pl.BlockSpec(a.shape, lambda i: (0,) * a.ndim, pipeline_mode=pl.Buffered(1))
    layer_of = lambda a: pl.BlockSpec((None,) + a.shape[1:], lambda i: (l,) + (0,) * (a.ndim - 1),
                                      pipeline_mode=pl.Buffered(1))

    ins = [x, w["norm_mix"], w["w_in"], shift0, conv0, w["mu"], w["w0"], w["a0"], w["w_lora"], w["conv_w"]]
    specs = [one(x), _whole(w["norm_mix"]), _whole(w["w_in"]), layer_of(shift0), layer_of(conv0)] + [
        _whole(a) for a in ins[5:]]
    if has_vmix:
        ins += [v_first, w["v0"], w["v1"], w["v2"]]
        specs += [one(v_first), _whole(w["v0"]), _whole(w["v1"]), _whole(w["v2"])]
    out_shape = [_sds((rw, nb))] * 5 + [_sds((nb, rw))] * 2 + [_sds((nb, cw)), _sds((nb, d)), _sds((nb, 2, cw))]
    r, k, v, a_sig, ld, v_rows, gate, y_b, xn, conv_new = pl.pallas_call(
        functools.partial(_mix_in_step_kernel, has_vmix, w["lora_dims"]),
        grid=(1,), in_specs=specs, out_specs=[one(s_) for s_ in out_shape], out_shape=out_shape,
        compiler_params=_params("arbitrary"),
        name="mix_in_step",
    )(*map(_operand, ins))
    if not has_vmix:
        v_first = v_rows

    aliased = wkv_all is not None
    n_clear = 0 if aliased else wkv_t.shape[0] - 1 - l
    last = n_heads - 1
    hsel = lambda c, h: jnp.where(c == 0, h, last)
    heads = lambda a: a.reshape(n_heads, hs, nb)
    hblk = pl.BlockSpec((None, hs, nb), lambda c, h: (hsel(c, h), 0, 0))
    head_params = [_head_rows(w[n_], n_heads) for n_ in ("k_k", "k_a", "r_k", "ln_w", "ln_b")]
    pblk = pl.BlockSpec((None, None, hs, 1), lambda c, h: (l, hsel(c, h), 0, 0))
    sblk = pl.BlockSpec((None, None, hs, hs, nb), lambda c, h: (l, hsel(c, h), 0, 0, 0))
    soblk = pl.BlockSpec((None, None, hs, hs, nb), lambda c, h: (l + c, h, 0, 0, 0))
    ins = [heads(r), heads(k), heads(v), heads(a_sig), heads(ld)] + [a.stacked for a in head_params] + [wkv_t]
    specs = [hblk] * 5 + [pblk] * 5 + [sblk]
    if aliased:
        ins.append(wkv_all)
        specs.append(pl.BlockSpec(memory_space=pl.ANY))
    y_a, wkv_all = pl.pallas_call(
        functools.partial(_wkv_step_kernel, aliased),
        grid=(1 + n_clear, n_heads),
        in_specs=specs, out_specs=[hblk, soblk],
        out_shape=[_sds((n_heads, hs, nb)), _sds(wkv_t.shape)],
        scratch_shapes=[pltpu.VMEM((hs, nb), F32)],
        input_output_aliases={len(ins) - 1: 1} if aliased else {},
        compiler_params=_params("arbitrary", "arbitrary"),
        name="wkv_step",
    )(*ins)

    ins = [x, y_a.reshape(rw, nb), gate, y_b, w["w_out"], w["norm_x"], w["wq"]]
    x1, q = pl.pallas_call(
        _mix_out_q_step_kernel,
        grid=(1,), in_specs=[one(a) for a in ins[:4]] + [_whole(a) for a in ins[4:]],
        out_specs=[one(x)] * 2, out_shape=[_sds((nb, d))] * 2,
        compiler_params=_params("arbitrary"),
        name="mix_out_q_step",
    )(*map(_operand, ins))

    xd = d // n_xheads
    return x1, q.reshape(nb, n_xheads, xd), xn, conv_new, (v_first, wkv_all)


def _sample_ffn(l, x1, o, ffn0, ffn_all, w, final):
    nb, d = x1.shape
    o = o.reshape(nb, d)
    one = lambda a: pl.BlockSpec(a.shape, lambda i: (0,) * a.ndim, pipeline_mode=pl.Buffered(1))
    ins = [x1, o, w["wo"], w["norm_ffn"], w["w_up"], ffn0, w["ffn_conv_w"], w["w_down"]]
    fblk = pl.BlockSpec((None,) + ffn0.shape[1:], lambda c: (l, 0, 0, 0))
    foblk = pl.BlockSpec((None,) + ffn0.shape[1:], lambda c: (l + c, 0, 0, 0))
    specs = [one(x1), one(o), _whole(w["wo"]), _whole(w["norm_ffn"]), _whole(w["w_up"]), fblk,
             _whole(w["ffn_conv_w"]), _whole(w["w_down"])]
    if final:
        ins.append(w["norm_final"])
        specs.append(_whole(w["norm_final"]))
    ffn_aliased = ffn_all is not None
    if ffn_aliased:
        ins.append(ffn_all)
        specs.append(pl.BlockSpec(memory_space=pl.ANY))
    x3, ffn_all = pl.pallas_call(
        functools.partial(_ffn_step_kernel, final, ffn_aliased),
        grid=(1 if ffn_aliased else ffn0.shape[0] - l,), in_specs=specs, out_specs=[one(x1), foblk],
        out_shape=[_sds((nb, d)), _sds(ffn0.shape)],
        input_output_aliases={len(ins) - 1: 1} if ffn_aliased else {},
        compiler_params=_params("arbitrary"),
        name="ffn_step",
    )(*map(_operand, ins))
    return x3, ffn_all


def kernel(x_prompt, x_sample, mem_prompt, state_shift, state_wkv, state_conv, state_ffn, cache_mem_k, cache_mem_v, norm_mix, w_in, mu_shift, w0, w2, a0, a2, g2, v0, v1, v2, k_k, k_a, r_k, ln_x_w, ln_x_b, conv_w, w_out, norm_x, norm_mem, wq, wk, wv, wo, norm_ffn, w_up, ffn_conv_w, w_down, norm_final):
    p = dict(norm_mix=norm_mix, w_in=w_in, mu_shift=mu_shift, w0=w0, w2=w2, a0=a0, a2=a2, g2=g2, v0=v0, v1=v1,
             v2=v2, k_k=k_k, k_a=k_a, r_k=r_k, ln_x_w=ln_x_w, ln_x_b=ln_x_b, conv_w=conv_w, w_out=w_out,
             norm_x=norm_x, norm_mem=norm_mem, wq=wq, wk=wk, wv=wv, wo=wo, norm_ffn=norm_ffn, w_up=w_up,
             ffn_conv_w=ffn_conv_w, w_down=w_down, norm_final=norm_final)
    depth = w_in.shape[0]
    n_heads = state_wkv.shape[2]
    hs = state_wkv.shape[3]
    n_xheads, xd = cache_mem_k.shape[3], cache_mem_k.shape[4]
    assert x_sample.shape[1] == 1, "the sample group advances one token per sequence"
    weights = _stacked_weights(p)

    bp, tp, d = x_prompt.shape
    m = mem_prompt.shape[1]
    mks, mvs, mem_k_p, mem_v_p = _mem_kv(mem_prompt, weights[0], n_xheads)
    cw = conv_w.shape[2]
    f2 = w_up.shape[2]
    zeros = lambda *s_: jnp.zeros((depth, bp) + s_, F32)
    nb = x_sample.shape[0]
    wkv_t = jnp.transpose(state_wkv, (0, 2, 3, 4, 1))
    xp, v_first = x_prompt, None
    xs, carried, ffn_s = x_sample.reshape(nb, d), (None, None), None
    shs, wks, cvs, ffs, xns, cns = [], [], [], [], [], []
    for l in range(depth):
        final = l == depth - 1
        x2, sh, s_, cs, v_first = _prompt_mixer(l, xp, mks, mvs, zeros(d), zeros(n_heads, hs, hs),
                                                zeros(state_conv.shape[2], cw), v_first, weights[l],
                                                n_heads, n_xheads)
        x1, q, xn, cn, carried = _sample_mixer(l, xs, state_shift, wkv_t, state_conv, carried, weights[l],
                                               n_heads, n_xheads)
        xp, fs, o = _prompt_ffn(l, x2, zeros(state_ffn.shape[2], f2), q, cache_mem_k, cache_mem_v,
                                weights[l], final)
        xs, ffn_s = _sample_ffn(l, x1, o, state_ffn, ffn_s, weights[l], final)
        shs.append(sh)
        wks.append(s_)
        cvs.append(cs)
        ffs.append(fs)
        xns.append(xn)
        cns.append(cn)
    y_p, x = xp, xs
    wkv_s = jnp.transpose(carried[1], (0, 4, 1, 2, 3))
    return (y_p, x.reshape(x_sample.shape), jnp.stack(shs), jnp.stack(wks), jnp.stack(cvs), jnp.stack(ffs),
            mem_k_p, mem_v_p, jnp.stack(xns), wkv_s, jnp.stack(cns), ffn_s)
```

```python
import functools
from typing import NamedTuple

import jax
import jax.numpy as jnp
from jax import lax
from jax.experimental import pallas as pl
from jax.experimental.pallas import tpu as pltpu

F32 = jnp.float32
BF16 = jnp.bfloat16
HIGHEST = lax.Precision.HIGHEST

RMS_EPS = 1e-6
GROUPNORM_EPS_PER_CHANNEL = 1e-5
KK_NORM_FLOOR = 1e-12

ROW_TILE = 512
MIX_OUT_ROW_TILE = 1024
WKV_ROW_TILE = 1024
FFN_ROW_TILE = 512
FFN_SUB_ROWS = 256
MIX_SUB_ROWS = 256
CHUNK = 64
WKV_HEAD_PACK = 2
WKV_BATCH_CHUNKS = 4
WKV_BATCH_TILE = 8
FFN_COL_TILE = 256
FFN_LOOKAHEAD = 4
V7X_VMEM_LIMIT_BYTES = 56 * 1024 * 1024
SUBLANES = 8


def _rms(x, g):
    ms = jnp.mean(x * x, axis=-1, keepdims=True)
    return x * lax.rsqrt(ms + RMS_EPS) * g


def _bdot(a, b):
    return jnp.dot(a.astype(BF16), b.astype(BF16), preferred_element_type=F32)


def _hdot(a, b, dims=(((1,), (0,)), ((), ()))):
    return lax.dot_general(a, b, dims, precision=HIGHEST, preferred_element_type=F32)


def _sigmoid(x):
    return 1.0 / (1.0 + jnp.exp(-x))


def _softplus(x):
    return jnp.maximum(x, 0.0) + jnp.log1p(jnp.exp(-jnp.abs(x)))


def _shift_rows(x, prev_rows):
    s = len(prev_rows)
    rolled = pltpu.roll(x, s, axis=0)
    top = rolled[0:SUBLANES, :]
    row = lax.broadcasted_iota(jnp.int32, top.shape, 0)
    for i, pr in enumerate(prev_rows):
        top = jnp.where(row == i, pr, top)
    return jnp.concatenate([top, rolled[SUBLANES:, :]], axis=0)


def _mix_rows_stages(p_cur, p_shift, mu, w0, a0, w_lora, lora_dims, rw, vmix, out):
    d_decay, d_aaa, _ = lora_dims
    z = p_cur + (p_shift - p_cur) * mu
    r = z[:, 0:rw]
    k = z[:, rw:2 * rw]
    v = z[:, 2 * rw:3 * rw]
    lo = z[:, 3 * rw:]
    lane = lax.broadcasted_iota(jnp.int32, lo.shape, 1)
    feat = jnp.where(lane < d_decay, jnp.tanh(lo),
                     jnp.where(lane < d_decay + d_aaa, lo, _sigmoid(lo)))
    yield
    lora = _bdot(feat, w_lora)
    yield
    w = -_softplus(-(w0 + lora[:, 0:rw])) - 0.5
    log_decay = -jnp.exp(w)
    a_sig = _sigmoid(a0 + lora[:, rw:2 * rw])
    gate = lora[:, 2 * rw:3 * rw]
    if vmix is not None:
        v_first, v0, v1, v2 = vmix
        t = _bdot(v, v1)
        yield
        pre = _bdot(t, v2)
        yield
        v = v + (v_first - v) * _sigmoid(v0 + pre)
    out.extend([r, k, v, a_sig, log_decay, gate])


def _mix_rows(*args):
    out = []
    for _ in _mix_rows_stages(*args, out):
        pass
    return out


def _run_together(*progs):
    progs = list(progs)
    while progs:
        for prog in list(progs):
            if next(prog, StopIteration) is StopIteration:
                progs.remove(prog)


def _group_norm_bonus(y, r, kmod, v, rk, lnw, lnb, n):
    mean = jnp.mean(y, axis=-1, keepdims=True)
    yc = y - mean
    var = jnp.mean(yc * yc, axis=-1, keepdims=True)
    y = yc * lax.rsqrt(var + n * GROUPNORM_EPS_PER_CHANNEL) * lnw + lnb
    return y + jnp.sum(r * kmod * rk, axis=-1, keepdims=True) * v


def _key_features(k, a_sig, kk_w, ka_w):
    kk = k * kk_w
    nrm = jnp.sqrt(jnp.sum(kk * kk, axis=-1, keepdims=True))
    kk = kk / jnp.maximum(nrm, KK_NORM_FLOOR)
    kmod = k * (1.0 + (a_sig - 1.0) * ka_w)
    return kk, kmod


def _softmax_rows(s):
    m = jnp.max(s, axis=-1, keepdims=True)
    e = jnp.exp(s - m)
    return e / jnp.sum(e, axis=-1, keepdims=True)


def _mix_in_seq_kernel(has_vmix, lora_dims, *refs):
    it = iter(refs)
    x_ref, gmix_ref, win_ref, sprev_ref, cprev_ref, mu_ref, w0_ref, a0_ref, wl_ref, cw_ref = (
        next(it) for _ in range(10))
    if has_vmix:
        vf_ref, v0_ref, v1_ref, v2_ref = (next(it) for _ in range(4))
    r_o, k_o, v_o, a_o, ld_o, g_o, yb_o, xn_o, cnew_o, pc_scr, uc_scr = (next(it) for _ in range(11))

    t = pl.program_id(1)
    tm = x_ref.shape[0]
    rc = mu_ref.shape[1]
    cw = cw_ref.shape[1]
    rw = w0_ref.shape[1]

    @pl.when(t == 0)
    def _():
        sprev = jnp.broadcast_to(sprev_ref[...], (8, sprev_ref.shape[1]))
        pc_scr[...] = _bdot(sprev, win_ref[:, 0:rc])
        uc_scr[0:2, :] = cprev_ref[...]

    n_sub = max(1, tm // MIX_SUB_ROWS)
    sub = tm // n_sub
    col_chunks = [(0, 2 * rw), (2 * rw, rc), (rc, rc + cw), (rc + cw, rc + 3 * cw)]
    xns = [_rms(x_ref[i * sub:(i + 1) * sub, :], gmix_ref[...]) for i in range(n_sub)]
    carry = dict(p=pc_scr[0:1, :], u=[uc_scr[0:1, :], uc_scr[1:2, :]])

    def project(i, out):
        xb = xns[i].astype(BF16)
        for c0, c1 in col_chunks:
            out.append(jnp.dot(xb, win_ref[:, c0:c1], preferred_element_type=F32))
            yield

    def features(i, parts):
        rows = slice(i * sub, (i + 1) * sub)
        p_cur = jnp.concatenate(parts[0:2], axis=1)
        p_shift = _shift_rows(p_cur, [carry["p"]])
        vmix = None
        if has_vmix:
            vmix = (vf_ref[rows, :], v0_ref[...], v1_ref[...], v2_ref[...])
        mixed = []
        yield from _mix_rows_stages(p_cur, p_shift, mu_ref[...], w0_ref[...], a0_ref[...], wl_ref[...],
                                    lora_dims, rw, vmix, mixed)
        for ref, val in zip((r_o, k_o, v_o, a_o, ld_o, g_o), mixed):
            ref[rows, :] = val
        yield
        gate_b = parts[2]
        u = parts[3][:, 0:cw] * parts[3][:, cw:2 * cw]
        u1 = _shift_rows(u, [carry["u"][1]])
        u2 = _shift_rows(u, carry["u"])
        yb_o[rows, :] = gate_b * (cw_ref[0:1, :] * u2 + cw_ref[1:2, :] * u1 + cw_ref[2:3, :] * u)
        carry["p"] = p_cur[sub - 1:sub, :]
        carry["u"] = [u[sub - 2:sub - 1, :], u[sub - 1:sub, :]]

    parts = []
    _run_together(project(0, parts))
    for i in range(n_sub):
        nxt = []
        if i + 1 < n_sub:
            _run_together(features(i, parts), project(i + 1, nxt))
        else:
            _run_together(features(i, parts))
        parts = nxt

    pc_scr[0:1, :] = carry["p"]
    uc_scr[0:1, :] = carry["u"][0]
    uc_scr[1:2, :] = carry["u"][1]

    @pl.when(t == pl.num_programs(1) - 1)
    def _():
        xn_o[...] = xns[-1][sub - 1:sub, :]
        cnew_o[0:1, :] = carry["u"][0]
        cnew_o[1:2, :] = carry["u"][1]


def _split3(x):
    hi = x.astype(BF16)
    r1 = x - hi.astype(F32)
    mid = r1.astype(BF16)
    lo = (r1 - mid.astype(F32)).astype(BF16)
    return hi, mid, lo


def _dot3(a, b, dims=(((1,), (0,)), ((), ()))):
    a_hi = a.astype(BF16)
    a_lo = (a - a_hi.astype(F32)).astype(BF16)
    b_hi = b.astype(BF16)
    b_lo = (b - b_hi.astype(F32)).astype(BF16)
    d = functools.partial(lax.dot_general, dimension_numbers=dims, preferred_element_type=F32)
    return d(a_hi, b_hi) + d(a_lo, b_hi) + d(a_hi, b_lo)


def _dot1(a, b, dims=(((1,), (0,)), ((), ()))):
    return lax.dot_general(a.astype(BF16), b.astype(BF16), dims, preferred_element_type=F32)


_NT = (((1,), (1,)), ((), ()))
_TN = (((0,), (0,)), ((), ()))
_WKV_DOT = _dot1


def _wkv_seq_kernel(hs, r_ref, k_ref, v_ref, a_ref, ld_ref, kkw_ref, kaw_ref, rkw_ref, lnw_ref, lnb_ref,
                    s0_ref, y_o, s_o, st_scr):
    t = pl.program_id(1)
    tt, rw = r_ref.shape
    c = CHUNK
    pack = WKV_HEAD_PACK
    pw_lanes = pack * hs
    n_pairs = rw // pw_lanes
    assert c == hs, "chunk length equals the head size in the packed layout"
    shift = hs.bit_length() - 1
    assert 1 << shift == hs

    lane_head = lax.broadcasted_iota(jnp.int32, (1, pw_lanes), 1) >> shift
    row_p = lax.broadcasted_iota(jnp.int32, (pw_lanes, pw_lanes), 0)
    col_p = lax.broadcasted_iota(jnp.int32, (pw_lanes, pw_lanes), 1)
    same_head = (row_p >> shift) == (col_p >> shift)
    head_ones = jnp.where(same_head, 1.0, 0.0).astype(BF16)
    eye_p = (row_p == col_p).astype(F32)
    row_c = lax.broadcasted_iota(jnp.int32, (c, pw_lanes), 0)
    col_c = lax.broadcasted_iota(jnp.int32, (c, pw_lanes), 1) & (hs - 1)
    tril_strict = row_c > col_c
    tril_incl = row_c >= col_c
    eye_c = (row_c == col_c).astype(F32)
    row_t = lax.broadcasted_iota(jnp.int32, (c, 3 * c), 0)
    col_t = lax.broadcasted_iota(jnp.int32, (c, 3 * c), 1)
    col_t = jnp.where(col_t < c, col_t, jnp.where(col_t < 2 * c, col_t - c, col_t - 2 * c))
    tril3 = jnp.where(row_t >= col_t, 1.0, 0.0).astype(BF16)
    ones2 = jnp.concatenate([head_ones] * 2, axis=0)

    def blockdiag(x):
        return jnp.concatenate([jnp.where(lane_head == i, x, 0.0) for i in range(pack)], axis=0)

    def head_sums(x):
        hi = x.astype(BF16)
        lo = (x - hi.astype(F32)).astype(BF16)
        return jnp.dot(jnp.concatenate([hi, lo], axis=1), ones2, preferred_element_type=F32)

    @pl.when(t == 0)
    def _():
        z = jnp.zeros((hs, hs), F32)
        for p in range(n_pairs):
            blk = jnp.concatenate(
                [jnp.concatenate([s0_ref[pack * p + i] if j == i else z for j in range(pack)], axis=1)
                 for i in range(pack)], axis=0)
            st_scr[p] = blk.T

    n_chunks = tt // c
    cat = jnp.concatenate
    dot = _WKV_DOT

    def each(f, *lists):
        return [f(*xs) for xs in zip(*lists)]

    def cumsum_rows(x):
        hi, mid, lo = _split3(x)
        return jnp.dot(tril3, cat([hi, mid, lo], axis=0), preferred_element_type=F32)

    def prepare(chunks, out):
        items = [(ci, p) for ci in chunks for p in range(n_pairs)]
        tiles = lambda ref: [ref[ci * c:(ci + 1) * c, p * pw_lanes:(p + 1) * pw_lanes] for ci, p in items]
        params = lambda ref: [ref[:, p * pw_lanes:(p + 1) * pw_lanes] for _, p in items]
        r, k, v, a_sig, ld = (tiles(ref) for ref in (r_ref, k_ref, v_ref, a_ref, ld_ref))
        kkw, kaw, rkw = (params(ref) for ref in (kkw_ref, kaw_ref, rkw_ref))
        kk = each(lambda k_, w_: k_ * w_, k, kkw)
        kmod = each(lambda k_, a_, w_: k_ * (1.0 + (a_ - 1.0) * w_), k, a_sig, kaw)
        sums = each(lambda kk_, r_, km_, w_: head_sums(cat([kk_ * kk_, r_ * km_ * w_], axis=0)), kk, r, kmod, rkw)
        yield
        kk = each(lambda kk_, s_: kk_ / jnp.maximum(jnp.sqrt(s_[0:c]), KK_NORM_FLOOR), kk, sums)
        b = each(lambda kk_, a_: kk_ * a_, kk, a_sig)
        cum = each(cumsum_rows, ld)
        yield
        at = each(lambda kk_, cu, l_: -kk_ * jnp.exp(cu - l_), kk, cum, ld)
        rt = each(lambda r_, cu: r_ * jnp.exp(cu), r, cum)
        inv = each(lambda cu: jnp.exp(-cu), cum)
        bt = each(lambda b_, i_: b_ * i_, b, inv)
        kt = each(lambda km_, i_: km_ * i_, kmod, inv)
        yield
        to_end = each(lambda cu: jnp.exp(cu[c - 1:c, :] - cu), cum)
        bk_end = each(lambda b_, km_, te: cat([b_ * te, km_ * te], axis=0), b, kmod, to_end)
        p_end_col = each(lambda cu: jnp.sum(eye_p * jnp.exp(cu[c - 1:c, :]), axis=-1, keepdims=True), cum)
        gm = each(lambda at_, rt_, bt_, kt_: dot(cat([at_, rt_], axis=0),
                                                 cat([blockdiag(bt_), blockdiag(kt_)], axis=0), _NT),
                  at, rt, bt, kt)
        yield
        pc = pack * c
        l_ab = [jnp.where(tril_strict, g[0:c, 0:pc], 0.0) for g in gm]
        l_ak = [jnp.where(tril_strict, g[0:c, pc:2 * pc], 0.0) for g in gm]
        m_rb = [jnp.where(tril_incl, g[c:2 * c, 0:pc], 0.0) for g in gm]
        m_rk = [jnp.where(tril_incl, g[c:2 * c, pc:2 * pc], 0.0) for g in gm]
        t_inv = [eye_c + l_ for l_ in l_ab]
        pw = each(lambda q: dot(q, blockdiag(q)), l_ab)
        yield
        span = 4
        while span <= c:
            last = span == c
            both = each(lambda q, t_: dot(q, blockdiag(t_) if last else
                                          cat([blockdiag(t_), blockdiag(q)], axis=1)), pw, t_inv)
            t_inv = each(lambda t_, m_: t_ + m_[:, 0:pc], t_inv, both)
            if not last:
                pw = [m_[:, pc:2 * pc] for m_ in both]
            yield
            span *= 2
        out.update(items=items, r=r, v=v, kmod=kmod, bonus=[s_[c:2 * c] for s_ in sums], at=at, rt=rt,
                   l_ak=l_ak, m_rb=m_rb, m_rk=m_rk, t_inv=t_inv, bk_end=bk_end, p_end_col=p_end_col)

    def state_pass(d):
        items = d["items"]
        y = []
        for i0 in range(0, len(items), n_pairs):
            idx = range(i0, i0 + n_pairs)
            st = [st_scr[p] for p in range(n_pairs)]
            vb = [blockdiag(d["v"][i]) for i in idx]
            x = [dot(cat([d["at"][i], d["l_ak"][i]], axis=1), cat([st[p], vb[p]], axis=0))
                 for p, i in enumerate(idx)]
            yield
            u = [dot(d["t_inv"][i], blockdiag(x[p])) for p, i in enumerate(idx)]
            yield
            y += [dot(cat([d["rt"][i], d["m_rk"][i], d["m_rb"][i]], axis=1),
                      cat([st[p], vb[p], blockdiag(u[p])], axis=0)) for p, i in enumerate(idx)]
            upd = [dot(d["bk_end"][i], cat([u[p], d["v"][i]], axis=0), _TN) for p, i in enumerate(idx)]
            for p, i in enumerate(idx):
                st_scr[p] = d["p_end_col"][i] * st[p] + jnp.where(same_head, upd[p], 0.0)
            yield
        s2 = each(lambda y_: head_sums(cat([y_, y_ * y_], axis=0)), y)
        yield
        for (ci, p), y_, s_, bo_, v_ in zip(items, y, s2, d["bonus"], d["v"]):
            ln = slice(p * pw_lanes, (p + 1) * pw_lanes)
            mean = s_[0:c] * (1.0 / hs)
            var = s_[c:2 * c] * (1.0 / hs) - mean * mean
            yn = (y_ - mean) * lax.rsqrt(var + hs * GROUPNORM_EPS_PER_CHANNEL) * lnw_ref[:, ln] + lnb_ref[:, ln]
            y_o[ci * c:(ci + 1) * c, ln] = yn + bo_ * v_

    def run_together(*progs):
        progs = list(progs)
        while progs:
            for prog in list(progs):
                if next(prog, StopIteration) is StopIteration:
                    progs.remove(prog)

    halves = [range(h, min(h + WKV_BATCH_CHUNKS, n_chunks)) for h in range(0, n_chunks, WKV_BATCH_CHUNKS)]
    prev = None
    for chunks in halves:
        cur = {}
        if prev is None:
            run_together(prepare(chunks, cur))
        else:
            run_together(prepare(chunks, cur), state_pass(prev))
        prev = cur
    run_together(state_pass(prev))

    @pl.when(t == pl.num_programs(1) - 1)
    def _():
        for p in range(n_pairs):
            s_grp = st_scr[p].T
            for i in range(pack):
                s_o[pack * p + i] = s_grp[i * hs:(i + 1) * hs, i * hs:(i + 1) * hs]


def _mix_out_attn_seq_kernel(n_xheads, x_ref, y_ref, g_ref, yb_ref, wout_ref, gx_ref, wq_ref,
                             mk_ref, mv_ref, wo_ref, x2_o):
    tm, d = x_ref.shape
    xd = d // n_xheads
    scale = xd ** -0.5
    n_sub = max(1, tm // MIX_SUB_ROWS)
    sub = tm // n_sub
    mk = mk_ref[...].astype(BF16)
    mv = mv_ref[...].astype(BF16)

    def sub_tile(i):
        rows = slice(i * sub, (i + 1) * sub)
        mixed = jnp.concatenate([y_ref[rows, :] * g_ref[rows, :], yb_ref[rows, :]], axis=-1)
        x1 = x_ref[rows, :] + _bdot(mixed, wout_ref[...])
        yield
        q = _bdot(_rms(x1, gx_ref[...]), wq_ref[...]).astype(BF16)
        yield
        heads = [slice(h * xd, (h + 1) * xd) for h in range(n_xheads)]
        s = [lax.dot_general(q[:, sl], mk[:, sl], _NT, preferred_element_type=F32) * scale for sl in heads]
        yield
        p = [_softmax_rows(s_).astype(BF16) for s_ in s]
        yield
        o = jnp.concatenate([jnp.dot(p_, mv[:, sl], preferred_element_type=F32) for p_, sl in zip(p, heads)],
                            axis=-1)
        yield
        x2_o[rows, :] = x1 + _bdot(o, wo_ref[...])

    _run_together(*[sub_tile(i) for i in range(n_sub)])


def _ffn_seq_kernel(final, *refs):
    it = iter(refs)
    x_ref, gf_ref, wup_ref, fprev_ref, fcw_ref, wdown_ref = (next(it) for _ in range(6))
    gfin_ref = next(it) if final else None
    q_ref, mk_ref, mv_ref = (next(it) for _ in range(3))
    y_o, fnew_o, o_o, carry = (next(it) for _ in range(4))

    t = pl.program_id(1)
    tm = x_ref.shape[0]
    f = wdown_ref.shape[0]
    fc = FFN_COL_TILE

    @pl.when(t == 0)
    def _():
        carry[0:2, :] = fprev_ref[...]

    n_sub = max(1, tm // FFN_SUB_ROWS)
    sub = tm // n_sub
    n_tiles = f // fc
    xs, xfs = {}, {}

    def up_item(s, j):
        if s not in xs:
            xs[s] = x_ref[s * sub:(s + 1) * sub, :]
            xfs[s] = _rms(xs[s], gf_ref[...]).astype(BF16)
        c0 = j * fc
        return (jnp.dot(xfs[s], wup_ref[:, c0:c0 + fc], preferred_element_type=F32),
                jnp.dot(xfs[s], wup_ref[:, f + c0:f + c0 + fc], preferred_element_type=F32))

    def conv_cols(up, c0):
        p0 = carry[0:1, c0:c0 + fc]
        p1 = carry[1:2, c0:c0 + fc]
        up1 = _shift_rows(up, [p1])
        up2 = _shift_rows(up, [p0, p1])
        carry[0:2, c0:c0 + fc] = up[sub - 2:sub, :]
        return (fcw_ref[0:1, c0:c0 + fc] * up2 + fcw_ref[1:2, c0:c0 + fc] * up1
                + fcw_ref[2:3, c0:c0 + fc] * up)

    items = [(s, j) for s in range(n_sub) for j in range(n_tiles)]
    ahead = [up_item(*it_) for it_ in items[:FFN_LOOKAHEAD]]
    attention = _sample_attention_stages(q_ref, mk_ref, mv_ref, o_o)
    acc = None
    for idx, (s, j) in enumerate(items):
        c0 = j * fc
        cur = ahead.pop(0)
        if idx + FFN_LOOKAHEAD < len(items):
            ahead.append(up_item(*items[idx + FFN_LOOKAHEAD]))
        if j == 0:
            acc = xs[s]
        u = conv_cols(cur[0], c0)
        gt = conv_cols(cur[1], f + c0)
        hidden = gt * _sigmoid(gt) * u
        acc = acc + _bdot(hidden, wdown_ref[c0:c0 + fc, :])
        next(attention, None)
        if j == n_tiles - 1:
            if final:
                acc = _rms(acc, gfin_ref[...])
            y_o[s * sub:(s + 1) * sub, :] = acc
    for _ in attention:
        pass

    @pl.when(t == pl.num_programs(1) - 1)
    def _():
        fnew_o[...] = carry[0:2, :]


def _mem_kv_kernel(m_ref, g_ref, wk_ref, wv_ref, k_o, v_o, kh_o, vh_o):
    mn = _rms(m_ref[...], g_ref[...])
    nseq, m, nh, xd = kh_o.shape
    for w_ref, flat_o, heads_o in ((wk_ref, k_o, kh_o), (wv_ref, v_o, vh_o)):
        val = _bdot(mn, w_ref[...])
        flat_o[...] = val
        for s in range(nseq):
            for h in range(nh):
                heads_o[s, :, h, :] = val[s * m:(s + 1) * m, h * xd:(h + 1) * xd]


def _mix_in_step_kernel(has_vmix, lora_dims, *refs):
    it = iter(refs)
    x_ref, gmix_ref, win_ref, sprev_ref, cprev_ref, mu_ref, w0_ref, a0_ref, wl_ref, cw_ref = (
        next(it) for _ in range(10))
    if has_vmix:
        vf_ref, v0_ref, v1_ref, v2_ref = (next(it) for _ in range(4))
    r_o, k_o, v_o, a_o, ld_o, vrow_o, g_o, yb_o, xn_o, cnew_o = (next(it) for _ in range(10))

    nb = x_ref.shape[0]
    rc = mu_ref.shape[1]
    cw = cw_ref.shape[1]
    rw = w0_ref.shape[1]

    xn = _rms(x_ref[...], gmix_ref[...])
    xn_o[...] = xn
    stacked = jnp.concatenate([xn, sprev_ref[...]], axis=0)
    proj = _bdot(stacked, win_ref[...])
    p_cur = proj[0:nb, 0:rc]
    p_shift = proj[nb:2 * nb, 0:rc]
    vmix = None
    if has_vmix:
        vmix = (vf_ref[...], v0_ref[...], v1_ref[...], v2_ref[...])
    r, k, v, a_sig, log_decay, gate = _mix_rows(
        p_cur, p_shift, mu_ref[...], w0_ref[...], a0_ref[...], wl_ref[...], lora_dims, rw, vmix)
    r_o[...] = r.T
    k_o[...] = k.T
    v_o[...] = v.T
    a_o[...] = a_sig.T
    ld_o[...] = log_decay.T
    vrow_o[...] = v
    g_o[...] = gate

    gate_b = proj[0:nb, rc:rc + cw]
    gate_c = proj[0:nb, rc + cw:rc + 2 * cw]
    h_in = proj[0:nb, rc + 2 * cw:rc + 3 * cw]
    u = gate_c * h_in
    u2 = cprev_ref[:, 0, :]
    u1 = cprev_ref[:, 1, :]
    yb_o[...] = gate_b * (cw_ref[0:1, :] * u2 + cw_ref[1:2, :] * u1 + cw_ref[2:3, :] * u)
    cnew_o[:, 0, :] = u1
    cnew_o[:, 1, :] = u


def _wkv_step_kernel(aliased, *refs):
    r_ref, k_ref, v_ref, a_ref, ld_ref, kkw_ref, kaw_ref, rkw_ref, lnw_ref, lnb_ref, s_ref = refs[:11]
    y_o, s_o, y_scr = refs[12:] if aliased else refs[11:]
    n = s_ref.shape[0]

    @pl.when(pl.program_id(0) == 0)
    def _():
        r, k, v, a_sig, ld = r_ref[...], k_ref[...], v_ref[...], a_ref[...], ld_ref[...]
        kk = k * kkw_ref[...]
        kk = kk / jnp.maximum(jnp.sqrt(jnp.sum(kk * kk, axis=0, keepdims=True)), KK_NORM_FLOOR)
        kmod = k * (1.0 + (a_sig - 1.0) * kaw_ref[...])
        a = -kk
        b = kk * a_sig
        decay = jnp.exp(ld)
        for vi in range(n):
            s = s_ref[vi]
            sa = jnp.sum(s * a, axis=0, keepdims=True)
            s_new = s * decay + sa * b + v[vi:vi + 1, :] * kmod
            s_o[vi] = s_new
            y_scr[vi:vi + 1, :] = jnp.sum(s_new * r, axis=0, keepdims=True)
        y = y_scr[...]
        mean = jnp.mean(y, axis=0, keepdims=True)
        yc = y - mean
        var = jnp.mean(yc * yc, axis=0, keepdims=True)
        y = yc * lax.rsqrt(var + n * GROUPNORM_EPS_PER_CHANNEL) * lnw_ref[...] + lnb_ref[...]
        y_o[...] = y + jnp.sum(r * kmod * rkw_ref[...], axis=0, keepdims=True) * v

    @pl.when(pl.program_id(0) > 0)
    def _():
        s_o[...] = jnp.zeros(s_o.shape, F32)


def _mix_out_q_step_kernel(x_ref, ya_ref, g_ref, yb_ref, wout_ref, gx_ref, wq_ref, x1_o, q_o):
    mixed = jnp.concatenate([ya_ref[...].T * g_ref[...], yb_ref[...]], axis=-1)
    x1 = x_ref[...] + _bdot(mixed, wout_ref[...])
    x1_o[...] = x1
    q_o[...] = _bdot(_rms(x1, gx_ref[...]), wq_ref[...])


def _sample_attention_stages(q_ref, mk_ref, mv_ref, o_o):
    bt = q_ref.shape[0]
    m, nh, xd = mk_ref.shape[1:]
    scale = xd ** -0.5
    fold = SUBLANES // nh if SUBLANES % nh == 0 and m % max(SUBLANES // nh, 1) == 0 else 1

    def folded(x):
        return [x[j * nh:(j + 1) * nh] for j in range(fold)]

    for i in range(bt):
        q = jnp.concatenate([q_ref[i]] * fold, axis=0)
        k = mk_ref[i].reshape(m // fold, fold * nh, xd)
        s = jnp.sum(k * q, axis=-1, keepdims=True) * scale
        yield
        mx = functools.reduce(jnp.maximum, folded(jnp.max(s, axis=0)))
        e = jnp.exp(s - jnp.concatenate([mx] * fold, axis=0))
        den = sum(folded(jnp.sum(e, axis=0)))
        yield
        v = mv_ref[i].reshape(m // fold, fold * nh, xd)
        acc = sum(folded(jnp.sum(e * v, axis=0)))
        o_o[i] = acc / den
        yield


def _ffn_step_kernel(final, aliased, *refs):
    it = iter(refs)
    x1_ref, o_ref, wo_ref, gf_ref, wup_ref, fprev_ref, fcw_ref, wdown_ref = (next(it) for _ in range(8))
    gfin_ref = next(it) if final else None
    if aliased:
        next(it)
    y_o, fnew_o = (next(it) for _ in range(2))
    f = wdown_ref.shape[0]
    f2 = 2 * f

    @pl.when(pl.program_id(0) == 0)
    def _():
        x2 = x1_ref[...] + _bdot(o_ref[...], wo_ref[...])
        up = _bdot(_rms(x2, gf_ref[...]), wup_ref[...])
        up2 = fprev_ref[:, 0, :]
        up1 = fprev_ref[:, 1, :]
        upc = fcw_ref[0:1, :] * up2 + fcw_ref[1:2, :] * up1 + fcw_ref[2:3, :] * up
        fnew_o[:, 0, :] = up1
        fnew_o[:, 1, :] = up
        u = upc[:, 0:f]
        gt = upc[:, f:f2]
        x3 = x2 + _bdot(gt * _sigmoid(gt) * u, wdown_ref[...])
        if final:
            x3 = _rms(x3, gfin_ref[...])
        y_o[...] = x3

    @pl.when(pl.program_id(0) > 0)
    def _():
        fnew_o[...] = jnp.zeros(fnew_o.shape, F32)


def _params(*sem):
    return pltpu.CompilerParams(dimension_semantics=sem, vmem_limit_bytes=V7X_VMEM_LIMIT_BYTES)


class _Layered(NamedTuple):
    stacked: jax.Array
    layer: int

    @property
    def shape(self):
        return self.stacked.shape[1:]


def _operand(x):
    return x.stacked if isinstance(x, _Layered) else x


def _whole(x):
    if isinstance(x, _Layered):
        nd, l = x.stacked.ndim, x.layer
        return pl.BlockSpec((None,) + x.shape, lambda *_: (l,) + (0,) * (nd - 1), pipeline_mode=pl.Buffered(1))
    nd = x.ndim
    return pl.BlockSpec(x.shape, lambda *_: (0,) * nd, pipeline_mode=pl.Buffered(1))


def _sds(shape):
    return jax.ShapeDtypeStruct(shape, F32)


def _stacked_weights(p):
    depth = p["w_in"].shape[0]
    rows = lambda a: a.reshape(a.shape[0], 1, -1)
    rw = p["w0"].shape[1]
    d_decay, d_aaa, d_gate = p["w2"].shape[1], p["a2"].shape[1], p["g2"].shape[1]
    pad_cols = lambda a, before, after: jnp.pad(a, ((0, 0), (0, 0), (before, after)))
    w_lora = jnp.concatenate([pad_cols(p["w2"], 0, 2 * rw), pad_cols(p["a2"], rw, rw), pad_cols(p["g2"], 2 * rw, 0)],
                             axis=1)
    lanes = 128
    mv = p["v1"].shape[2]
    mvp = -(-mv // lanes) * lanes
    w = dict(
        norm_mix=rows(p["norm_mix"]), w_in=p["w_in"].astype(BF16), mu=rows(p["mu_shift"]), w0=rows(p["w0"]),
        a0=rows(p["a0"]), w_lora=w_lora.astype(BF16), conv_w=p["conv_w"], k_k=p["k_k"], k_a=p["k_a"],
        r_k=p["r_k"], ln_w=p["ln_x_w"], ln_b=p["ln_x_b"], w_out=p["w_out"].astype(BF16), norm_x=rows(p["norm_x"]),
        wq=p["wq"].astype(BF16), wo=p["wo"].astype(BF16), norm_ffn=rows(p["norm_ffn"]),
        w_up=p["w_up"].astype(BF16), ffn_conv_w=p["ffn_conv_w"], w_down=p["w_down"].astype(BF16),
        norm_mem=rows(p["norm_mem"]), wk=p["wk"].astype(BF16), wv=p["wv"].astype(BF16),
    )
    layers = []
    for l in range(depth):
        wl = {k_: _Layered(v_, l) for k_, v_ in w.items()}
        wl["norm_final"] = p["norm_final"].reshape(1, -1)
        wl["lora_dims"] = (d_decay, d_aaa, d_gate)
        if l > 0:
            wl["v0"] = _Layered(rows(p["v0"]), l - 1)
            wl["v1"] = _Layered(jnp.pad(p["v1"], ((0, 0), (0, 0), (0, mvp - mv))).astype(BF16), l - 1)
            wl["v2"] = _Layered(jnp.pad(p["v2"], ((0, 0), (0, mvp - mv), (0, 0))).astype(BF16), l - 1)
        layers.append(wl)
    return layers


def _head_rows(x, n_heads):
    st = x.stacked
    return _Layered(st.reshape(st.shape[0], n_heads, st.shape[1] // n_heads, 1), x.layer)


def _mem_kv(mem, w, n_xheads):
    bsz, m, d = mem.shape
    depth = w["wk"].stacked.shape[0]
    nseq = max(1, min(bsz, 512 // m))
    xd = d // n_xheads
    stacked = lambda a: pl.BlockSpec((None,) + a.shape, lambda l, i: (l,) + (0,) * len(a.shape))
    flat = pl.BlockSpec((None, nseq * m, d), lambda l, i: (l, i, 0))
    heads = pl.BlockSpec((None, nseq, m, n_xheads, xd), lambda l, i: (l, i, 0, 0, 0))
    args = [mem.reshape(bsz * m, d), w["norm_mem"], w["wk"], w["wv"]]
    k, v, kh, vh = pl.pallas_call(
        _mem_kv_kernel,
        grid=(depth, bsz // nseq),
        in_specs=[pl.BlockSpec((nseq * m, d), lambda l, i: (i, 0))] + [stacked(a) for a in args[1:]],
        out_specs=[flat, flat, heads, heads],
        out_shape=[_sds((depth, bsz * m, d))] * 2 + [_sds((depth, bsz, m, n_xheads, xd))] * 2,
        compiler_params=_params("arbitrary", "arbitrary"),
        name="mem_kv",
    )(*map(_operand, args))
    return k.reshape(depth, bsz, m, d), v.reshape(depth, bsz, m, d), kh, vh


def _prompt_mixer(l, x, mem_k, mem_v, shift0, wkv0, conv0, v_first, w, n_heads, n_xheads):
    shift0, wkv0, conv0 = (a[l] for a in (shift0, wkv0, conv0))
    bsz, t, d = x.shape
    rw = w["w0"].shape[1]
    hs = rw // n_heads
    cw = w["conv_w"].shape[1]
    rc = w["mu"].shape[1]
    tm = min(ROW_TILE, t)
    nt = t // tm
    has_vmix = v_first is not None

    tile = lambda n: pl.BlockSpec((None, tm, n), lambda b, i: (b, i, 0))
    per_b = lambda s: pl.BlockSpec((None,) + s, lambda b, i: (b,) + (0,) * len(s))

    ins = [x, w["norm_mix"], w["w_in"], shift0[:, None, :], conv0, w["mu"], w["w0"], w["a0"], w["w_lora"],
           w["conv_w"]]
    specs = [tile(d), _whole(w["norm_mix"]), _whole(w["w_in"]), per_b((1, d)), per_b((2, cw)),
             _whole(w["mu"]), _whole(w["w0"]), _whole(w["a0"]), _whole(w["w_lora"]), _whole(w["conv_w"])]
    if has_vmix:
        ins += [v_first, w["v0"], w["v1"], w["v2"]]
        specs += [tile(rw), _whole(w["v0"]), _whole(w["v1"]), _whole(w["v2"])]
    out_shape = [_sds((bsz, t, rw))] * 6 + [_sds((bsz, t, cw)), _sds((bsz, 1, d)), _sds((bsz, 2, cw))]
    out_specs = [tile(rw)] * 6 + [tile(cw), per_b((1, d)), per_b((2, cw))]
    r, k, v, a_sig, ld, gate, y_b, xn_last, conv_new = pl.pallas_call(
        functools.partial(_mix_in_seq_kernel, has_vmix, w["lora_dims"]),
        grid=(bsz, nt), in_specs=specs, out_specs=out_specs, out_shape=out_shape,
        scratch_shapes=[pltpu.VMEM((8, rc), F32), pltpu.VMEM((8, cw), F32)],
        compiler_params=_params("parallel", "arbitrary"),
        name="mix_in_seq",
    )(*map(_operand, ins))
    if not has_vmix:
        v_first = v

    head_params = [_Layered(a.stacked[:, None, :], a.layer) for a in
                   (w["k_k"], w["k_a"], w["r_k"], w["ln_w"], w["ln_b"])]
    sblk = per_b((n_heads, hs, hs))
    wt = min(WKV_ROW_TILE, t)
    wtile = pl.BlockSpec((None, wt, rw), lambda b, i: (b, i, 0))
    y_a, wkv_new = pl.pallas_call(
        functools.partial(_wkv_seq_kernel, hs),
        grid=(bsz, t // wt),
        in_specs=[wtile] * 5 + [_whole(a) for a in head_params] + [sblk],
        out_specs=[wtile, sblk],
        out_shape=[_sds((bsz, t, rw)), _sds((bsz, n_heads, hs, hs))],
        scratch_shapes=[pltpu.VMEM((rw // (WKV_HEAD_PACK * hs), WKV_HEAD_PACK * hs, WKV_HEAD_PACK * hs), F32)],
        compiler_params=_params("parallel", "arbitrary"),
        name="wkv_seq",
    )(r, k, v, a_sig, ld, *map(_operand, head_params), wkv0)

    mblk = pl.BlockSpec((None, None) + mem_k.shape[2:], lambda b, i: (l, b, 0, 0))
    ot = min(MIX_OUT_ROW_TILE, t)
    tile = lambda n: pl.BlockSpec((None, ot, n), lambda b, i: (b, i, 0))
    x2 = pl.pallas_call(
        functools.partial(_mix_out_attn_seq_kernel, n_xheads),
        grid=(bsz, t // ot),
        in_specs=[tile(d), tile(rw), tile(rw), tile(cw), _whole(w["w_out"]), _whole(w["norm_x"]), _whole(w["wq"]),
                  mblk, mblk, _whole(w["wo"])],
        out_specs=tile(d), out_shape=_sds((bsz, t, d)),
        compiler_params=_params("parallel", "arbitrary"),
        name="mix_out_attn_seq",
    )(*map(_operand, (x, y_a, gate, y_b, w["w_out"], w["norm_x"], w["wq"], mem_k, mem_v, w["wo"])))

    return x2, xn_last[:, 0, :], wkv_new, conv_new, v_first


def _prompt_ffn(l, x2, ffn0, q, mem_k, mem_v, w, final):
    bsz, t, d = x2.shape
    ffn0 = ffn0[l]
    f2 = w["w_up"].shape[1]
    ft = min(FFN_ROW_TILE, t)
    nt = t // ft
    nb, n_xheads, xd = q.shape
    m = mem_k.shape[2]
    assert nb % (bsz * nt) == 0, "sample sequences must split evenly over the FFN grid steps"
    ab = nb // (bsz * nt)
    per_b = lambda s: pl.BlockSpec((None,) + s, lambda b, i: (b,) + (0,) * len(s))
    ftile = pl.BlockSpec((None, ft, d), lambda b, i: (b, i, 0))
    qblk = pl.BlockSpec((ab, n_xheads, xd), lambda b, i: (b * nt + i, 0, 0))
    mblk = pl.BlockSpec((None, ab, m, n_xheads, xd), lambda b, i: (l, b * nt + i, 0, 0, 0))
    ins = [x2, w["norm_ffn"], w["w_up"], ffn0, w["ffn_conv_w"], w["w_down"]]
    specs = [ftile, _whole(w["norm_ffn"]), _whole(w["w_up"]), per_b((2, f2)), _whole(w["ffn_conv_w"]),
             _whole(w["w_down"])]
    if final:
        ins.append(w["norm_final"])
        specs.append(_whole(w["norm_final"]))
    ins += [q, mem_k, mem_v]
    specs += [qblk, mblk, mblk]
    x3, ffn_new, o = pl.pallas_call(
        functools.partial(_ffn_seq_kernel, final),
        grid=(bsz, nt), in_specs=specs,
        out_specs=[ftile, per_b((2, f2)), qblk],
        out_shape=[_sds((bsz, t, d)), _sds((bsz, 2, f2)), _sds((nb, n_xheads, xd))],
        scratch_shapes=[pltpu.VMEM((8, f2), F32)],
        compiler_params=_params("arbitrary", "arbitrary"),
        name="ffn_seq",
    )(*map(_operand, ins))
    return x3, ffn_new, o


def _sample_mixer(l, x, shift0, wkv_t, conv0, carried, w, n_heads, n_xheads):
    v_first, wkv_all = carried
    nb, d = x.shape
    rw = w["w0"].shape[1]
    hs = rw // n_heads
    cw = w["conv_w"].shape[1]
    has_vmix = v_first is not None
    one = lambda a: pl.BlockSpec(a.shape, lambda i: (0,) * a.ndim, pipeline_mode=pl.Buffered(1))
    layer_of = lambda a: pl.BlockSpec((None,) + a.shape[1:], lambda i: (l,) + (0,) * (a.ndim - 1),
                                      pipeline_mode=pl.Buffered(1))

    ins = [x, w["norm_mix"], w["w_in"], shift0, conv0, w["mu"], w["w0"], w["a0"], w["w_lora"], w["conv_w"]]
    specs = [one(x), _whole(w["norm_mix"]), _whole(w["w_in"]), layer_of(shift0), layer_of(conv0)] + [
        _whole(a) for a in ins[5:]]
    if has_vmix:
        ins += [v_first, w["v0"], w["v1"], w["v2"]]
        specs += [one(v_first), _whole(w["v0"]), _whole(w["v1"]), _whole(w["v2"])]
    out_shape = [_sds((rw, nb))] * 5 + [_sds((nb, rw))] * 2 + [_sds((nb, cw)), _sds((nb, d)), _sds((nb, 2, cw))]
    r, k, v, a_sig, ld, v_rows, gate, y_b, xn, conv_new = pl.pallas_call(
        functools.partial(_mix_in_step_kernel, has_vmix, w["lora_dims"]),
        grid=(1,), in_specs=specs, out_specs=[one(s_) for s_ in out_shape], out_shape=out_shape,
        compiler_params=_params("arbitrary"),
        name="mix_in_step",
    )(*map(_operand, ins))
    if not has_vmix:
        v_first = v_rows

    aliased = wkv_all is not None
    n_clear = 0 if aliased else wkv_t.shape[0] - 1 - l
    last = n_heads - 1
    hsel = lambda c, h: jnp.where(c == 0, h, last)
    heads = lambda a: a.reshape(n_heads, hs, nb)
    hblk = pl.BlockSpec((None, hs, nb), lambda c, h: (hsel(c, h), 0, 0))
    head_params = [_head_rows(w[n_], n_heads) for n_ in ("k_k", "k_a", "r_k", "ln_w", "ln_b")]
    pblk = pl.BlockSpec((None, None, hs, 1), lambda c, h: (l, hsel(c, h), 0, 0))
    sblk = pl.BlockSpec((None, None, hs, hs, nb), lambda c, h: (l, hsel(c, h), 0, 0, 0))
    soblk = pl.BlockSpec((None, None, hs, hs, nb), lambda c, h: (l + c, h, 0, 0, 0))
    ins = [heads(r), heads(k), heads(v), heads(a_sig), heads(ld)] + [a.stacked for a in head_params] + [wkv_t]
    specs = [hblk] * 5 + [pblk] * 5 + [sblk]
    if aliased:
        ins.append(wkv_all)
        specs.append(pl.BlockSpec(memory_space=pl.ANY))
    y_a, wkv_all = pl.pallas_call(
        functools.partial(_wkv_step_kernel, aliased),
        grid=(1 + n_clear, n_heads),
        in_specs=specs, out_specs=[hblk, soblk],
        out_shape=[_sds((n_heads, hs, nb)), _sds(wkv_t.shape)],
        scratch_shapes=[pltpu.VMEM((hs, nb), F32)],
        input_output_aliases={len(ins) - 1: 1} if aliased else {},
        compiler_params=_params("arbitrary", "arbitrary"),
        name="wkv_step",
    )(*ins)

    ins = [x, y_a.reshape(rw, nb), gate, y_b, w["w_out"], w["norm_x"], w["wq"]]
    x1, q = pl.pallas_call(
        _mix_out_q_step_kernel,
        grid=(1,), in_specs=[one(a) for a in ins[:4]] + [_whole(a) for a in ins[4:]],
        out_specs=[one(x)] * 2, out_shape=[_sds((nb, d))] * 2,
        compiler_params=_params("arbitrary"),
        name="mix_out_q_step",
    )(*map(_operand, ins))

    xd = d // n_xheads
    return x1, q.reshape(nb, n_xheads, xd), xn, conv_new, (v_first, wkv_all)


def _sample_ffn(l, x1, o, ffn0, ffn_all, w, final):
    nb, d = x1.shape
    o = o.reshape(nb, d)
    one = lambda a: pl.BlockSpec(a.shape, lambda i: (0,) * a.ndim, pipeline_mode=pl.Buffered(1))
    ins = [x1, o, w["wo"], w["norm_ffn"], w["w_up"], ffn0, w["ffn_conv_w"], w["w_down"]]
    fblk = pl.BlockSpec((None,) + ffn0.shape[1:], lambda c: (l, 0, 0, 0))
    foblk = pl.BlockSpec((None,) + ffn0.shape[1:], lambda c: (l + c, 0, 0, 0))
    specs = [one(x1), one(o), _whole(w["wo"]), _whole(w["norm_ffn"]), _whole(w["w_up"]), fblk,
             _whole(w["ffn_conv_w"]), _whole(w["w_down"])]
    if final:
        ins.append(w["norm_final"])
        specs.append(_whole(w["norm_final"]))
    ffn_aliased = ffn_all is not None
    if ffn_aliased:
        ins.append(ffn_all)
        specs.append(pl.BlockSpec(memory_space=pl.ANY))
    x3, ffn_all = pl.pallas_call(
        functools.partial(_ffn_step_kernel, final, ffn_aliased),
        grid=(1 if ffn_aliased else ffn0.shape[0] - l,), in_specs=specs, out_specs=[one(x1), foblk],
        out_shape=[_sds((nb, d)), _sds(ffn0.shape)],
        input_output_aliases={len(ins) - 1: 1} if ffn_aliased else {},
        compiler_params=_params("arbitrary"),
        name="ffn_step",
    )(*map(_operand, ins))
    return x3, ffn_all


def kernel(x_prompt, x_sample, mem_prompt, state_shift, state_wkv, state_conv, state_ffn, cache_mem_k, cache_mem_v, norm_mix, w_in, mu_shift, w0, w2, a0, a2, g2, v0, v1, v2, k_k, k_a, r_k, ln_x_w, ln_x_b, conv_w, w_out, norm_x, norm_mem, wq, wk, wv, wo, norm_ffn, w_up, ffn_conv_w, w_down, norm_final):
    p = dict(norm_mix=norm_mix, w_in=w_in, mu_shift=mu_shift, w0=w0, w2=w2, a0=a0, a2=a2, g2=g2, v0=v0, v1=v1,
             v2=v2, k_k=k_k, k_a=k_a, r_k=r_k, ln_x_w=ln_x_w, ln_x_b=ln_x_b, conv_w=conv_w, w_out=w_out,
             norm_x=norm_x, norm_mem=norm_mem, wq=wq, wk=wk, wv=wv, wo=wo, norm_ffn=norm_ffn, w_up=w_up,
             ffn_conv_w=ffn_conv_w, w_down=w_down, norm_final=norm_final)
    depth = w_in.shape[0]
    n_heads = state_wkv.shape[2]
    hs = state_wkv.shape[3]
    n_xheads, xd = cache_mem_k.shape[3], cache_mem_k.shape[4]
    assert x_sample.shape[1] == 1, "the sample group advances one token per sequence"
    weights = _stacked_weights(p)

    bp, tp, d = x_prompt.shape
    m = mem_prompt.shape[1]
    mks, mvs, mem_k_p, mem_v_p = _mem_kv(mem_prompt, weights[0], n_xheads)
    cw = conv_w.shape[2]
    f2 = w_up.shape[2]
    zeros = lambda *s_: jnp.zeros((depth, bp) + s_, F32)
    nb = x_sample.shape[0]
    wkv_t = jnp.transpose(state_wkv, (0, 2, 3, 4, 1))
    xp, v_first = x_prompt, None
    xs, carried, ffn_s = x_sample.reshape(nb, d), (None, None), None
    shs, wks, cvs, ffs, xns, cns = [], [], [], [], [], []
    for l in range(depth):
        final = l == depth - 1
        x2, sh, s_, cs, v_first = _prompt_mixer(l, xp, mks, mvs, zeros(d), zeros(n_heads, hs, hs),
                                                zeros(state_conv.shape[2], cw), v_first, weights[l],
                                                n_heads, n_xheads)
        x1, q, xn, cn, carried = _sample_mixer(l, xs, state_shift, wkv_t, state_conv, carried, weights[l],
                                               n_heads, n_xheads)
        xp, fs, o = _prompt_ffn(l, x2, zeros(state_ffn.shape[2], f2), q, cache_mem_k, cache_mem_v,
                                weights[l], final)
        xs, ffn_s = _sample_ffn(l, x1, o, state_ffn, ffn_s, weights[l], final)
        shs.append(sh)
        wks.append(s_)
        cvs.append(cs)
        ffs.append(fs)
        xns.append(xn)
        cns.append(cn)
    y_p, x = xp, xs
    wkv_s = jnp.transpose(carried[1], (0, 4, 1, 2, 3))
    return (y_p, x.reshape(x_sample.shape), jnp.stack(shs), jnp.stack(wks), jnp.stack(cvs), jnp.stack(ffs),
            mem_k_p, mem_v_p, jnp.stack(xns), wkv_s, jnp.stack(cns), ffn_s)
```
